```python
import math
import jax, jax.numpy as jnp
from jax import lax
import numpy as np

D_MODEL = 2048
BATCH = 1
SEQ = 16384
DEPTH = 2

HEAD_DIM = 128
N_HEADS_A = 8
DILATED_PATTERNS = ((128, 1), (512, 4), (2048, 16))
BLK = 128
N_BUCKETS = 32
MAX_DISTANCE = 2048
N_HEADS_B = 8
Q_LORA = 512
KV_LORA = 256
NOPE_DIM = 128
ROPE_DIM = 64
V_DIM = 128
QK_DIM_B = NOPE_DIM + ROPE_DIM
ROPE_THETA = 10000.0
WIDTH_A = N_HEADS_A * HEAD_DIM
WIDTH_B = N_HEADS_B * V_DIM
MIX_WIDTH = WIDTH_A + WIDTH_B
IN_COLS = 3 * WIDTH_A + Q_LORA + KV_LORA + ROPE_DIM
D_FF = 5632
N_EXPERTS = 8
TOP_K = 2
D_FF_EXPERT = 5632
N_DENSE = (DEPTH + 1) // 2
N_MOE = DEPTH // 2
EPS = 1e-6
NEG = -1e30

kernel_name = "hybrid_dilated_swa_mla_moe_block"


def rms_norm(x, g):
    xf = x.astype(jnp.float32)
    y = xf * lax.rsqrt(jnp.mean(xf * xf, axis=-1, keepdims=True) + EPS)
    return (y * g.astype(jnp.float32)).astype(x.dtype)


def t5_bucket(n):
    max_exact = N_BUCKETS // 2
    nf = jnp.maximum(n, 1).astype(jnp.float32)
    large = max_exact + (jnp.log(nf / max_exact) / math.log(MAX_DISTANCE / max_exact)
                         * (N_BUCKETS - max_exact)).astype(jnp.int32)
    large = jnp.minimum(large, N_BUCKETS - 1)
    return jnp.where(n < max_exact, n, large)


def dilated_window_attn(q, k, v, rel_bias, window, dilation):
    B, S, H, Dh = q.shape
    span = dilation * BLK
    L = -(-S // span) * span
    M = L // dilation
    nb = M // BLK
    steps = window // dilation

    def to_classes(t):
        t = jnp.pad(t, ((0, 0), (0, L - S), (0, 0), (0, 0)))
        t = t.reshape(B, M, dilation, H, Dh).transpose(0, 2, 3, 1, 4)
        return t.reshape(B, dilation, H, nb, BLK, Dh)

    qb, kb, vb = to_classes(q), to_classes(k), to_classes(v)
    prev = lambda t: jnp.pad(t, ((0, 0), (0, 0), (0, 0), (1, 0), (0, 0), (0, 0)))[:, :, :, :-1]
    kb2 = jnp.concatenate([prev(kb), kb], axis=4)
    vb2 = jnp.concatenate([prev(vb), vb], axis=4)

    i = jnp.arange(BLK)[:, None]
    j = jnp.arange(2 * BLK)[None, :]
    delta = i + BLK - j
    band = (delta >= 0) & (delta <= steps)
    bucket = t5_bucket(jnp.maximum(delta, 0) * dilation)
    bias = rel_bias[bucket].astype(jnp.float32).transpose(2, 0, 1)
    blk_valid = (jnp.arange(nb)[:, None, None] > 0) | (j[None] >= BLK)
    mask = band[None] & blk_valid

    s = jnp.einsum('bdhnqc,bdhnkc->bdhnqk', qb, kb2).astype(jnp.float32) * (Dh ** -0.5)
    s = jnp.where(mask, s + bias[:, None], NEG)
    m = jnp.max(s, axis=-1, keepdims=True)
    p = jnp.exp(s - m)
    den = jnp.sum(p, axis=-1, keepdims=True)
    o = jnp.einsum('bdhnqk,bdhnkc->bdhnqc', p.astype(v.dtype), vb2).astype(jnp.float32) / den
    lse = (m + jnp.log(den))[..., 0]

    o = o.reshape(B, dilation, H, M, Dh).transpose(0, 3, 1, 2, 4).reshape(B, L, H, Dh)[:, :S]
    lse = lse.reshape(B, dilation, H, M).transpose(0, 3, 1, 2).reshape(B, L, H)[:, :S]
    return o, lse


def dilated_mixer(qa, ka, va, q_g, k_g, rel_bias):
    B, S, _ = qa.shape
    sh = (B, S, N_HEADS_A, HEAD_DIM)
    q = rms_norm(qa.reshape(sh), q_g)
    k = rms_norm(ka.reshape(sh), k_g)
    v = va.reshape(sh)
    outs, lses = [], []
    for window, dilation in DILATED_PATTERNS:
        o, l = dilated_window_attn(q, k, v, rel_bias, window, dilation)
        outs.append(o)
        lses.append(l)
    w = jax.nn.softmax(jnp.stack(lses, axis=0), axis=0)[..., None]
    out = jnp.sum(w * jnp.stack(outs, axis=0), axis=0)
    return out.reshape(B, S, WIDTH_A).astype(qa.dtype)


def rope(x, positions):
    half = ROPE_DIM // 2
    inv = ROPE_THETA ** (-jnp.arange(half, dtype=jnp.float32) / half)
    ang = positions.astype(jnp.float32)[..., None] * inv
    cos = jnp.cos(ang)[:, :, None, :]
    sin = jnp.sin(ang)[:, :, None, :]
    xf = x.astype(jnp.float32)
    x1, x2 = xf[..., :half], xf[..., half:]
    return jnp.concatenate([x1 * cos - x2 * sin, x1 * sin + x2 * cos], axis=-1).astype(x.dtype)


def causal_block_attn(q, k, v, scale):
    B, S, H, Dq = q.shape
    nq = S // BLK
    qb = q.reshape(B, nq, BLK, H, Dq).transpose(1, 0, 2, 3, 4)
    kpos = jnp.arange(S)

    def one_block(args):
        qi, bi = args
        s = jnp.einsum('bqhc,bkhc->bhqk', qi, k).astype(jnp.float32) * scale
        qpos = bi * BLK + jnp.arange(BLK)
        s = jnp.where(kpos[None, :] <= qpos[:, None], s, NEG)
        p = jax.nn.softmax(s, axis=-1)
        return jnp.einsum('bhqk,bkhc->bqhc', p.astype(v.dtype), v)

    out = lax.map(one_block, (qb, jnp.arange(nq)))
    return out.transpose(1, 0, 2, 3, 4).reshape(B, S, H, v.shape[-1])


def mla_mixer(cq, ckv, kr, positions, q_a_g, kv_a_g, w_q_b, w_kv_b, q_g, k_g):
    B, S, _ = cq.shape
    q = (rms_norm(cq, q_a_g) @ w_q_b).reshape(B, S, N_HEADS_B, QK_DIM_B)
    kv = (rms_norm(ckv, kv_a_g) @ w_kv_b).reshape(B, S, N_HEADS_B, NOPE_DIM + V_DIM)
    k_nope, v = kv[..., :NOPE_DIM], kv[..., NOPE_DIM:]
    k_rope = jnp.broadcast_to(kr[:, :, None, :], (B, S, N_HEADS_B, ROPE_DIM))
    k = jnp.concatenate([k_nope, k_rope], axis=-1)
    q = rms_norm(q, q_g)
    k = rms_norm(k, k_g)
    q = jnp.concatenate([q[..., :NOPE_DIM], rope(q[..., NOPE_DIM:], positions)], axis=-1)
    k = jnp.concatenate([k[..., :NOPE_DIM], rope(k[..., NOPE_DIM:], positions)], axis=-1)
    out = causal_block_attn(q, k, v, QK_DIM_B ** -0.5)
    return out.reshape(B, S, WIDTH_B)


def swiglu(h, wg, wu, wd):
    return (jax.nn.silu(h @ wg) * (h @ wu)) @ wd


def moe_swiglu(h, w_router, w_gate, w_up, w_down):
    logits = (h @ w_router).astype(jnp.float32)
    top_val, top_idx = lax.top_k(logits, TOP_K)
    gates = jax.nn.softmax(top_val, axis=-1)
    combine = jnp.sum(jax.nn.one_hot(top_idx, N_EXPERTS, dtype=jnp.float32)
                      * gates[..., None], axis=-2)
    out = jnp.zeros_like(h)
    for e in range(N_EXPERTS):
        out = out + combine[..., e:e + 1].astype(h.dtype) * swiglu(h, w_gate[e], w_up[e], w_down[e])
    return out


def setup_inputs(seed: int = 0) -> dict:
    key = jax.random.key(seed)
    ks = jax.random.split(key, 24)
    f32 = jnp.float32
    nrm = lambda k, shape, fan_in: jax.random.normal(k, shape, f32) * (fan_in ** -0.5)
    gain = lambda k, shape: 1.0 + 0.02 * jax.random.normal(k, shape, f32)
    start = jax.random.randint(ks[1], (BATCH, 1), 0, 4096, dtype=jnp.int32)
    positions = start + jnp.arange(SEQ, dtype=jnp.int32)[None, :]
    return {
        "x": jax.random.normal(ks[0], (BATCH, SEQ, D_MODEL), f32),
        "positions": positions,
        "rel_bias_table": 0.5 * jax.random.normal(ks[2], (N_BUCKETS, N_HEADS_A), f32),
        "norm_mix_g": gain(ks[3], (DEPTH, D_MODEL)),
        "w_in": nrm(ks[4], (DEPTH, D_MODEL, IN_COLS), D_MODEL),
        "q_a_norm_g": gain(ks[5], (DEPTH, Q_LORA)),
        "kv_a_norm_g": gain(ks[6], (DEPTH, KV_LORA)),
        "w_q_b": nrm(ks[7], (DEPTH, Q_LORA, N_HEADS_B * QK_DIM_B), Q_LORA),
        "w_kv_b": nrm(ks[8], (DEPTH, KV_LORA, N_HEADS_B * (NOPE_DIM + V_DIM)), KV_LORA),
        "q_norm_a_g": gain(ks[9], (DEPTH, HEAD_DIM)),
        "k_norm_a_g": gain(ks[10], (DEPTH, HEAD_DIM)),
        "q_norm_b_g": gain(ks[11], (DEPTH, QK_DIM_B)),
        "k_norm_b_g": gain(ks[12], (DEPTH, QK_DIM_B)),
        "w_out": nrm(ks[13], (DEPTH, MIX_WIDTH, D_MODEL), MIX_WIDTH),
        "norm_ffn_g": gain(ks[14], (DEPTH, D_MODEL)),
        "w_ff_gate": nrm(ks[15], (N_DENSE, D_MODEL, D_FF), D_MODEL),
        "w_ff_up": nrm(ks[16], (N_DENSE, D_MODEL, D_FF), D_MODEL),
        "w_ff_down": nrm(ks[17], (N_DENSE, D_FF, D_MODEL), D_FF),
        "w_router": nrm(ks[18], (N_MOE, D_MODEL, N_EXPERTS), D_MODEL),
        "w_exp_gate": nrm(ks[19], (N_MOE, N_EXPERTS, D_MODEL, D_FF_EXPERT), D_MODEL),
        "w_exp_up": nrm(ks[20], (N_MOE, N_EXPERTS, D_MODEL, D_FF_EXPERT), D_MODEL),
        "w_exp_down": nrm(ks[21], (N_MOE, N_EXPERTS, D_FF_EXPERT, D_MODEL), D_FF_EXPERT),
    }


def reference(x, positions, rel_bias_table, norm_mix_g, w_in, q_a_norm_g, kv_a_norm_g,
              w_q_b, w_kv_b, q_norm_a_g, k_norm_a_g, q_norm_b_g, k_norm_b_g, w_out,
              norm_ffn_g, w_ff_gate, w_ff_up, w_ff_down, w_router, w_exp_gate,
              w_exp_up, w_exp_down):
    splits = [WIDTH_A, 2 * WIDTH_A, 3 * WIDTH_A, 3 * WIDTH_A + Q_LORA,
              3 * WIDTH_A + Q_LORA + KV_LORA]
    for l in range(DEPTH):
        h = rms_norm(x, norm_mix_g[l])
        proj = h @ w_in[l]
        qa, ka, va, cq, ckv, kr = jnp.split(proj, splits, axis=-1)
        a = dilated_mixer(qa, ka, va, q_norm_a_g[l], k_norm_a_g[l], rel_bias_table)
        b = mla_mixer(cq, ckv, kr, positions, q_a_norm_g[l], kv_a_norm_g[l],
                      w_q_b[l], w_kv_b[l], q_norm_b_g[l], k_norm_b_g[l])
        x = x + jnp.concatenate([a, b.astype(a.dtype)], axis=-1) @ w_out[l]
        h2 = rms_norm(x, norm_ffn_g[l])
        if l % 2 == 0:
            f = swiglu(h2, w_ff_gate[l // 2], w_ff_up[l // 2], w_ff_down[l // 2])
        else:
            f = moe_swiglu(h2, w_router[l // 2], w_exp_gate[l // 2], w_exp_up[l // 2],
                           w_exp_down[l // 2])
        x = x + f
    return x
```

```python
import functools
import math

import jax
import jax.numpy as jnp
from jax import lax
from jax.experimental import pallas as pl
from jax.experimental.pallas import tpu as pltpu

F32 = jnp.float32
BF16 = jnp.bfloat16

D_MODEL = 2048
HEAD_DIM = 128
N_HEADS_A = 8
DILATED_PATTERNS = ((128, 1), (512, 4), (2048, 16))
BLK = 128
N_BUCKETS = 32
MAX_DISTANCE = 2048
N_HEADS_B = 8
Q_LORA = 512
KV_LORA = 256
NOPE_DIM = 128
ROPE_DIM = 64
V_DIM = 128
QK_DIM_B = NOPE_DIM + ROPE_DIM
QK_PAD_B = 256
ROPE_THETA = 10000.0
WIDTH_A = N_HEADS_A * HEAD_DIM
WIDTH_B = N_HEADS_B * V_DIM
IN_COLS = 3 * WIDTH_A + Q_LORA + KV_LORA + ROPE_DIM
PROJ_COLS = 4096
D_FF = 5632
N_EXPERTS = 8
EPS = 1e-6
NEG = -1e30

SUPER = 2048
VMEM_LIMIT = 56 * 1024 * 1024


def _params(sem, vmem=VMEM_LIMIT):
    return pltpu.CompilerParams(dimension_semantics=sem, vmem_limit_bytes=vmem)


def _rms(x, g, n=None):
    ss = jnp.sum(x * x, axis=-1, keepdims=True)
    n = x.shape[-1] if n is None else n
    return x * lax.rsqrt(ss * (1.0 / n) + EPS) * g


def _norm_matmul_kernel(x_ref, g_ref, w_ref, o_ref, hn_ref):
    @pl.when(pl.program_id(1) == 0)
    def _():
        hn_ref[...] = _rms(x_ref[...], g_ref[...]).astype(BF16)

    o_ref[...] = jnp.dot(hn_ref[...], w_ref[...], preferred_element_type=F32).astype(o_ref.dtype)


def norm_matmul(x, g, w, *, tm, tn, out_dtype):
    s, k = x.shape
    n = w.shape[1]
    return pl.pallas_call(
        _norm_matmul_kernel,
        grid=(s // tm, n // tn),
        in_specs=[
            pl.BlockSpec((tm, k), lambda i, j: (i, 0)),
            pl.BlockSpec((1, k), lambda i, j: (0, 0)),
            pl.BlockSpec((k, tn), lambda i, j: (0, j)),
        ],
        out_specs=pl.BlockSpec((tm, tn), lambda i, j: (i, j)),
        out_shape=jax.ShapeDtypeStruct((s, n), out_dtype),
        scratch_shapes=[pltpu.VMEM((tm, k), BF16)],
        compiler_params=_params(("parallel", "arbitrary")),
        name="norm_matmul",
    )(x, g, w)


def _t5_bucket(n):
    max_exact = N_BUCKETS // 2
    nf = jnp.maximum(n, 1).astype(F32)
    large = max_exact + (jnp.log(nf / max_exact) / math.log(MAX_DISTANCE / max_exact)
                         * (N_BUCKETS - max_exact)).astype(jnp.int32)
    large = jnp.minimum(large, N_BUCKETS - 1)
    return jnp.where(n < max_exact, n, large)


def _dilated_bias(rel_bias):
    i = jnp.arange(BLK)[:, None]
    j = jnp.arange(2 * BLK)[None, :]
    delta = i + BLK - j
    out = []
    for window, dilation in DILATED_PATTERNS:
        band = (delta >= 0) & (delta <= window // dilation)
        bucket = _t5_bucket(jnp.maximum(delta, 0) * dilation)
        bias = rel_bias[bucket].astype(F32).transpose(2, 0, 1)
        out.append(jnp.where(band[None], bias, NEG))
    return jnp.stack(out, axis=0)


def _dilated_kernel(q_ref, kp_ref, kc_ref, vp_ref, vc_ref, bias_ref, qg_ref, kg_ref, o_ref,
                    qn_s, kn_s, v_s, op_s, lse_s):
    sb = pl.program_id(1)
    qn_s[...] = _rms(q_ref[...], qg_ref[...]) * (HEAD_DIM ** -0.5)
    kn_s[0:SUPER, :] = _rms(kp_ref[...], kg_ref[...])
    kn_s[SUPER:2 * SUPER, :] = _rms(kc_ref[...], kg_ref[...])
    v_s[0:SUPER, :] = vp_ref[...]
    v_s[SUPER:2 * SUPER, :] = vc_ref[...]
    col = lax.broadcasted_iota(jnp.int32, (BLK, 2 * BLK), 1)

    for pi, (_, d) in enumerate(DILATED_PATTERNS):
        def rows(start, size, d=d):
            return pl.ds(start, size) if d == 1 else pl.ds(start, size, stride=d)

        def body(idx, carry, pi=pi, d=d, rows=rows):
            r = idx % d
            b = idx // d
            q0 = r + d * BLK * b
            k0 = SUPER + q0 - d * BLK
            qb = qn_s[rows(q0, BLK), :].astype(BF16)
            k2 = kn_s[rows(k0, 2 * BLK), :].astype(BF16)
            v2 = v_s[rows(k0, 2 * BLK), :].astype(BF16)
            s = lax.dot_general(qb, k2, (((1,), (1,)), ((), ())), preferred_element_type=F32)
            s = s + bias_ref[pi, 0]
            first = jnp.logical_and(sb == 0, b == 0)
            s = jnp.where(jnp.logical_and(first, col < BLK), NEG, s)
            m = jnp.max(s, axis=-1, keepdims=True)
            p = jnp.exp(s - m)
            den = jnp.sum(p, axis=-1, keepdims=True)
            o = jnp.dot(p.astype(BF16), v2, preferred_element_type=F32) / den
            lse = m + jnp.log(den)
            op_s[pi, rows(q0, BLK), :] = o
            lse_s[pi, rows(q0, BLK), :] = jnp.broadcast_to(lse, (BLK, HEAD_DIM))
            return carry

        lax.fori_loop(0, SUPER // BLK, body, 0)

    chunk = 256
    for c in range(SUPER // chunk):
        sl = slice(c * chunk, (c + 1) * chunk)
        l0, l1, l2 = lse_s[0, sl, :], lse_s[1, sl, :], lse_s[2, sl, :]
        mx = jnp.maximum(jnp.maximum(l0, l1), l2)
        e0, e1, e2 = jnp.exp(l0 - mx), jnp.exp(l1 - mx), jnp.exp(l2 - mx)
        num = e0 * op_s[0, sl, :] + e1 * op_s[1, sl, :] + e2 * op_s[2, sl, :]
        o_ref[sl, :] = (num / (e0 + e1 + e2)).astype(o_ref.dtype)


def dilated_mixer(proj, bias, q_g, k_g):
    s = proj.shape[0]
    nsb = s // SUPER
    h8 = N_HEADS_A
    blk = (SUPER, HEAD_DIM)
    return pl.pallas_call(
        _dilated_kernel,
        grid=(h8, nsb),
        in_specs=[
            pl.BlockSpec(blk, lambda h, i: (i, h)),
            pl.BlockSpec(blk, lambda h, i: (jnp.maximum(i - 1, 0), h8 + h)),
            pl.BlockSpec(blk, lambda h, i: (i, h8 + h)),
            pl.BlockSpec(blk, lambda h, i: (jnp.maximum(i - 1, 0), 2 * h8 + h)),
            pl.BlockSpec(blk, lambda h, i: (i, 2 * h8 + h)),
            pl.BlockSpec((len(DILATED_PATTERNS), 1, BLK, 2 * BLK), lambda h, i: (0, h, 0, 0)),
            pl.BlockSpec((1, HEAD_DIM), lambda h, i: (0, 0)),
            pl.BlockSpec((1, HEAD_DIM), lambda h, i: (0, 0)),
        ],
        out_specs=pl.BlockSpec(blk, lambda h, i: (i, h)),
        out_shape=jax.ShapeDtypeStruct((s, WIDTH_A), BF16),
        scratch_shapes=[
            pltpu.VMEM((SUPER, HEAD_DIM), F32),
            pltpu.VMEM((2 * SUPER, HEAD_DIM), F32),
            pltpu.VMEM((2 * SUPER, HEAD_DIM), F32),
            pltpu.VMEM((len(DILATED_PATTERNS), SUPER, HEAD_DIM), F32),
            pltpu.VMEM((len(DILATED_PATTERNS), SUPER, HEAD_DIM), F32),
        ],
        compiler_params=_params(("parallel", "arbitrary")),
        name="dilated_mixer",
    )(proj, proj, proj, proj, proj, bias, q_g, k_g)


def _mla_prep_kernel(cq_ref, ckv_ref, kr_ref, qag_ref, kvag_ref, wq_ref, wk_ref, wv_ref,
                     qg_ref, kg_ref, c_ref, s1_ref, s2_ref, q_out, k_out, v_out):
    c, s1, s2 = c_ref[...], s1_ref[...], s2_ref[...]

    def rope(x):
        return x * c + pltpu.roll(x, 96, 1) * s1 + pltpu.roll(x, 32, 1) * s2

    cqn = _rms(cq_ref[...], qag_ref[...]).astype(BF16)
    ckvn = _rms(ckv_ref[...], kvag_ref[...]).astype(BF16)
    qpre = jnp.dot(cqn, wq_ref[...], preferred_element_type=F32)
    knope = jnp.dot(ckvn, wk_ref[...], preferred_element_type=F32)
    v = jnp.dot(ckvn, wv_ref[...], preferred_element_type=F32)
    kr = kr_ref[...]
    kr_ss = jnp.sum(kr * kr, axis=-1, keepdims=True)
    qg, kg = qg_ref[...], kg_ref[...]
    kr_roped = rope(kr * kg[:, NOPE_DIM:])
    inv_n = 1.0 / QK_DIM_B
    for h in range(N_HEADS_B):
        qh = qpre[:, h * QK_PAD_B:(h + 1) * QK_PAD_B]
        rs = lax.rsqrt(jnp.sum(qh * qh, axis=-1, keepdims=True) * inv_n + EPS)
        q_out[h, :, 0:NOPE_DIM] = (qh[:, :NOPE_DIM] * rs * qg[:, :NOPE_DIM]).astype(BF16)
        q_out[h, :, NOPE_DIM:QK_PAD_B] = rope(qh[:, NOPE_DIM:] * rs * qg[:, NOPE_DIM:]).astype(BF16)
        kh = knope[:, h * NOPE_DIM:(h + 1) * NOPE_DIM]
        rs = lax.rsqrt((jnp.sum(kh * kh, axis=-1, keepdims=True) + kr_ss) * inv_n + EPS)
        k_out[h, :, 0:NOPE_DIM] = (kh * rs * kg[:, :NOPE_DIM]).astype(BF16)
        k_out[h, :, NOPE_DIM:QK_PAD_B] = (kr_roped * rs).astype(BF16)
        v_out[h, :, :] = v[:, h * V_DIM:(h + 1) * V_DIM].astype(BF16)


def mla_prep(proj, q_a_g, kv_a_g, wq, wk, wv, qg, kg, rope_c, rope_s1, rope_s2, *, tm):
    s = proj.shape[0]
    hb = N_HEADS_B
    full = lambda shape: pl.BlockSpec(shape, lambda i: (0,) * len(shape))
    cq_blk = (3 * WIDTH_A) // Q_LORA
    ckv_blk = (3 * WIDTH_A + Q_LORA) // KV_LORA
    kr_blk = (3 * WIDTH_A + Q_LORA + KV_LORA) // 128
    return pl.pallas_call(
        _mla_prep_kernel,
        grid=(s // tm,),
        in_specs=[
            pl.BlockSpec((tm, Q_LORA), lambda i: (i, cq_blk)),
            pl.BlockSpec((tm, KV_LORA), lambda i: (i, ckv_blk)),
            pl.BlockSpec((tm, 128), lambda i: (i, kr_blk)),
            full((1, Q_LORA)), full((1, KV_LORA)),
            full(wq.shape), full(wk.shape), full(wv.shape),
            full((1, QK_PAD_B)), full((1, QK_PAD_B)),
            pl.BlockSpec((tm, 128), lambda i: (i, 0)),
            pl.BlockSpec((tm, 128), lambda i: (i, 0)),
            pl.BlockSpec((tm, 128), lambda i: (i, 0)),
        ],
        out_specs=[
            pl.BlockSpec((hb, tm, QK_PAD_B), lambda i: (0, i, 0)),
            pl.BlockSpec((hb, tm, QK_PAD_B), lambda i: (0, i, 0)),
            pl.BlockSpec((hb, tm, V_DIM), lambda i: (0, i, 0)),
        ],
        out_shape=[
            jax.ShapeDtypeStruct((hb, s, QK_PAD_B), BF16),
            jax.ShapeDtypeStruct((hb, s, QK_PAD_B), BF16),
            jax.ShapeDtypeStruct((hb, s, V_DIM), BF16),
        ],
        compiler_params=_params(("parallel",)),
        name="mla_prep",
    )(proj, proj, proj, q_a_g, kv_a_g, wq, wk, wv, qg, kg, rope_c, rope_s1, rope_s2)


def _flash_kernel(q_ref, k_ref, v_ref, o_ref, m_s, l_s, acc_s, *, tq):
    i = pl.program_id(1)
    q = q_ref[0]
    m_s[...] = jnp.full(m_s.shape, NEG, F32)
    l_s[...] = jnp.zeros(l_s.shape, F32)
    acc_s[...] = jnp.zeros(acc_s.shape, F32)

    def step(j, masked):
        start = pl.multiple_of(j * tq, tq)
        k = k_ref[0, pl.ds(start, tq), :]
        v = v_ref[0, pl.ds(start, tq), :]
        s = lax.dot_general(q, k, (((1,), (1,)), ((), ())), preferred_element_type=F32)
        if masked:
            row = lax.broadcasted_iota(jnp.int32, (tq, tq), 0)
            colk = lax.broadcasted_iota(jnp.int32, (tq, tq), 1)
            s = jnp.where(colk <= row, s, NEG)
        m_prev = m_s[...]
        m_new = jnp.maximum(m_prev, jnp.max(s, axis=-1, keepdims=True))
        alpha = jnp.exp(m_prev - m_new)
        p = jnp.exp(s - m_new)
        l_s[...] = alpha * l_s[...] + jnp.sum(p, axis=-1, keepdims=True)
        acc_s[...] = alpha * acc_s[...] + jnp.dot(p.astype(BF16), v, preferred_element_type=F32)
        m_s[...] = m_new

    def body(j, carry):
        step(j, False)
        return carry

    lax.fori_loop(0, i, body, 0)
    step(i, True)
    o_ref[...] = (acc_s[...] / l_s[...]).astype(o_ref.dtype)


def mla_flash(q, k, v, *, tq):
    hb, s, _ = q.shape
    return pl.pallas_call(
        functools.partial(_flash_kernel, tq=tq),
        grid=(hb, s // tq),
        in_specs=[
            pl.BlockSpec((1, tq, QK_PAD_B), lambda h, i: (h, i, 0)),
            pl.BlockSpec((1, s, QK_PAD_B), lambda h, i: (h, 0, 0)),
            pl.BlockSpec((1, s, V_DIM), lambda h, i: (h, 0, 0)),
        ],
        out_specs=pl.BlockSpec((tq, V_DIM), lambda h, i: (i, h)),
        out_shape=jax.ShapeDtypeStruct((s, WIDTH_B), BF16),
        scratch_shapes=[
            pltpu.VMEM((tq, 1), F32),
            pltpu.VMEM((tq, 1), F32),
            pltpu.VMEM((tq, V_DIM), F32),
        ],
        compiler_params=_params(("parallel", "arbitrary")),
        name="mla_flash",
    )(q, k, v)


def _out_proj_kernel(x_ref, a_ref, b_ref, wa_ref, wb_ref, o_ref):
    acc = jnp.dot(a_ref[...], wa_ref[...], preferred_element_type=F32)
    acc = acc + jnp.dot(b_ref[...], wb_ref[...], preferred_element_type=F32)
    o_ref[...] = x_ref[...] + acc


def out_proj(x, a, b, wa, wb, *, tm, tn):
    s, n = x.shape
    ka, kb = a.shape[1], b.shape[1]
    return pl.pallas_call(
        _out_proj_kernel,
        grid=(s // tm, n // tn),
        in_specs=[
            pl.BlockSpec((tm, tn), lambda i, j: (i, j)),
            pl.BlockSpec((tm, ka), lambda i, j: (i, 0)),
            pl.BlockSpec((tm, kb), lambda i, j: (i, 0)),
            pl.BlockSpec((ka, tn), lambda i, j: (0, j)),
            pl.BlockSpec((kb, tn), lambda i, j: (0, j)),
        ],
        out_specs=pl.BlockSpec((tm, tn), lambda i, j: (i, j)),
        out_shape=jax.ShapeDtypeStruct((s, n), F32),
        compiler_params=_params(("parallel", "arbitrary")),
        name="out_proj",
    )(x, a, b, wa, wb)


def _ffn_kernel(x_ref, g_ref, comb_ref, wg_ref, wu_ref, wd_ref, o_ref, hn_ref, *, n_experts):
    e = pl.program_id(1)
    f = pl.program_id(2)

    @pl.when(jnp.logical_and(e == 0, f == 0))
    def _():
        hn_ref[...] = _rms(x_ref[...], g_ref[...]).astype(BF16)
        o_ref[...] = x_ref[...]

    hn = hn_ref[...]
    gate = jnp.dot(hn, wg_ref[0], preferred_element_type=F32)
    up = jnp.dot(hn, wu_ref[0], preferred_element_type=F32)
    act = gate * jax.nn.sigmoid(gate) * up
    if n_experts > 1:
        lane = lax.broadcasted_iota(jnp.int32, comb_ref.shape, 1)
        ce = jnp.sum(jnp.where(lane == e, comb_ref[...], 0.0), axis=-1, keepdims=True)
        act = act * ce
    o_ref[...] += jnp.dot(act.astype(BF16), wd_ref[0], preferred_element_type=F32)


def ffn(x, g, comb, wg, wu, wd, *, tm, tf):
    s, d = x.shape
    ne, _, fdim = wg.shape
    return pl.pallas_call(
        functools.partial(_ffn_kernel, n_experts=ne),
        grid=(s // tm, ne, fdim // tf),
        in_specs=[
            pl.BlockSpec((tm, d), lambda i, e, f: (i, 0)),
            pl.BlockSpec((1, d), lambda i, e, f: (0, 0)),
            pl.BlockSpec((tm, 128), lambda i, e, f: (i, 0)),
            pl.BlockSpec((1, d, tf), lambda i, e, f: (e, 0, f)),
            pl.BlockSpec((1, d, tf), lambda i, e, f: (e, 0, f)),
            pl.BlockSpec((1, tf, d), lambda i, e, f: (e, f, 0)),
        ],
        out_specs=pl.BlockSpec((tm, d), lambda i, e, f: (i, 0)),
        out_shape=jax.ShapeDtypeStruct((s, d), F32),
        scratch_shapes=[pltpu.VMEM((tm, d), BF16)],
        compiler_params=_params(("parallel", "arbitrary", "arbitrary")),
        name="ffn",
    )(x, g, comb, wg, wu, wd)


def _router_kernel(x_ref, g_ref, wr_ref, comb_ref):
    hn = _rms(x_ref[...], g_ref[...])
    logits = jnp.dot(hn, wr_ref[...], preferred_element_type=F32, precision=lax.Precision.HIGHEST)
    lane = lax.broadcasted_iota(jnp.int32, logits.shape, 1)
    logits = jnp.where(lane < N_EXPERTS, logits, NEG)
    v1 = jnp.max(logits, axis=-1, keepdims=True)
    i1 = jnp.min(jnp.where(logits == v1, lane, 128), axis=-1, keepdims=True)
    rest = jnp.where(lane == i1, NEG, logits)
    v2 = jnp.max(rest, axis=-1, keepdims=True)
    i2 = jnp.min(jnp.where(rest == v2, lane, 128), axis=-1, keepdims=True)
    e2 = jnp.exp(v2 - v1)
    g1 = 1.0 / (1.0 + e2)
    g2 = e2 / (1.0 + e2)
    comb_ref[...] = jnp.where(lane == i1, g1, 0.0) + jnp.where(lane == i2, g2, 0.0)


def router(x, g, wr, *, tm):
    s, d = x.shape
    return pl.pallas_call(
        _router_kernel,
        grid=(s // tm,),
        in_specs=[
            pl.BlockSpec((tm, d), lambda i: (i, 0)),
            pl.BlockSpec((1, d), lambda i: (0, 0)),
            pl.BlockSpec((d, 128), lambda i: (0, 0)),
        ],
        out_specs=pl.BlockSpec((tm, 128), lambda i: (i, 0)),
        out_shape=jax.ShapeDtypeStruct((s, 128), F32),
        compiler_params=_params(("parallel",)),
        name="router",
    )(x, g, wr)


def _pad_cols(w, n):
    return jnp.pad(w, ((0, 0), (0, n - w.shape[1])))


def _mla_weights(w_q_b, w_kv_b):
    wq = w_q_b.reshape(Q_LORA, N_HEADS_B, QK_DIM_B)
    wq = jnp.pad(wq, ((0, 0), (0, 0), (0, QK_PAD_B - QK_DIM_B))).reshape(Q_LORA, N_HEADS_B * QK_PAD_B)
    wkv = w_kv_b.reshape(KV_LORA, N_HEADS_B, NOPE_DIM + V_DIM)
    wk = wkv[:, :, :NOPE_DIM].reshape(KV_LORA, N_HEADS_B * NOPE_DIM)
    wv = wkv[:, :, NOPE_DIM:].reshape(KV_LORA, N_HEADS_B * V_DIM)
    return wq.astype(BF16), wk.astype(BF16), wv.astype(BF16)


def _rope_tables(positions):
    half = ROPE_DIM // 2
    inv = ROPE_THETA ** (-jnp.arange(half, dtype=F32) / half)
    ang = positions.astype(F32)[:, None] * inv
    cos, sin = jnp.cos(ang), jnp.sin(ang)
    z = jnp.zeros_like(cos)
    c = jnp.concatenate([cos, cos, z, z], axis=-1)
    s1 = jnp.concatenate([-sin, z, z, z], axis=-1)
    s2 = jnp.concatenate([z, sin, z, z], axis=-1)
    return c, s1, s2


def kernel(x, positions, rel_bias_table, norm_mix_g, w_in, q_a_norm_g, kv_a_norm_g, w_q_b, w_kv_b, q_norm_a_g, k_norm_a_g, q_norm_b_g, k_norm_b_g, w_out, norm_ffn_g, w_ff_gate, w_ff_up, w_ff_down, w_router, w_exp_gate, w_exp_up, w_exp_down):
    batch, seq, d = x.shape
    depth = w_in.shape[0]
    outs = []
    bias = _dilated_bias(rel_bias_table)
    for bi in range(batch):
        xs = x[bi]
        rope_c, rope_s1, rope_s2 = _rope_tables(positions[bi])
        for l in range(depth):
            w_in_l = _pad_cols(w_in[l], PROJ_COLS).astype(BF16)
            proj = norm_matmul(xs, norm_mix_g[l][None], w_in_l, tm=1024, tn=1024, out_dtype=F32)
            a = dilated_mixer(proj, bias, q_norm_a_g[l][None], k_norm_a_g[l][None])
            wq, wk, wv = _mla_weights(w_q_b[l], w_kv_b[l])
            qg = _pad_cols(q_norm_b_g[l][None] * (QK_DIM_B ** -0.5), QK_PAD_B)
            kg = _pad_cols(k_norm_b_g[l][None], QK_PAD_B)
            qb, kb, vb = mla_prep(proj, q_a_norm_g[l][None], kv_a_norm_g[l][None], wq, wk, wv,
                                  qg, kg, rope_c, rope_s1, rope_s2, tm=512)
            b = mla_flash(qb, kb, vb, tq=512)
            w_out_l = w_out[l].astype(BF16)
            xs = out_proj(xs, a, b, w_out_l[:WIDTH_A], w_out_l[WIDTH_A:], tm=1024, tn=1024)
            gf = norm_ffn_g[l][None]
            if l % 2 == 0:
                i = l // 2
                comb = jnp.ones((seq, 128), F32)
                xs = ffn(xs, gf, comb, w_ff_gate[i][None].astype(BF16), w_ff_up[i][None].astype(BF16),
                         w_ff_down[i][None].astype(BF16), tm=512, tf=512)
            else:
                i = l // 2
                comb = router(xs, gf, _pad_cols(w_router[i], 128), tm=512)
                xs = ffn(xs, gf, comb, w_exp_gate[i].astype(BF16), w_exp_up[i].astype(BF16),
                         w_exp_down[i].astype(BF16), tm=512, tf=512)
        outs.append(xs)
    return jnp.stack(outs, axis=0)
```

```python
import functools
import math

import jax
import jax.numpy as jnp
from jax import lax
from jax.experimental import pallas as pl
from jax.experimental.pallas import tpu as pltpu

F32 = jnp.float32
BF16 = jnp.bfloat16

D_MODEL = 2048
HEAD_DIM = 128
N_HEADS_A = 8
DILATED_PATTERNS = ((128, 1), (512, 4), (2048, 16))
BLK = 128
N_BUCKETS = 32
MAX_DISTANCE = 2048
N_HEADS_B = 8
Q_LORA = 512
KV_LORA = 256
NOPE_DIM = 128
ROPE_DIM = 64
V_DIM = 128
QK_DIM_B = NOPE_DIM + ROPE_DIM
QK_PAD_B = 256
VT_ROWS = V_DIM + 16
LOG2E = math.log2(math.e)
ROPE_THETA = 10000.0
WIDTH_A = N_HEADS_A * HEAD_DIM
WIDTH_B = N_HEADS_B * V_DIM
IN_COLS = 3 * WIDTH_A + Q_LORA + KV_LORA + ROPE_DIM
PROJ_COLS = 4096
D_FF = 5632
N_EXPERTS = 8
EPS = 1e-6
NEG = -1e30

SUPER = 2048
VMEM_LIMIT = 56 * 1024 * 1024


def _params(sem, vmem=VMEM_LIMIT):
    return pltpu.CompilerParams(dimension_semantics=sem, vmem_limit_bytes=vmem)


def _rms(x, g, n=None):
    ss = jnp.sum(x * x, axis=-1, keepdims=True)
    n = x.shape[-1] if n is None else n
    return x * lax.rsqrt(ss * (1.0 / n) + EPS) * g


def _norm_matmul_kernel(x_ref, g_ref, w_ref, o_ref, hn_ref):
    @pl.when(pl.program_id(1) == 0)
    def _():
        hn_ref[...] = _rms(x_ref[...], g_ref[...]).astype(BF16)

    o_ref[...] = jnp.dot(hn_ref[...], w_ref[...], preferred_element_type=F32).astype(o_ref.dtype)


def norm_matmul(x, g, w, *, tm, tn, out_dtype):
    s, k = x.shape
    n = w.shape[1]
    return pl.pallas_call(
        _norm_matmul_kernel,
        grid=(s // tm, n // tn),
        in_specs=[
            pl.BlockSpec((tm, k), lambda i, j: (i, 0)),
            pl.BlockSpec((1, k), lambda i, j: (0, 0)),
            pl.BlockSpec((k, tn), lambda i, j: (0, j)),
        ],
        out_specs=pl.BlockSpec((tm, tn), lambda i, j: (i, j)),
        out_shape=jax.ShapeDtypeStruct((s, n), out_dtype),
        scratch_shapes=[pltpu.VMEM((tm, k), BF16)],
        compiler_params=_params(("parallel", "arbitrary")),
        name="norm_matmul",
    )(x, g, w)


def _t5_bucket(n):
    max_exact = N_BUCKETS // 2
    nf = jnp.maximum(n, 1).astype(F32)
    large = max_exact + (jnp.log(nf / max_exact) / math.log(MAX_DISTANCE / max_exact)
                         * (N_BUCKETS - max_exact)).astype(jnp.int32)
    large = jnp.minimum(large, N_BUCKETS - 1)
    return jnp.where(n < max_exact, n, large)


def _dilated_bias(rel_bias):
    i = jnp.arange(BLK)[:, None]
    j = jnp.arange(2 * BLK)[None, :]
    delta = i + BLK - j
    out = []
    for window, dilation in DILATED_PATTERNS:
        band = (delta >= 0) & (delta <= window // dilation)
        bucket = _t5_bucket(jnp.maximum(delta, 0) * dilation)
        onehot = (bucket[None] == jnp.arange(N_BUCKETS)[:, None, None]).astype(F32)
        bias = jnp.einsum('nij,nh->hij', onehot, rel_bias.astype(F32), precision=lax.Precision.HIGHEST)
        out.append(jnp.where(band[None], bias, NEG))
    return jnp.stack(out, axis=0)


def _dilated_kernel(q_ref, kp_ref, kc_ref, vp_ref, vc_ref, bias_ref, qg_ref, kg_ref, o_ref,
                    qn_s, kn_s, v_s, op_s, lse_s):
    sb = pl.program_id(1)
    qn_s[...] = _rms(q_ref[...], qg_ref[...]) * (HEAD_DIM ** -0.5)
    kn_s[0:SUPER, :] = _rms(kp_ref[...], kg_ref[...])
    kn_s[SUPER:2 * SUPER, :] = _rms(kc_ref[...], kg_ref[...])
    v_s[0:SUPER, :] = vp_ref[...]
    v_s[SUPER:2 * SUPER, :] = vc_ref[...]
    col = lax.broadcasted_iota(jnp.int32, (BLK, 2 * BLK), 1)

    for pi, (_, d) in enumerate(DILATED_PATTERNS):
        def rows(start, size, d=d):
            return pl.ds(start, size) if d == 1 else pl.ds(start, size, stride=d)

        def body(idx, carry, pi=pi, d=d, rows=rows):
            r = idx % d
            b = idx // d
            q0 = r + d * BLK * b
            k0 = SUPER + q0 - d * BLK
            qb = qn_s[rows(q0, BLK), :].astype(BF16)
            k2 = kn_s[rows(k0, 2 * BLK), :].astype(BF16)
            v2 = v_s[rows(k0, 2 * BLK), :].astype(BF16)
            s = lax.dot_general(qb, k2, (((1,), (1,)), ((), ())), preferred_element_type=F32)
            s = s + bias_ref[pi, 0]
            first = jnp.logical_and(sb == 0, b == 0)
            s = jnp.where(jnp.logical_and(first, col < BLK), NEG, s)
            m = jnp.max(s, axis=-1, keepdims=True)
            p = jnp.exp(s - m)
            den = jnp.sum(p, axis=-1, keepdims=True)
            o = jnp.dot(p.astype(BF16), v2, preferred_element_type=F32) / den
            lse = m + jnp.log(den)
            op_s[pi, rows(q0, BLK), :] = o
            lse_s[pi, rows(q0, BLK), :] = jnp.broadcast_to(lse, (BLK, HEAD_DIM))
            return carry

        lax.fori_loop(0, SUPER // BLK, body, 0)

    chunk = 256
    for c in range(SUPER // chunk):
        sl = slice(c * chunk, (c + 1) * chunk)
        l0, l1, l2 = lse_s[0, sl, :], lse_s[1, sl, :], lse_s[2, sl, :]
        mx = jnp.maximum(jnp.maximum(l0, l1), l2)
        e0, e1, e2 = jnp.exp(l0 - mx), jnp.exp(l1 - mx), jnp.exp(l2 - mx)
        num = e0 * op_s[0, sl, :] + e1 * op_s[1, sl, :] + e2 * op_s[2, sl, :]
        o_ref[sl, :] = (num / (e0 + e1 + e2)).astype(o_ref.dtype)


def dilated_mixer(proj, bias, q_g, k_g):
    s = proj.shape[0]
    nsb = s // SUPER
    h8 = N_HEADS_A
    blk = (SUPER, HEAD_DIM)
    return pl.pallas_call(
        _dilated_kernel,
        grid=(h8, nsb),
        in_specs=[
            pl.BlockSpec(blk, lambda h, i: (i, h)),
            pl.BlockSpec(blk, lambda h, i: (jnp.maximum(i - 1, 0), h8 + h)),
            pl.BlockSpec(blk, lambda h, i: (i, h8 + h)),
            pl.BlockSpec(blk, lambda h, i: (jnp.maximum(i - 1, 0), 2 * h8 + h)),
            pl.BlockSpec(blk, lambda h, i: (i, 2 * h8 + h)),
            pl.BlockSpec((len(DILATED_PATTERNS), 1, BLK, 2 * BLK), lambda h, i: (0, h, 0, 0)),
            pl.BlockSpec((1, HEAD_DIM), lambda h, i: (0, 0)),
            pl.BlockSpec((1, HEAD_DIM), lambda h, i: (0, 0)),
        ],
        out_specs=pl.BlockSpec(blk, lambda h, i: (i, h)),
        out_shape=jax.ShapeDtypeStruct((s, WIDTH_A), BF16),
        scratch_shapes=[
            pltpu.VMEM((SUPER, HEAD_DIM), F32),
            pltpu.VMEM((2 * SUPER, HEAD_DIM), F32),
            pltpu.VMEM((2 * SUPER, HEAD_DIM), F32),
            pltpu.VMEM((len(DILATED_PATTERNS), SUPER, HEAD_DIM), F32),
            pltpu.VMEM((len(DILATED_PATTERNS), SUPER, HEAD_DIM), F32),
        ],
        compiler_params=_params(("parallel", "arbitrary")),
        name="dilated_mixer",
    )(proj, proj, proj, proj, proj, bias, q_g, k_g)


def _mla_prep_kernel(cq_ref, ckv_ref, kr_ref, qag_ref, kvag_ref, wq_ref, wk_ref, wv_ref,
                     qg_ref, kg_ref, c_ref, s1_ref, s2_ref, q_out, k_out, v_out):
    c, s1, s2 = c_ref[...], s1_ref[...], s2_ref[...]

    def rope(x):
        return x * c + pltpu.roll(x, 96, 1) * s1 + pltpu.roll(x, 32, 1) * s2

    cqn = _rms(cq_ref[...], qag_ref[...]).astype(BF16)
    ckvn = _rms(ckv_ref[...], kvag_ref[...]).astype(BF16)
    qpre = jnp.dot(cqn, wq_ref[...], preferred_element_type=F32)
    knope = jnp.dot(ckvn, wk_ref[...], preferred_element_type=F32)
    v = jnp.dot(ckvn, wv_ref[...], preferred_element_type=F32)
    kr = kr_ref[...]
    kr_ss = jnp.sum(kr * kr, axis=-1, keepdims=True)
    qg, kg = qg_ref[...], kg_ref[...]
    kr_roped = rope(kr * kg[:, NOPE_DIM:])
    inv_n = 1.0 / QK_DIM_B
    ones = jnp.ones((VT_ROWS - V_DIM, cq_ref.shape[0]), BF16)
    for h in range(N_HEADS_B):
        qh = qpre[:, h * QK_PAD_B:(h + 1) * QK_PAD_B]
        rs = lax.rsqrt(jnp.sum(qh * qh, axis=-1, keepdims=True) * inv_n + EPS)
        q_out[h, 0:NOPE_DIM, :] = (qh[:, :NOPE_DIM] * rs * qg[:, :NOPE_DIM]).T.astype(BF16)
        q_out[h, NOPE_DIM:QK_PAD_B, :] = rope(qh[:, NOPE_DIM:] * rs * qg[:, NOPE_DIM:]).T.astype(BF16)
        kh = knope[:, h * NOPE_DIM:(h + 1) * NOPE_DIM]
        rs = lax.rsqrt((jnp.sum(kh * kh, axis=-1, keepdims=True) + kr_ss) * inv_n + EPS)
        k_out[h, :, 0:NOPE_DIM] = (kh * rs * kg[:, :NOPE_DIM]).astype(BF16)
        k_out[h, :, NOPE_DIM:QK_PAD_B] = (kr_roped * rs).astype(BF16)
        v_out[h, 0:V_DIM, :] = v[:, h * V_DIM:(h + 1) * V_DIM].T.astype(BF16)
        v_out[h, V_DIM:VT_ROWS, :] = ones


def mla_prep(proj, q_a_g, kv_a_g, wq, wk, wv, qg, kg, rope_c, rope_s1, rope_s2, *, tm):
    s = proj.shape[0]
    hb = N_HEADS_B
    full = lambda shape: pl.BlockSpec(shape, lambda i: (0,) * len(shape))
    cq_blk = (3 * WIDTH_A) // Q_LORA
    ckv_blk = (3 * WIDTH_A + Q_LORA) // KV_LORA
    kr_blk = (3 * WIDTH_A + Q_LORA + KV_LORA) // 128
    return pl.pallas_call(
        _mla_prep_kernel,
        grid=(s // tm,),
        in_specs=[
            pl.BlockSpec((tm, Q_LORA), lambda i: (i, cq_blk)),
            pl.BlockSpec((tm, KV_LORA), lambda i: (i, ckv_blk)),
            pl.BlockSpec((tm, 128), lambda i: (i, kr_blk)),
            full((1, Q_LORA)), full((1, KV_LORA)),
            full(wq.shape), full(wk.shape), full(wv.shape),
            full((1, QK_PAD_B)), full((1, QK_PAD_B)),
            pl.BlockSpec((tm, 128), lambda i: (i, 0)),
            pl.BlockSpec((tm, 128), lambda i: (i, 0)),
            pl.BlockSpec((tm, 128), lambda i: (i, 0)),
        ],
        out_specs=[
            pl.BlockSpec((hb, QK_PAD_B, tm), lambda i: (0, 0, i)),
            pl.BlockSpec((hb, tm, QK_PAD_B), lambda i: (0, i, 0)),
            pl.BlockSpec((hb, VT_ROWS, tm), lambda i: (0, 0, i)),
        ],
        out_shape=[
            jax.ShapeDtypeStruct((hb, QK_PAD_B, s), BF16),
            jax.ShapeDtypeStruct((hb, s, QK_PAD_B), BF16),
            jax.ShapeDtypeStruct((hb, VT_ROWS, s), BF16),
        ],
        compiler_params=_params(("parallel",)),
        name="mla_prep",
    )(proj, proj, proj, q_a_g, kv_a_g, wq, wk, wv, qg, kg, rope_c, rope_s1, rope_s2)


def _flash_kernel(qt_ref, k_ref, vt_ref, o_ref, s_a, s_b, m_s, acc_s, *, tq, tk):
    i = pl.program_id(1)
    qt = qt_ref[0]
    m_s[...] = jnp.full(m_s.shape, NEG, F32)
    acc_s[...] = jnp.zeros(acc_s.shape, F32)

    def compute(c, dst):
        start = pl.multiple_of(c * tk, tk)
        dst[...] = jnp.dot(k_ref[0, pl.ds(start, tk), :], qt, preferred_element_type=F32)

    def process(c, src, masked):
        s = src[...]
        if masked:
            key = c * tk + lax.broadcasted_iota(jnp.int32, (tk, tq), 0)
            qry = i * tq + lax.broadcasted_iota(jnp.int32, (tk, tq), 1)
            s = jnp.where(key <= qry, s, NEG)
        m_prev = m_s[...]
        m_new = jnp.maximum(m_prev, jnp.max(s, axis=0, keepdims=True))
        alpha = jnp.exp2(m_prev - m_new)
        p = jnp.exp2(s - m_new).astype(BF16)
        start = pl.multiple_of(c * tk, tk)
        pv = jnp.dot(vt_ref[0, :, pl.ds(start, tk)], p, preferred_element_type=F32)
        acc_s[...] = alpha * acc_s[...] + pv
        m_s[...] = m_new

    assert tq == 2 * tk
    compute(0, s_a)

    def pair(pp, carry):
        c = 2 * pp
        process(c, s_a, False)
        compute(c + 1, s_b)
        process(c + 1, s_b, False)
        compute(c + 2, s_a)
        return carry

    lax.fori_loop(0, i, pair, 0)
    c = 2 * i
    process(c, s_a, True)
    compute(c + 1, s_b)
    process(c + 1, s_b, True)
    acc = acc_s[...]
    o_t = acc[0:V_DIM, :] / acc[V_DIM:V_DIM + 1, :]
    o_ref[...] = o_t.T.astype(o_ref.dtype)


def mla_flash(qt, k, vt, *, tq, tk):
    hb, s, _ = k.shape
    return pl.pallas_call(
        functools.partial(_flash_kernel, tq=tq, tk=tk),
        grid=(hb, s // tq),
        in_specs=[
            pl.BlockSpec((1, QK_PAD_B, tq), lambda h, i: (h, 0, i)),
            pl.BlockSpec((1, s, QK_PAD_B), lambda h, i: (h, 0, 0)),
            pl.BlockSpec((1, VT_ROWS, s), lambda h, i: (h, 0, 0)),
        ],
        out_specs=pl.BlockSpec((tq, V_DIM), lambda h, i: (i, h)),
        out_shape=jax.ShapeDtypeStruct((s, WIDTH_B), BF16),
        scratch_shapes=[
            pltpu.VMEM((tk, tq), F32),
            pltpu.VMEM((tk, tq), F32),
            pltpu.VMEM((1, tq), F32),
            pltpu.VMEM((VT_ROWS, tq), F32),
        ],
        compiler_params=_params(("parallel", "arbitrary")),
        name="mla_flash",
    )(qt, k, vt)


def _out_proj_kernel(x_ref, a_ref, b_ref, wa_ref, wb_ref, o_ref):
    acc = jnp.dot(a_ref[...], wa_ref[...], preferred_element_type=F32)
    acc = acc + jnp.dot(b_ref[...], wb_ref[...], preferred_element_type=F32)
    o_ref[...] = x_ref[...] + acc


def out_proj(x, a, b, wa, wb, *, tm, tn):
    s, n = x.shape
    ka, kb = a.shape[1], b.shape[1]
    return pl.pallas_call(
        _out_proj_kernel,
        grid=(s // tm, n // tn),
        in_specs=[
            pl.BlockSpec((tm, tn), lambda i, j: (i, j)),
            pl.BlockSpec((tm, ka), lambda i, j: (i, 0)),
            pl.BlockSpec((tm, kb), lambda i, j: (i, 0)),
            pl.BlockSpec((ka, tn), lambda i, j: (0, j)),
            pl.BlockSpec((kb, tn), lambda i, j: (0, j)),
        ],
        out_specs=pl.BlockSpec((tm, tn), lambda i, j: (i, j)),
        out_shape=jax.ShapeDtypeStruct((s, n), F32),
        compiler_params=_params(("parallel", "arbitrary")),
        name="out_proj",
    )(x, a, b, wa, wb)


def _ffn_kernel(x_ref, g_ref, comb_ref, wg_ref, wu_ref, wd_ref, o_ref, hn_ref, *, n_experts):
    e = pl.program_id(1)
    f = pl.program_id(2)

    @pl.when(jnp.logical_and(e == 0, f == 0))
    def _():
        hn_ref[...] = _rms(x_ref[...], g_ref[...]).astype(BF16)
        o_ref[...] = x_ref[...]

    hn = hn_ref[...]
    gate = jnp.dot(hn, wg_ref[0], preferred_element_type=F32)
    up = jnp.dot(hn, wu_ref[0], preferred_element_type=F32)
    act = gate * jax.nn.sigmoid(gate) * up
    if n_experts > 1:
        lane = lax.broadcasted_iota(jnp.int32, comb_ref.shape, 1)
        ce = jnp.sum(jnp.where(lane == e, comb_ref[...], 0.0), axis=-1, keepdims=True)
        act = act * ce
    o_ref[...] += jnp.dot(act.astype(BF16), wd_ref[0], preferred_element_type=F32)


def ffn(x, g, comb, wg, wu, wd, *, tm, tf):
    s, d = x.shape
    ne, _, fdim = wg.shape
    return pl.pallas_call(
        functools.partial(_ffn_kernel, n_experts=ne),
        grid=(s // tm, ne, fdim // tf),
        in_specs=[
            pl.BlockSpec((tm, d), lambda i, e, f: (i, 0)),
            pl.BlockSpec((1, d), lambda i, e, f: (0, 0)),
            pl.BlockSpec((tm, 128), lambda i, e, f: (i, 0)),
            pl.BlockSpec((1, d, tf), lambda i, e, f: (e, 0, f)),
            pl.BlockSpec((1, d, tf), lambda i, e, f: (e, 0, f)),
            pl.BlockSpec((1, tf, d), lambda i, e, f: (e, f, 0)),
        ],
        out_specs=pl.BlockSpec((tm, d), lambda i, e, f: (i, 0)),
        out_shape=jax.ShapeDtypeStruct((s, d), F32),
        scratch_shapes=[pltpu.VMEM((tm, d), BF16)],
        compiler_params=_params(("parallel", "arbitrary", "arbitrary")),
        name="ffn",
    )(x, g, comb, wg, wu, wd)


def _router_kernel(x_ref, g_ref, wr_ref, comb_ref):
    hn = _rms(x_ref[...], g_ref[...])
    logits = jnp.dot(hn, wr_ref[...], preferred_element_type=F32, precision=lax.Precision.HIGHEST)
    lane = lax.broadcasted_iota(jnp.int32, logits.shape, 1)
    logits = jnp.where(lane < N_EXPERTS, logits, NEG)
    v1 = jnp.max(logits, axis=-1, keepdims=True)
    i1 = jnp.min(jnp.where(logits == v1, lane, 128), axis=-1, keepdims=True)
    rest = jnp.where(lane == i1, NEG, logits)
    v2 = jnp.max(rest, axis=-1, keepdims=True)
    i2 = jnp.min(jnp.where(rest == v2, lane, 128), axis=-1, keepdims=True)
    e2 = jnp.exp(v2 - v1)
    g1 = 1.0 / (1.0 + e2)
    g2 = e2 / (1.0 + e2)
    comb_ref[...] = jnp.where(lane == i1, g1, 0.0) + jnp.where(lane == i2, g2, 0.0)


def router(x, g, wr, *, tm):
    s, d = x.shape
    return pl.pallas_call(
        _router_kernel,
        grid=(s // tm,),
        in_specs=[
            pl.BlockSpec((tm, d), lambda i: (i, 0)),
            pl.BlockSpec((1, d), lambda i: (0, 0)),
            pl.BlockSpec((d, 128), lambda i: (0, 0)),
        ],
        out_specs=pl.BlockSpec((tm, 128), lambda i: (i, 0)),
        out_shape=jax.ShapeDtypeStruct((s, 128), F32),
        compiler_params=_params(("parallel",)),
        name="router",
    )(x, g, wr)


def _pad_cols(w, n):
    return jnp.pad(w, ((0, 0), (0, n - w.shape[1])))


def _mla_weights(w_q_b, w_kv_b):
    wq = w_q_b.reshape(Q_LORA, N_HEADS_B, QK_DIM_B)
    wq = jnp.pad(wq, ((0, 0), (0, 0), (0, QK_PAD_B - QK_DIM_B))).reshape(Q_LORA, N_HEADS_B * QK_PAD_B)
    wkv = w_kv_b.reshape(KV_LORA, N_HEADS_B, NOPE_DIM + V_DIM)
    wk = wkv[:, :, :NOPE_DIM].reshape(KV_LORA, N_HEADS_B * NOPE_DIM)
    wv = wkv[:, :, NOPE_DIM:].reshape(KV_LORA, N_HEADS_B * V_DIM)
    return wq.astype(BF16), wk.astype(BF16), wv.astype(BF16)


def _rope_tables(positions):
    half = ROPE_DIM // 2
    inv = ROPE_THETA ** (-jnp.arange(half, dtype=F32) / half)
    ang = positions.astype(F32)[:, None] * inv
    cos, sin = jnp.cos(ang), jnp.sin(ang)
    z = jnp.zeros_like(cos)
    c = jnp.concatenate([cos, cos, z, z], axis=-1)
    s1 = jnp.concatenate([-sin, z, z, z], axis=-1)
    s2 = jnp.concatenate([z, sin, z, z], axis=-1)
    return c, s1, s2


def kernel(x, positions, rel_bias_table, norm_mix_g, w_in, q_a_norm_g, kv_a_norm_g, w_q_b, w_kv_b, q_norm_a_g, k_norm_a_g, q_norm_b_g, k_norm_b_g, w_out, norm_ffn_g, w_ff_gate, w_ff_up, w_ff_down, w_router, w_exp_gate, w_exp_up, w_exp_down):
    batch, seq, d = x.shape
    depth = w_in.shape[0]
    outs = []
    bias = _dilated_bias(rel_bias_table)
    for bi in range(batch):
        xs = x[bi]
        rope_c, rope_s1, rope_s2 = _rope_tables(positions[bi])
        for l in range(depth):
            w_in_l = _pad_cols(w_in[l], PROJ_COLS).astype(BF16)
            proj = norm_matmul(xs, norm_mix_g[l][None], w_in_l, tm=1024, tn=1024, out_dtype=F32)
            a = dilated_mixer(proj, bias, q_norm_a_g[l][None], k_norm_a_g[l][None])
            wq, wk, wv = _mla_weights(w_q_b[l], w_kv_b[l])
            qg = _pad_cols(q_norm_b_g[l][None] * (QK_DIM_B ** -0.5 * LOG2E), QK_PAD_B)
            kg = _pad_cols(k_norm_b_g[l][None], QK_PAD_B)
            qb, kb, vb = mla_prep(proj, q_a_norm_g[l][None], kv_a_norm_g[l][None], wq, wk, wv,
                                  qg, kg, rope_c, rope_s1, rope_s2, tm=512)
            b = mla_flash(qb, kb, vb, tq=1024, tk=512)
            w_out_l = w_out[l].astype(BF16)
            xs = out_proj(xs, a, b, w_out_l[:WIDTH_A], w_out_l[WIDTH_A:], tm=1024, tn=1024)
            gf = norm_ffn_g[l][None]
            if l % 2 == 0:
                i = l // 2
                comb = jnp.ones((seq, 128), F32)
                xs = ffn(xs, gf, comb, w_ff_gate[i][None].astype(BF16), w_ff_up[i][None].astype(BF16),
                         w_ff_down[i][None].astype(BF16), tm=512, tf=512)
            else:
                i = l // 2
                comb = router(xs, gf, _pad_cols(w_router[i], 128), tm=512)
                xs = ffn(xs, gf, comb, w_exp_gate[i].astype(BF16), w_exp_up[i].astype(BF16),
                         w_exp_down[i].astype(BF16), tm=512, tf=512)
        outs.append(xs)
    return jnp.stack(outs, axis=0)
```

```python
import functools
import math

import jax
import jax.numpy as jnp
from jax import lax
from jax.experimental import pallas as pl
from jax.experimental.pallas import tpu as pltpu

F32 = jnp.float32
BF16 = jnp.bfloat16

D_MODEL = 2048
HEAD_DIM = 128
N_HEADS_A = 8
DILATED_PATTERNS = ((128, 1), (512, 4), (2048, 16))
BLK = 128
N_BUCKETS = 32
MAX_DISTANCE = 2048
N_HEADS_B = 8
Q_LORA = 512
KV_LORA = 256
NOPE_DIM = 128
ROPE_DIM = 64
V_DIM = 128
QK_DIM_B = NOPE_DIM + ROPE_DIM
QK_PAD_B = 256
VT_ROWS = V_DIM + 16
LOG2E = math.log2(math.e)
ROPE_THETA = 10000.0
WIDTH_A = N_HEADS_A * HEAD_DIM
WIDTH_B = N_HEADS_B * V_DIM
IN_COLS = 3 * WIDTH_A + Q_LORA + KV_LORA + ROPE_DIM
PROJ_COLS = 4096
D_FF = 5632
N_EXPERTS = 8
TOP_K = 2
ROW_SUB = 8
HALF = D_MODEL // 2
U32 = jnp.uint32
EPS = 1e-6
NEG = -1e30

SUPER = 2048
VMEM_LIMIT = 56 * 1024 * 1024


def _params(sem, vmem=VMEM_LIMIT):
    return pltpu.CompilerParams(dimension_semantics=sem, vmem_limit_bytes=vmem)


def _rms(x, g, n=None):
    ss = jnp.sum(x * x, axis=-1, keepdims=True)
    n = x.shape[-1] if n is None else n
    return x * lax.rsqrt(ss * (1.0 / n) + EPS) * g


def _norm_matmul_kernel(x_ref, g_ref, w_ref, o_ref, hn_ref):
    @pl.when(pl.program_id(1) == 0)
    def _():
        hn_ref[...] = _rms(x_ref[...], g_ref[...]).astype(BF16)

    o_ref[...] = jnp.dot(hn_ref[...], w_ref[...], preferred_element_type=F32).astype(o_ref.dtype)


def norm_matmul(x, g, w, *, tm, tn, out_dtype):
    s, k = x.shape
    n = w.shape[1]
    return pl.pallas_call(
        _norm_matmul_kernel,
        grid=(s // tm, n // tn),
        in_specs=[
            pl.BlockSpec((tm, k), lambda i, j: (i, 0)),
            pl.BlockSpec((1, k), lambda i, j: (0, 0)),
            pl.BlockSpec((k, tn), lambda i, j: (0, j)),
        ],
        out_specs=pl.BlockSpec((tm, tn), lambda i, j: (i, j)),
        out_shape=jax.ShapeDtypeStruct((s, n), out_dtype),
        scratch_shapes=[pltpu.VMEM((tm, k), BF16)],
        compiler_params=_params(("parallel", "arbitrary")),
        name="norm_matmul",
    )(x, g, w)


def _t5_bucket(n):
    max_exact = N_BUCKETS // 2
    nf = jnp.maximum(n, 1).astype(F32)
    large = max_exact + (jnp.log(nf / max_exact) / math.log(MAX_DISTANCE / max_exact)
                         * (N_BUCKETS - max_exact)).astype(jnp.int32)
    large = jnp.minimum(large, N_BUCKETS - 1)
    return jnp.where(n < max_exact, n, large)


def _dilated_bias(rel_bias):
    i = jnp.arange(BLK)[:, None]
    j = jnp.arange(2 * BLK)[None, :]
    delta = i + BLK - j
    out = []
    for window, dilation in DILATED_PATTERNS:
        band = (delta >= 0) & (delta <= window // dilation)
        bucket = _t5_bucket(jnp.maximum(delta, 0) * dilation)
        onehot = (bucket[None] == jnp.arange(N_BUCKETS)[:, None, None]).astype(F32)
        bias = jnp.einsum('nij,nh->hij', onehot, rel_bias.astype(F32), precision=lax.Precision.HIGHEST)
        out.append(jnp.where(band[None], bias, NEG))
    return jnp.stack(out, axis=0)


def _dilated_kernel(q_ref, kp_ref, kc_ref, vp_ref, vc_ref, bias_ref, qg_ref, kg_ref, o_ref,
                    qn_s, kn_s, v_s, op_s, lse_s):
    sb = pl.program_id(1)
    qn_s[...] = _rms(q_ref[...], qg_ref[...]) * (HEAD_DIM ** -0.5)
    kn_s[0:SUPER, :] = _rms(kp_ref[...], kg_ref[...])
    kn_s[SUPER:2 * SUPER, :] = _rms(kc_ref[...], kg_ref[...])
    v_s[0:SUPER, :] = vp_ref[...]
    v_s[SUPER:2 * SUPER, :] = vc_ref[...]
    col = lax.broadcasted_iota(jnp.int32, (BLK, 2 * BLK), 1)

    for pi, (_, d) in enumerate(DILATED_PATTERNS):
        def rows(start, size, d=d):
            return pl.ds(start, size) if d == 1 else pl.ds(start, size, stride=d)

        def body(idx, carry, pi=pi, d=d, rows=rows):
            r = idx % d
            b = idx // d
            q0 = r + d * BLK * b
            k0 = SUPER + q0 - d * BLK
            qb = qn_s[rows(q0, BLK), :].astype(BF16)
            k2 = kn_s[rows(k0, 2 * BLK), :].astype(BF16)
            v2 = v_s[rows(k0, 2 * BLK), :].astype(BF16)
            s = lax.dot_general(qb, k2, (((1,), (1,)), ((), ())), preferred_element_type=F32)
            s = s + bias_ref[pi, 0]
            first = jnp.logical_and(sb == 0, b == 0)
            s = jnp.where(jnp.logical_and(first, col < BLK), NEG, s)
            m = jnp.max(s, axis=-1, keepdims=True)
            p = jnp.exp(s - m)
            den = jnp.sum(p, axis=-1, keepdims=True)
            o = jnp.dot(p.astype(BF16), v2, preferred_element_type=F32) / den
            lse = m + jnp.log(den)
            op_s[pi, rows(q0, BLK), :] = o
            lse_s[pi, rows(q0, BLK), :] = jnp.broadcast_to(lse, (BLK, HEAD_DIM))
            return carry

        lax.fori_loop(0, SUPER // BLK, body, 0)

    chunk = 256
    for c in range(SUPER // chunk):
        sl = slice(c * chunk, (c + 1) * chunk)
        l0, l1, l2 = lse_s[0, sl, :], lse_s[1, sl, :], lse_s[2, sl, :]
        mx = jnp.maximum(jnp.maximum(l0, l1), l2)
        e0, e1, e2 = jnp.exp(l0 - mx), jnp.exp(l1 - mx), jnp.exp(l2 - mx)
        num = e0 * op_s[0, sl, :] + e1 * op_s[1, sl, :] + e2 * op_s[2, sl, :]
        o_ref[sl, :] = (num / (e0 + e1 + e2)).astype(o_ref.dtype)


def dilated_mixer(proj, bias, q_g, k_g):
    s = proj.shape[0]
    nsb = s // SUPER
    h8 = N_HEADS_A
    blk = (SUPER, HEAD_DIM)
    return pl.pallas_call(
        _dilated_kernel,
        grid=(h8, nsb),
        in_specs=[
            pl.BlockSpec(blk, lambda h, i: (i, h)),
            pl.BlockSpec(blk, lambda h, i: (jnp.maximum(i - 1, 0), h8 + h)),
            pl.BlockSpec(blk, lambda h, i: (i, h8 + h)),
            pl.BlockSpec(blk, lambda h, i: (jnp.maximum(i - 1, 0), 2 * h8 + h)),
            pl.BlockSpec(blk, lambda h, i: (i, 2 * h8 + h)),
            pl.BlockSpec((len(DILATED_PATTERNS), 1, BLK, 2 * BLK), lambda h, i: (0, h, 0, 0)),
            pl.BlockSpec((1, HEAD_DIM), lambda h, i: (0, 0)),
            pl.BlockSpec((1, HEAD_DIM), lambda h, i: (0, 0)),
        ],
        out_specs=pl.BlockSpec(blk, lambda h, i: (i, h)),
        out_shape=jax.ShapeDtypeStruct((s, WIDTH_A), BF16),
        scratch_shapes=[
            pltpu.VMEM((SUPER, HEAD_DIM), F32),
            pltpu.VMEM((2 * SUPER, HEAD_DIM), F32),
            pltpu.VMEM((2 * SUPER, HEAD_DIM), F32),
            pltpu.VMEM((len(DILATED_PATTERNS), SUPER, HEAD_DIM), F32),
            pltpu.VMEM((len(DILATED_PATTERNS), SUPER, HEAD_DIM), F32),
        ],
        compiler_params=_params(("parallel", "arbitrary")),
        name="dilated_mixer",
    )(proj, proj, proj, proj, proj, bias, q_g, k_g)


def _mla_prep_kernel(cq_ref, ckv_ref, kr_ref, qag_ref, kvag_ref, wq_ref, wk_ref, wv_ref,
                     qg_ref, kg_ref, c_ref, s1_ref, s2_ref, q_out, k_out, v_out):
    c, s1, s2 = c_ref[...], s1_ref[...], s2_ref[...]

    def rope(x):
        return x * c + pltpu.roll(x, 96, 1) * s1 + pltpu.roll(x, 32, 1) * s2

    cqn = _rms(cq_ref[...], qag_ref[...]).astype(BF16)
    ckvn = _rms(ckv_ref[...], kvag_ref[...]).astype(BF16)
    qpre = jnp.dot(cqn, wq_ref[...], preferred_element_type=F32)
    knope = jnp.dot(ckvn, wk_ref[...], preferred_element_type=F32)
    v = jnp.dot(ckvn, wv_ref[...], preferred_element_type=F32)
    kr = kr_ref[...]
    kr_ss = jnp.sum(kr * kr, axis=-1, keepdims=True)
    qg, kg = qg_ref[...], kg_ref[...]
    kr_roped = rope(kr * kg[:, NOPE_DIM:])
    inv_n = 1.0 / QK_DIM_B
    ones = jnp.ones((VT_ROWS - V_DIM, cq_ref.shape[0]), BF16)
    for h in range(N_HEADS_B):
        qh = qpre[:, h * QK_PAD_B:(h + 1) * QK_PAD_B]
        rs = lax.rsqrt(jnp.sum(qh * qh, axis=-1, keepdims=True) * inv_n + EPS)
        q_out[h, 0:NOPE_DIM, :] = (qh[:, :NOPE_DIM] * rs * qg[:, :NOPE_DIM]).T.astype(BF16)
        q_out[h, NOPE_DIM:QK_PAD_B, :] = rope(qh[:, NOPE_DIM:] * rs * qg[:, NOPE_DIM:]).T.astype(BF16)
        kh = knope[:, h * NOPE_DIM:(h + 1) * NOPE_DIM]
        rs = lax.rsqrt((jnp.sum(kh * kh, axis=-1, keepdims=True) + kr_ss) * inv_n + EPS)
        k_out[h, :, 0:NOPE_DIM] = (kh * rs * kg[:, :NOPE_DIM]).astype(BF16)
        k_out[h, :, NOPE_DIM:QK_PAD_B] = (kr_roped * rs).astype(BF16)
        v_out[h, 0:V_DIM, :] = v[:, h * V_DIM:(h + 1) * V_DIM].T.astype(BF16)
        v_out[h, V_DIM:VT_ROWS, :] = ones


def mla_prep(proj, q_a_g, kv_a_g, wq, wk, wv, qg, kg, rope_c, rope_s1, rope_s2, *, tm):
    s = proj.shape[0]
    hb = N_HEADS_B
    full = lambda shape: pl.BlockSpec(shape, lambda i: (0,) * len(shape))
    cq_blk = (3 * WIDTH_A) // Q_LORA
    ckv_blk = (3 * WIDTH_A + Q_LORA) // KV_LORA
    kr_blk = (3 * WIDTH_A + Q_LORA + KV_LORA) // 128
    return pl.pallas_call(
        _mla_prep_kernel,
        grid=(s // tm,),
        in_specs=[
            pl.BlockSpec((tm, Q_LORA), lambda i: (i, cq_blk)),
            pl.BlockSpec((tm, KV_LORA), lambda i: (i, ckv_blk)),
            pl.BlockSpec((tm, 128), lambda i: (i, kr_blk)),
            full((1, Q_LORA)), full((1, KV_LORA)),
            full(wq.shape), full(wk.shape), full(wv.shape),
            full((1, QK_PAD_B)), full((1, QK_PAD_B)),
            pl.BlockSpec((tm, 128), lambda i: (i, 0)),
            pl.BlockSpec((tm, 128), lambda i: (i, 0)),
            pl.BlockSpec((tm, 128), lambda i: (i, 0)),
        ],
        out_specs=[
            pl.BlockSpec((hb, QK_PAD_B, tm), lambda i: (0, 0, i)),
            pl.BlockSpec((hb, tm, QK_PAD_B), lambda i: (0, i, 0)),
            pl.BlockSpec((hb, VT_ROWS, tm), lambda i: (0, 0, i)),
        ],
        out_shape=[
            jax.ShapeDtypeStruct((hb, QK_PAD_B, s), BF16),
            jax.ShapeDtypeStruct((hb, s, QK_PAD_B), BF16),
            jax.ShapeDtypeStruct((hb, VT_ROWS, s), BF16),
        ],
        compiler_params=_params(("parallel",)),
        name="mla_prep",
    )(proj, proj, proj, q_a_g, kv_a_g, wq, wk, wv, qg, kg, rope_c, rope_s1, rope_s2)


def _flash_kernel(qt_ref, k_ref, vt_ref, o_ref, s_a, s_b, m_s, acc_s, *, tq, tk):
    i = pl.program_id(1)
    qt = qt_ref[0]
    m_s[...] = jnp.full(m_s.shape, NEG, F32)
    acc_s[...] = jnp.zeros(acc_s.shape, F32)

    def compute(c, dst):
        start = pl.multiple_of(c * tk, tk)
        dst[...] = jnp.dot(k_ref[0, pl.ds(start, tk), :], qt, preferred_element_type=F32)

    def process(c, src, masked):
        s = src[...]
        if masked:
            key = c * tk + lax.broadcasted_iota(jnp.int32, (tk, tq), 0)
            qry = i * tq + lax.broadcasted_iota(jnp.int32, (tk, tq), 1)
            s = jnp.where(key <= qry, s, NEG)
        m_prev = m_s[...]
        m_new = jnp.maximum(m_prev, jnp.max(s, axis=0, keepdims=True))
        alpha = jnp.exp2(m_prev - m_new)
        p = jnp.exp2(s - m_new).astype(BF16)
        start = pl.multiple_of(c * tk, tk)
        pv = jnp.dot(vt_ref[0, :, pl.ds(start, tk)], p, preferred_element_type=F32)
        acc_s[...] = alpha * acc_s[...] + pv
        m_s[...] = m_new

    assert tq == 2 * tk
    compute(0, s_a)

    def pair(pp, carry):
        c = 2 * pp
        process(c, s_a, False)
        compute(c + 1, s_b)
        process(c + 1, s_b, False)
        compute(c + 2, s_a)
        return carry

    lax.fori_loop(0, i, pair, 0)
    c = 2 * i
    process(c, s_a, True)
    compute(c + 1, s_b)
    process(c + 1, s_b, True)
    acc = acc_s[...]
    o_t = acc[0:V_DIM, :] / acc[V_DIM:V_DIM + 1, :]
    o_ref[...] = o_t.T.astype(o_ref.dtype)


def mla_flash(qt, k, vt, *, tq, tk):
    hb, s, _ = k.shape
    return pl.pallas_call(
        functools.partial(_flash_kernel, tq=tq, tk=tk),
        grid=(hb, s // tq),
        in_specs=[
            pl.BlockSpec((1, QK_PAD_B, tq), lambda h, i: (h, 0, i)),
            pl.BlockSpec((1, s, QK_PAD_B), lambda h, i: (h, 0, 0)),
            pl.BlockSpec((1, VT_ROWS, s), lambda h, i: (h, 0, 0)),
        ],
        out_specs=pl.BlockSpec((tq, V_DIM), lambda h, i: (i, h)),
        out_shape=jax.ShapeDtypeStruct((s, WIDTH_B), BF16),
        scratch_shapes=[
            pltpu.VMEM((tk, tq), F32),
            pltpu.VMEM((tk, tq), F32),
            pltpu.VMEM((1, tq), F32),
            pltpu.VMEM((VT_ROWS, tq), F32),
        ],
        compiler_params=_params(("parallel", "arbitrary")),
        name="mla_flash",
    )(qt, k, vt)


def _out_proj_kernel(x_ref, a_ref, b_ref, wa_ref, wb_ref, o_ref):
    acc = jnp.dot(a_ref[...], wa_ref[...], preferred_element_type=F32)
    acc = acc + jnp.dot(b_ref[...], wb_ref[...], preferred_element_type=F32)
    o_ref[...] = x_ref[...] + acc


def out_proj(x, a, b, wa, wb, *, tm, tn):
    s, n = x.shape
    ka, kb = a.shape[1], b.shape[1]
    return pl.pallas_call(
        _out_proj_kernel,
        grid=(s // tm, n // tn),
        in_specs=[
            pl.BlockSpec((tm, tn), lambda i, j: (i, j)),
            pl.BlockSpec((tm, ka), lambda i, j: (i, 0)),
            pl.BlockSpec((tm, kb), lambda i, j: (i, 0)),
            pl.BlockSpec((ka, tn), lambda i, j: (0, j)),
            pl.BlockSpec((kb, tn), lambda i, j: (0, j)),
        ],
        out_specs=pl.BlockSpec((tm, tn), lambda i, j: (i, j)),
        out_shape=jax.ShapeDtypeStruct((s, n), F32),
        compiler_params=_params(("parallel", "arbitrary")),
        name="out_proj",
    )(x, a, b, wa, wb)


def _swiglu_step(h, wg, wu, wd):
    gate = jnp.dot(h, wg, preferred_element_type=F32)
    up = jnp.dot(h, wu, preferred_element_type=F32)
    act = gate * jax.nn.sigmoid(gate) * up
    return jnp.dot(act.astype(BF16), wd, preferred_element_type=F32)


def _ffn_kernel(x_ref, g_ref, wg_ref, wu_ref, wd_ref, o_ref, hn_ref):
    @pl.when(pl.program_id(1) == 0)
    def _():
        hn_ref[...] = _rms(x_ref[...], g_ref[...]).astype(BF16)
        o_ref[...] = x_ref[...]

    o_ref[...] += _swiglu_step(hn_ref[...], wg_ref[...], wu_ref[...], wd_ref[...])


def ffn(x, g, wg, wu, wd, *, tm, tf):
    s, d = x.shape
    fdim = wg.shape[1]
    return pl.pallas_call(
        _ffn_kernel,
        grid=(s // tm, fdim // tf),
        in_specs=[
            pl.BlockSpec((tm, d), lambda i, f: (i, 0)),
            pl.BlockSpec((1, d), lambda i, f: (0, 0)),
            pl.BlockSpec((d, tf), lambda i, f: (0, f)),
            pl.BlockSpec((d, tf), lambda i, f: (0, f)),
            pl.BlockSpec((tf, d), lambda i, f: (f, 0)),
        ],
        out_specs=pl.BlockSpec((tm, d), lambda i, f: (i, 0)),
        out_shape=jax.ShapeDtypeStruct((s, d), F32),
        scratch_shapes=[pltpu.VMEM((tm, d), BF16)],
        compiler_params=_params(("parallel", "arbitrary")),
        name="ffn",
    )(x, g, wg, wu, wd)


def _pack_rows(y, dst):
    n = y.shape[0]
    for s in range(ROW_SUB):
        lo = y[:, s * 128:(s + 1) * 128].astype(BF16).astype(F32)
        hi = y[:, HALF + s * 128:HALF + (s + 1) * 128].astype(BF16).astype(F32)
        w = (lax.bitcast_convert_type(lo, U32) >> 16) | lax.bitcast_convert_type(hi, U32)
        dst[pl.ds(s, n, stride=ROW_SUB), :] = w


def _unpack_rows(src, n):
    lo, hi = [], []
    for s in range(ROW_SUB):
        w = src[pl.ds(s, n, stride=ROW_SUB), :]
        lo.append(lax.bitcast_convert_type(w << 16, F32))
        hi.append(lax.bitcast_convert_type(w & jnp.uint32(0xFFFF0000), F32))
    return lo + hi


META_I1, META_I2, META_R1, META_R2, META_G1, META_G2 = range(6)


def _router_kernel(x_ref, g_ref, wr_ref, hp_ref, meta_ref, cnt_ref, carry):
    @pl.when(pl.program_id(0) == 0)
    def _():
        carry[...] = jnp.zeros(carry.shape, F32)

    hn = _rms(x_ref[...], g_ref[...])
    _pack_rows(hn, hp_ref)
    logits = jnp.dot(hn, wr_ref[...], preferred_element_type=F32, precision=lax.Precision.HIGHEST)
    lane = lax.broadcasted_iota(jnp.int32, logits.shape, 1)
    logits = jnp.where(lane < N_EXPERTS, logits, NEG)
    v1 = jnp.max(logits, axis=-1, keepdims=True)
    i1 = jnp.min(jnp.where(logits == v1, lane, 128), axis=-1, keepdims=True)
    rest = jnp.where(lane == i1, NEG, logits)
    v2 = jnp.max(rest, axis=-1, keepdims=True)
    i2 = jnp.min(jnp.where(rest == v2, lane, 128), axis=-1, keepdims=True)
    e2 = jnp.exp(v2 - v1)
    g1 = 1.0 / (1.0 + e2)
    g2 = e2 / (1.0 + e2)

    tm = hn.shape[0]
    member = jnp.where(jnp.logical_or(lane == i1, lane == i2), 1.0, 0.0)
    row = lax.broadcasted_iota(jnp.int32, (tm, tm), 0)
    col = lax.broadcasted_iota(jnp.int32, (tm, tm), 1)
    earlier = jnp.where(col < row, 1.0, 0.0).astype(BF16)
    rank = carry[...] + jnp.dot(earlier, member.astype(BF16), preferred_element_type=F32)
    r1 = jnp.sum(jnp.where(lane == i1, rank, 0.0), axis=-1, keepdims=True)
    r2 = jnp.sum(jnp.where(lane == i2, rank, 0.0), axis=-1, keepdims=True)
    carry[...] += jnp.sum(member, axis=0, keepdims=True)
    cnt_ref[...] = carry[...]
    meta = jnp.zeros(logits.shape, F32)
    for pos, val in ((META_I1, i1.astype(F32)), (META_I2, i2.astype(F32)), (META_R1, r1),
                     (META_R2, r2), (META_G1, g1), (META_G2, g2)):
        meta = jnp.where(lane == pos, val, meta)
    meta_ref[...] = meta


def router(x, g, wr, *, tm):
    s, d = x.shape
    return pl.pallas_call(
        _router_kernel,
        grid=(s // tm,),
        in_specs=[
            pl.BlockSpec((tm, d), lambda i: (i, 0)),
            pl.BlockSpec((1, d), lambda i: (0, 0)),
            pl.BlockSpec((d, 128), lambda i: (0, 0)),
        ],
        out_specs=[
            pl.BlockSpec((tm * ROW_SUB, 128), lambda i: (i, 0)),
            pl.BlockSpec((tm, 128), lambda i: (i, 0)),
            pl.BlockSpec((1, 128), lambda i: (0, 0)),
        ],
        out_shape=[
            jax.ShapeDtypeStruct((s * ROW_SUB, 128), U32),
            jax.ShapeDtypeStruct((s, 128), F32),
            jax.ShapeDtypeStruct((1, 128), F32),
        ],
        scratch_shapes=[pltpu.VMEM((1, 128), F32)],
        compiler_params=_params(("arbitrary",)),
        name="router",
    )(x, g, wr)


def _row_tile(ref, r):
    return ref.at[pl.ds(pl.multiple_of(r * ROW_SUB, ROW_SUB), ROW_SUB), :]


def _dispatch_kernel(dest_ref, hp_ref, xs_in, xs_out, sem, *, tm):
    del xs_in
    base = pl.program_id(0) * tm

    def row_copy(t, k):
        return pltpu.make_async_copy(_row_tile(hp_ref, t), _row_tile(xs_out, dest_ref[2 * (base + t) + k]), sem)

    def issue(t, c):
        row_copy(t, 0).start()
        row_copy(t, 1).start()
        return c

    def drain(t, c):
        row_copy(t, 0).wait()
        row_copy(t, 1).wait()
        return c

    lax.fori_loop(0, tm, issue, 0)
    lax.fori_loop(0, tm, drain, 0)


def dispatch(dest, hp, n_rows, *, tm):
    s = hp.shape[0] // ROW_SUB
    xs0 = jnp.zeros((n_rows * ROW_SUB, 128), U32)
    return pl.pallas_call(
        functools.partial(_dispatch_kernel, tm=tm),
        grid_spec=pltpu.PrefetchScalarGridSpec(
            num_scalar_prefetch=1,
            grid=(s // tm,),
            in_specs=[
                pl.BlockSpec((tm * ROW_SUB, 128), lambda i, dest: (i, 0)),
                pl.BlockSpec(memory_space=pl.ANY),
            ],
            out_specs=pl.BlockSpec(memory_space=pl.ANY),
            scratch_shapes=[pltpu.SemaphoreType.DMA(())],
        ),
        out_shape=jax.ShapeDtypeStruct(xs0.shape, U32),
        input_output_aliases={2: 0},
        compiler_params=_params(("arbitrary",)),
        name="moe_dispatch",
    )(dest, hp, xs0)


def _moe_ffn_kernel(te_ref, nu_ref, xs_ref, wg_ref, wu_ref, wd_ref, ys_ref, xb, acc, *, tm):
    del te_ref
    f = pl.program_id(1)
    used = pl.program_id(0) < nu_ref[0]
    last = f == pl.num_programs(1) - 1

    @pl.when(jnp.logical_and(used, f == 0))
    def _():
        for c, v in enumerate(_unpack_rows(xs_ref, tm)):
            xb[:, c * 128:(c + 1) * 128] = v.astype(BF16)
        acc[...] = jnp.zeros(acc.shape, F32)

    @pl.when(used)
    def _():
        acc[...] += _swiglu_step(xb[...], wg_ref[0], wu_ref[0], wd_ref[0])

    @pl.when(jnp.logical_and(used, last))
    def _():
        _pack_rows(acc[...], ys_ref)

    @pl.when(jnp.logical_and(jnp.logical_not(used), last))
    def _():
        ys_ref[...] = jnp.zeros(ys_ref.shape, U32)


def moe_ffn(tile_expert, n_used, xs, wg, wu, wd, *, tm, tf):
    nt = tile_expert.shape[0]
    _, d, fdim = wg.shape
    nf = fdim // tf

    def f_idx(j, f, nu):
        return jnp.where(j < nu[0], f, nf - 1)

    return pl.pallas_call(
        functools.partial(_moe_ffn_kernel, tm=tm),
        grid_spec=pltpu.PrefetchScalarGridSpec(
            num_scalar_prefetch=2,
            grid=(nt, nf),
            in_specs=[
                pl.BlockSpec((tm * ROW_SUB, 128), lambda j, f, te, nu: (jnp.minimum(j, nu[0] - 1), 0)),
                pl.BlockSpec((1, d, tf), lambda j, f, te, nu: (te[j], 0, f_idx(j, f, nu))),
                pl.BlockSpec((1, d, tf), lambda j, f, te, nu: (te[j], 0, f_idx(j, f, nu))),
                pl.BlockSpec((1, tf, d), lambda j, f, te, nu: (te[j], f_idx(j, f, nu), 0)),
            ],
            out_specs=pl.BlockSpec((tm * ROW_SUB, 128), lambda j, f, te, nu: (j, 0)),
            scratch_shapes=[pltpu.VMEM((tm, d), BF16), pltpu.VMEM((tm, d), F32)],
        ),
        out_shape=jax.ShapeDtypeStruct(xs.shape, U32),
        compiler_params=_params(("parallel", "arbitrary")),
        name="moe_ffn",
    )(tile_expert, n_used, xs, wg, wu, wd)


def _combine_kernel(dest_ref, x_ref, meta_ref, ys_hbm, o_ref, buf, sem, *, tm):
    base = pl.program_id(0) * tm

    def row_copy(t, k):
        return pltpu.make_async_copy(_row_tile(ys_hbm, dest_ref[2 * (base + t) + k]), _row_tile(buf.at[k], t), sem)

    def issue(t, c):
        row_copy(t, 0).start()
        row_copy(t, 1).start()
        return c

    def drain(t, c):
        row_copy(t, 0).wait()
        row_copy(t, 1).wait()
        return c

    lax.fori_loop(0, tm, issue, 0)
    lax.fori_loop(0, tm, drain, 0)
    meta = meta_ref[...]
    g1 = meta[:, META_G1:META_G1 + 1]
    g2 = meta[:, META_G2:META_G2 + 1]
    y1 = _unpack_rows(buf.at[0], tm)
    y2 = _unpack_rows(buf.at[1], tm)
    for c in range(len(y1)):
        sl = slice(c * 128, (c + 1) * 128)
        o_ref[:, sl] = x_ref[:, sl] + g1 * y1[c] + g2 * y2[c]


def combine(dest, x, meta, ys, *, tm):
    s, d = x.shape
    return pl.pallas_call(
        functools.partial(_combine_kernel, tm=tm),
        grid_spec=pltpu.PrefetchScalarGridSpec(
            num_scalar_prefetch=1,
            grid=(s // tm,),
            in_specs=[
                pl.BlockSpec((tm, d), lambda i, dest: (i, 0)),
                pl.BlockSpec((tm, 128), lambda i, dest: (i, 0)),
                pl.BlockSpec(memory_space=pl.ANY),
            ],
            out_specs=pl.BlockSpec((tm, d), lambda i, dest: (i, 0)),
            scratch_shapes=[pltpu.VMEM((2, tm * ROW_SUB, 128), U32), pltpu.SemaphoreType.DMA(())],
        ),
        out_shape=jax.ShapeDtypeStruct((s, d), F32),
        compiler_params=_params(("arbitrary",)),
        name="moe_combine",
    )(dest, x, meta, ys)


def moe(x, g, w_router, wg, wu, wd, *, tm_rows, tf):
    s = x.shape[0]
    hp, meta, cnt = router(x, g, w_router, tm=512)
    i1, i2 = meta[:, META_I1].astype(jnp.int32), meta[:, META_I2].astype(jnp.int32)
    r1, r2 = meta[:, META_R1].astype(jnp.int32), meta[:, META_R2].astype(jnp.int32)
    counts = cnt[0, :N_EXPERTS].astype(jnp.int32)
    padded = (counts + tm_rows - 1) // tm_rows * tm_rows
    ends = jnp.cumsum(padded)
    offs = ends - padded
    eids = jnp.arange(N_EXPERTS, dtype=jnp.int32)
    off1 = jnp.sum(jnp.where(i1[:, None] == eids, offs, 0), axis=1)
    off2 = jnp.sum(jnp.where(i2[:, None] == eids, offs, 0), axis=1)
    dest = jnp.stack([off1 + r1, off2 + r2], axis=1).reshape(-1)
    n_tiles = (TOP_K * s) // tm_rows + N_EXPERTS
    n_used = (ends[-1] // tm_rows).reshape(1)
    tile_start = jnp.minimum(jnp.arange(n_tiles, dtype=jnp.int32), n_used - 1) * tm_rows
    tile_expert = jnp.sum(tile_start[:, None] >= ends[None, :], axis=1).astype(jnp.int32)
    xs = dispatch(dest, hp, n_tiles * tm_rows, tm=256)
    ys = moe_ffn(tile_expert, n_used, xs, wg, wu, wd, tm=tm_rows, tf=tf)
    return combine(dest, x, meta, ys, tm=256)


def _pad_cols(w, n):
    return jnp.pad(w, ((0, 0), (0, n - w.shape[1])))


def _mla_weights(w_q_b, w_kv_b):
    wq = w_q_b.reshape(Q_LORA, N_HEADS_B, QK_DIM_B)
    wq = jnp.pad(wq, ((0, 0), (0, 0), (0, QK_PAD_B - QK_DIM_B))).reshape(Q_LORA, N_HEADS_B * QK_PAD_B)
    wkv = w_kv_b.reshape(KV_LORA, N_HEADS_B, NOPE_DIM + V_DIM)
    wk = wkv[:, :, :NOPE_DIM].reshape(KV_LORA, N_HEADS_B * NOPE_DIM)
    wv = wkv[:, :, NOPE_DIM:].reshape(KV_LORA, N_HEADS_B * V_DIM)
    return wq.astype(BF16), wk.astype(BF16), wv.astype(BF16)


def _rope_tables(positions):
    half = ROPE_DIM // 2
    inv = ROPE_THETA ** (-jnp.arange(half, dtype=F32) / half)
    ang = positions.astype(F32)[:, None] * inv
    cos, sin = jnp.cos(ang), jnp.sin(ang)
    z = jnp.zeros_like(cos)
    c = jnp.concatenate([cos, cos, z, z], axis=-1)
    s1 = jnp.concatenate([-sin, z, z, z], axis=-1)
    s2 = jnp.concatenate([z, sin, z, z], axis=-1)
    return c, s1, s2


def kernel(x, positions, rel_bias_table, norm_mix_g, w_in, q_a_norm_g, kv_a_norm_g, w_q_b, w_kv_b, q_norm_a_g, k_norm_a_g, q_norm_b_g, k_norm_b_g, w_out, norm_ffn_g, w_ff_gate, w_ff_up, w_ff_down, w_router, w_exp_gate, w_exp_up, w_exp_down):
    batch, seq, d = x.shape
    depth = w_in.shape[0]
    outs = []
    bias = _dilated_bias(rel_bias_table)
    for bi in range(batch):
        xs = x[bi]
        rope_c, rope_s1, rope_s2 = _rope_tables(positions[bi])
        for l in range(depth):
            w_in_l = _pad_cols(w_in[l], PROJ_COLS).astype(BF16)
            proj = norm_matmul(xs, norm_mix_g[l][None], w_in_l, tm=1024, tn=1024, out_dtype=F32)
            a = dilated_mixer(proj, bias, q_norm_a_g[l][None], k_norm_a_g[l][None])
            wq, wk, wv = _mla_weights(w_q_b[l], w_kv_b[l])
            qg = _pad_cols(q_norm_b_g[l][None] * (QK_DIM_B ** -0.5 * LOG2E), QK_PAD_B)
            kg = _pad_cols(k_norm_b_g[l][None], QK_PAD_B)
            qb, kb, vb = mla_prep(proj, q_a_norm_g[l][None], kv_a_norm_g[l][None], wq, wk, wv,
                                  qg, kg, rope_c, rope_s1, rope_s2, tm=512)
            b = mla_flash(qb, kb, vb, tq=1024, tk=512)
            w_out_l = w_out[l].astype(BF16)
            xs = out_proj(xs, a, b, w_out_l[:WIDTH_A], w_out_l[WIDTH_A:], tm=1024, tn=1024)
            gf = norm_ffn_g[l][None]
            if l % 2 == 0:
                i = l // 2
                xs = ffn(xs, gf, w_ff_gate[i].astype(BF16), w_ff_up[i].astype(BF16),
                         w_ff_down[i].astype(BF16), tm=512, tf=512)
            else:
                i = l // 2
                xs = moe(xs, gf, _pad_cols(w_router[i], 128), w_exp_gate[i].astype(BF16),
                         w_exp_up[i].astype(BF16), w_exp_down[i].astype(BF16), tm_rows=512, tf=512)
        outs.append(xs)
    return jnp.stack(outs, axis=0)
```

```python
import functools
import math

import jax
import jax.numpy as jnp
from jax import lax
from jax.experimental import pallas as pl
from jax.experimental.pallas import tpu as pltpu

F32 = jnp.float32
BF16 = jnp.bfloat16

D_MODEL = 2048
HEAD_DIM = 128
N_HEADS_A = 8
DILATED_PATTERNS = ((128, 1), (512, 4), (2048, 16))
BLK = 128
N_BUCKETS = 32
MAX_DISTANCE = 2048
N_HEADS_B = 8
Q_LORA = 512
KV_LORA = 256
NOPE_DIM = 128
ROPE_DIM = 64
V_DIM = 128
QK_DIM_B = NOPE_DIM + ROPE_DIM
QK_PAD_B = 256
VT_ROWS = V_DIM + 16
LOG2E = math.log2(math.e)
ROPE_THETA = 10000.0
WIDTH_A = N_HEADS_A * HEAD_DIM
WIDTH_B = N_HEADS_B * V_DIM
IN_COLS = 3 * WIDTH_A + Q_LORA + KV_LORA + ROPE_DIM
PROJ_COLS = 4096
D_FF = 5632
N_EXPERTS = 8
TOP_K = 2
ROW_SUB = 8
HALF = D_MODEL // 2
U32 = jnp.uint32
EPS = 1e-6
NEG = -1e30

SUPER = 2048
DILATED_UNROLL = 16
FLASH_PER_TRIP = 4
VMEM_LIMIT = 56 * 1024 * 1024


def _params(sem, vmem=VMEM_LIMIT):
    return pltpu.CompilerParams(dimension_semantics=sem, vmem_limit_bytes=vmem)


def _rms(x, g, n=None):
    ss = jnp.sum(x * x, axis=-1, keepdims=True)
    n = x.shape[-1] if n is None else n
    return x * lax.rsqrt(ss * (1.0 / n) + EPS) * g


def _norm_matmul_kernel(x_ref, g_ref, w_ref, o_ref, hn_ref):
    @pl.when(pl.program_id(1) == 0)
    def _():
        hn_ref[...] = _rms(x_ref[...], g_ref[...]).astype(BF16)

    o_ref[...] = jnp.dot(hn_ref[...], w_ref[...], preferred_element_type=F32).astype(o_ref.dtype)


def norm_matmul(x, g, w, *, tm, tn, out_dtype):
    s, k = x.shape
    n = w.shape[1]
    return pl.pallas_call(
        _norm_matmul_kernel,
        grid=(s // tm, n // tn),
        in_specs=[
            pl.BlockSpec((tm, k), lambda i, j: (i, 0)),
            pl.BlockSpec((1, k), lambda i, j: (0, 0)),
            pl.BlockSpec((k, tn), lambda i, j: (0, j)),
        ],
        out_specs=pl.BlockSpec((tm, tn), lambda i, j: (i, j)),
        out_shape=jax.ShapeDtypeStruct((s, n), out_dtype),
        scratch_shapes=[pltpu.VMEM((tm, k), BF16)],
        compiler_params=_params(("parallel", "arbitrary")),
        name="norm_matmul",
    )(x, g, w)


def _t5_bucket(n):
    max_exact = N_BUCKETS // 2
    nf = jnp.maximum(n, 1).astype(F32)
    large = max_exact + (jnp.log(nf / max_exact) / math.log(MAX_DISTANCE / max_exact)
                         * (N_BUCKETS - max_exact)).astype(jnp.int32)
    large = jnp.minimum(large, N_BUCKETS - 1)
    return jnp.where(n < max_exact, n, large)


def _dilated_bias(rel_bias):
    i = jnp.arange(BLK)[:, None]
    j = jnp.arange(2 * BLK)[None, :]
    delta = i + BLK - j
    out = []
    for window, dilation in DILATED_PATTERNS:
        band = (delta >= 0) & (delta <= window // dilation)
        bucket = _t5_bucket(jnp.maximum(delta, 0) * dilation)
        onehot = (bucket[None] == jnp.arange(N_BUCKETS)[:, None, None]).astype(F32)
        bias = jnp.einsum('nij,nh->hij', onehot, rel_bias.astype(F32), precision=lax.Precision.HIGHEST)
        out.append(jnp.where(band[None], bias * LOG2E, NEG))
    return jnp.stack(out, axis=0)


def _dilated_kernel(q_ref, kp_ref, kc_ref, vp_ref, vc_ref, bias_ref, qg_ref, kg_ref, o_ref,
                    qn_s, kn_s, v_s, op_s, lse_s):
    sb = pl.program_id(1)
    qn_s[...] = _rms(q_ref[...], qg_ref[...]) * (HEAD_DIM ** -0.5 * LOG2E)
    kn_s[0:SUPER, :] = _rms(kp_ref[...], kg_ref[...])
    kn_s[SUPER:2 * SUPER, :] = _rms(kc_ref[...], kg_ref[...])
    v_s[0:SUPER, :] = vp_ref[...]
    v_s[SUPER:2 * SUPER, :] = vc_ref[...]
    col = lax.broadcasted_iota(jnp.int32, (BLK, 2 * BLK), 1)

    for pi, (_, d) in enumerate(DILATED_PATTERNS):
        def rows(start, size, d=d):
            return pl.ds(start, size) if d == 1 else pl.ds(start, size, stride=d)

        def block(idx, pi=pi, d=d, rows=rows):
            r = idx % d
            b = idx // d
            q0 = r + d * BLK * b
            k0 = SUPER + q0 - d * BLK
            qb = qn_s[rows(q0, BLK), :].astype(BF16)
            k2 = kn_s[rows(k0, 2 * BLK), :].astype(BF16)
            v2 = v_s[rows(k0, 2 * BLK), :].astype(BF16)
            s = lax.dot_general(qb, k2, (((1,), (1,)), ((), ())), preferred_element_type=F32)
            s = s + bias_ref[pi, 0]
            first = jnp.logical_and(sb == 0, b == 0)
            s = jnp.where(jnp.logical_and(first, col < BLK), NEG, s)
            m = jnp.max(s, axis=-1, keepdims=True)
            p = jnp.exp2(s - m)
            den = jnp.sum(p, axis=-1, keepdims=True)
            o = jnp.dot(p.astype(BF16), v2, preferred_element_type=F32) / den
            lse = m + jnp.log2(den)
            op_s[pi, rows(q0, BLK), :] = o
            lse_s[pi, rows(q0, BLK), :] = jnp.broadcast_to(lse, (BLK, HEAD_DIM))

        def body(it, carry, block=block):
            for u in range(DILATED_UNROLL):
                block(it * DILATED_UNROLL + u)
            return carry

        lax.fori_loop(0, SUPER // BLK // DILATED_UNROLL, body, 0)

    chunk = 256
    for c in range(SUPER // chunk):
        sl = slice(c * chunk, (c + 1) * chunk)
        l0, l1, l2 = lse_s[0, sl, :], lse_s[1, sl, :], lse_s[2, sl, :]
        mx = jnp.maximum(jnp.maximum(l0, l1), l2)
        e0, e1, e2 = jnp.exp2(l0 - mx), jnp.exp2(l1 - mx), jnp.exp2(l2 - mx)
        num = e0 * op_s[0, sl, :] + e1 * op_s[1, sl, :] + e2 * op_s[2, sl, :]
        o_ref[sl, :] = (num / (e0 + e1 + e2)).astype(o_ref.dtype)


def dilated_mixer(proj, bias, q_g, k_g):
    s = proj.shape[0]
    nsb = s // SUPER
    h8 = N_HEADS_A
    blk = (SUPER, HEAD_DIM)
    return pl.pallas_call(
        _dilated_kernel,
        grid=(h8, nsb),
        in_specs=[
            pl.BlockSpec(blk, lambda h, i: (i, h)),
            pl.BlockSpec(blk, lambda h, i: (jnp.maximum(i - 1, 0), h8 + h)),
            pl.BlockSpec(blk, lambda h, i: (i, h8 + h)),
            pl.BlockSpec(blk, lambda h, i: (jnp.maximum(i - 1, 0), 2 * h8 + h)),
            pl.BlockSpec(blk, lambda h, i: (i, 2 * h8 + h)),
            pl.BlockSpec((len(DILATED_PATTERNS), 1, BLK, 2 * BLK), lambda h, i: (0, h, 0, 0)),
            pl.BlockSpec((1, HEAD_DIM), lambda h, i: (0, 0)),
            pl.BlockSpec((1, HEAD_DIM), lambda h, i: (0, 0)),
        ],
        out_specs=pl.BlockSpec(blk, lambda h, i: (i, h)),
        out_shape=jax.ShapeDtypeStruct((s, WIDTH_A), BF16),
        scratch_shapes=[
            pltpu.VMEM((SUPER, HEAD_DIM), F32),
            pltpu.VMEM((2 * SUPER, HEAD_DIM), F32),
            pltpu.VMEM((2 * SUPER, HEAD_DIM), F32),
            pltpu.VMEM((len(DILATED_PATTERNS), SUPER, HEAD_DIM), F32),
            pltpu.VMEM((len(DILATED_PATTERNS), SUPER, HEAD_DIM), F32),
        ],
        compiler_params=_params(("parallel", "arbitrary")),
        name="dilated_mixer",
    )(proj, proj, proj, proj, proj, bias, q_g, k_g)


def _mla_prep_kernel(cq_ref, ckv_ref, kr_ref, qag_ref, kvag_ref, wq_ref, wk_ref, wv_ref,
                     qg_ref, kg_ref, c_ref, s1_ref, s2_ref, q_out, k_out, v_out):
    c, s1, s2 = c_ref[...], s1_ref[...], s2_ref[...]

    def rope(x):
        return x * c + pltpu.roll(x, 96, 1) * s1 + pltpu.roll(x, 32, 1) * s2

    cqn = _rms(cq_ref[...], qag_ref[...]).astype(BF16)
    ckvn = _rms(ckv_ref[...], kvag_ref[...]).astype(BF16)
    qpre = jnp.dot(cqn, wq_ref[...], preferred_element_type=F32)
    knope = jnp.dot(ckvn, wk_ref[...], preferred_element_type=F32)
    v = jnp.dot(ckvn, wv_ref[...], preferred_element_type=F32)
    kr = kr_ref[...]
    kr_ss = jnp.sum(kr * kr, axis=-1, keepdims=True)
    qg, kg = qg_ref[...], kg_ref[...]
    kr_roped = rope(kr * kg[:, NOPE_DIM:])
    inv_n = 1.0 / QK_DIM_B
    ones = jnp.ones((VT_ROWS - V_DIM, cq_ref.shape[0]), BF16)
    for h in range(N_HEADS_B):
        qh = qpre[:, h * QK_PAD_B:(h + 1) * QK_PAD_B]
        rs = lax.rsqrt(jnp.sum(qh * qh, axis=-1, keepdims=True) * inv_n + EPS)
        q_out[h, 0:NOPE_DIM, :] = (qh[:, :NOPE_DIM] * rs * qg[:, :NOPE_DIM]).T.astype(BF16)
        q_out[h, NOPE_DIM:QK_PAD_B, :] = rope(qh[:, NOPE_DIM:] * rs * qg[:, NOPE_DIM:]).T.astype(BF16)
        kh = knope[:, h * NOPE_DIM:(h + 1) * NOPE_DIM]
        rs = lax.rsqrt((jnp.sum(kh * kh, axis=-1, keepdims=True) + kr_ss) * inv_n + EPS)
        k_out[h, :, 0:NOPE_DIM] = (kh * rs * kg[:, :NOPE_DIM]).astype(BF16)
        k_out[h, :, NOPE_DIM:QK_PAD_B] = (kr_roped * rs).astype(BF16)
        v_out[h, 0:V_DIM, :] = v[:, h * V_DIM:(h + 1) * V_DIM].T.astype(BF16)
        v_out[h, V_DIM:VT_ROWS, :] = ones


def mla_prep(proj, q_a_g, kv_a_g, wq, wk, wv, qg, kg, rope_c, rope_s1, rope_s2, *, tm):
    s = proj.shape[0]
    hb = N_HEADS_B
    full = lambda shape: pl.BlockSpec(shape, lambda i: (0,) * len(shape))
    cq_blk = (3 * WIDTH_A) // Q_LORA
    ckv_blk = (3 * WIDTH_A + Q_LORA) // KV_LORA
    kr_blk = (3 * WIDTH_A + Q_LORA + KV_LORA) // 128
    return pl.pallas_call(
        _mla_prep_kernel,
        grid=(s // tm,),
        in_specs=[
            pl.BlockSpec((tm, Q_LORA), lambda i: (i, cq_blk)),
            pl.BlockSpec((tm, KV_LORA), lambda i: (i, ckv_blk)),
            pl.BlockSpec((tm, 128), lambda i: (i, kr_blk)),
            full((1, Q_LORA)), full((1, KV_LORA)),
            full(wq.shape), full(wk.shape), full(wv.shape),
            full((1, QK_PAD_B)), full((1, QK_PAD_B)),
            pl.BlockSpec((tm, 128), lambda i: (i, 0)),
            pl.BlockSpec((tm, 128), lambda i: (i, 0)),
            pl.BlockSpec((tm, 128), lambda i: (i, 0)),
        ],
        out_specs=[
            pl.BlockSpec((hb, QK_PAD_B, tm), lambda i: (0, 0, i)),
            pl.BlockSpec((hb, tm, QK_PAD_B), lambda i: (0, i, 0)),
            pl.BlockSpec((hb, VT_ROWS, tm), lambda i: (0, 0, i)),
        ],
        out_shape=[
            jax.ShapeDtypeStruct((hb, QK_PAD_B, s), BF16),
            jax.ShapeDtypeStruct((hb, s, QK_PAD_B), BF16),
            jax.ShapeDtypeStruct((hb, VT_ROWS, s), BF16),
        ],
        compiler_params=_params(("parallel",)),
        name="mla_prep",
    )(proj, proj, proj, q_a_g, kv_a_g, wq, wk, wv, qg, kg, rope_c, rope_s1, rope_s2)


def _flash_kernel(qt_ref, k_ref, vt_ref, o_ref, s_a, s_b, m_s, acc_s, *, tq, tk):
    i = pl.program_id(1)
    qt = qt_ref[0]
    m_s[...] = jnp.full(m_s.shape, NEG, F32)
    acc_s[...] = jnp.zeros(acc_s.shape, F32)

    def compute(c, dst):
        start = pl.multiple_of(c * tk, tk)
        dst[...] = jnp.dot(k_ref[0, pl.ds(start, tk), :], qt, preferred_element_type=F32)

    def process(c, src, masked):
        s = src[...]
        if masked:
            key = c * tk + lax.broadcasted_iota(jnp.int32, (tk, tq), 0)
            qry = i * tq + lax.broadcasted_iota(jnp.int32, (tk, tq), 1)
            s = jnp.where(key <= qry, s, NEG)
        m_prev = m_s[...]
        m_new = jnp.maximum(m_prev, jnp.max(s, axis=0, keepdims=True))
        alpha = jnp.exp2(m_prev - m_new)
        p = jnp.exp2(s - m_new).astype(BF16)
        start = pl.multiple_of(c * tk, tk)
        pv = jnp.dot(vt_ref[0, :, pl.ds(start, tk)], p, preferred_element_type=F32)
        acc_s[...] = alpha * acc_s[...] + pv
        m_s[...] = m_new

    assert tq == 2 * tk and FLASH_PER_TRIP % 2 == 0

    def run(c0, n):
        for u in range(n):
            src, dst = (s_a, s_b) if u % 2 == 0 else (s_b, s_a)
            compute(c0 + u + 1, dst)
            process(c0 + u, src, False)

    compute(0, s_a)
    n_full = 2 * i
    trips = n_full // FLASH_PER_TRIP

    def trip(t, carry):
        run(t * FLASH_PER_TRIP, FLASH_PER_TRIP)
        return carry

    lax.fori_loop(0, trips, trip, 0)
    rem = n_full - trips * FLASH_PER_TRIP
    for r in range(2, FLASH_PER_TRIP, 2):
        @pl.when(rem == r)
        def _(r=r):
            run(trips * FLASH_PER_TRIP, r)

    compute(n_full + 1, s_b)
    process(n_full, s_a, True)
    process(n_full + 1, s_b, True)
    acc = acc_s[...]
    o_t = acc[0:V_DIM, :] / acc[V_DIM:V_DIM + 1, :]
    o_ref[...] = o_t.T.astype(o_ref.dtype)


def mla_flash(qt, k, vt, *, tq, tk):
    hb, s, _ = k.shape
    return pl.pallas_call(
        functools.partial(_flash_kernel, tq=tq, tk=tk),
        grid=(hb, s // tq),
        in_specs=[
            pl.BlockSpec((1, QK_PAD_B, tq), lambda h, i: (h, 0, i)),
            pl.BlockSpec((1, s, QK_PAD_B), lambda h, i: (h, 0, 0)),
            pl.BlockSpec((1, VT_ROWS, s), lambda h, i: (h, 0, 0)),
        ],
        out_specs=pl.BlockSpec((tq, V_DIM), lambda h, i: (i, h)),
        out_shape=jax.ShapeDtypeStruct((s, WIDTH_B), BF16),
        scratch_shapes=[
            pltpu.VMEM((tk, tq), F32),
            pltpu.VMEM((tk, tq), F32),
            pltpu.VMEM((1, tq), F32),
            pltpu.VMEM((VT_ROWS, tq), F32),
        ],
        compiler_params=_params(("parallel", "arbitrary")),
        name="mla_flash",
    )(qt, k, vt)


def _out_proj_kernel(x_ref, a_ref, b_ref, wa_ref, wb_ref, o_ref):
    acc = jnp.dot(a_ref[...], wa_ref[...], preferred_element_type=F32)
    acc = acc + jnp.dot(b_ref[...], wb_ref[...], preferred_element_type=F32)
    o_ref[...] = x_ref[...] + acc


def out_proj(x, a, b, wa, wb, *, tm, tn):
    s, n = x.shape
    ka, kb = a.shape[1], b.shape[1]
    return pl.pallas_call(
        _out_proj_kernel,
        grid=(s // tm, n // tn),
        in_specs=[
            pl.BlockSpec((tm, tn), lambda i, j: (i, j)),
            pl.BlockSpec((tm, ka), lambda i, j: (i, 0)),
            pl.BlockSpec((tm, kb), lambda i, j: (i, 0)),
            pl.BlockSpec((ka, tn), lambda i, j: (0, j)),
            pl.BlockSpec((kb, tn), lambda i, j: (0, j)),
        ],
        out_specs=pl.BlockSpec((tm, tn), lambda i, j: (i, j)),
        out_shape=jax.ShapeDtypeStruct((s, n), F32),
        compiler_params=_params(("parallel", "arbitrary")),
        name="out_proj",
    )(x, a, b, wa, wb)


def _swiglu_step(h, wg, wu, wd):
    gate = jnp.dot(h, wg, preferred_element_type=F32)
    up = jnp.dot(h, wu, preferred_element_type=F32)
    act = gate * jax.nn.sigmoid(gate) * up
    return jnp.dot(act.astype(BF16), wd, preferred_element_type=F32)


def _ffn_kernel(x_ref, g_ref, wg_ref, wu_ref, wd_ref, o_ref, hn_ref):
    @pl.when(pl.program_id(1) == 0)
    def _():
        hn_ref[...] = _rms(x_ref[...], g_ref[...]).astype(BF16)
        o_ref[...] = x_ref[...]

    o_ref[...] += _swiglu_step(hn_ref[...], wg_ref[...], wu_ref[...], wd_ref[...])


def ffn(x, g, wg, wu, wd, *, tm, tf):
    s, d = x.shape
    fdim = wg.shape[1]
    return pl.pallas_call(
        _ffn_kernel,
        grid=(s // tm, fdim // tf),
        in_specs=[
            pl.BlockSpec((tm, d), lambda i, f: (i, 0)),
            pl.BlockSpec((1, d), lambda i, f: (0, 0)),
            pl.BlockSpec((d, tf), lambda i, f: (0, f)),
            pl.BlockSpec((d, tf), lambda i, f: (0, f)),
            pl.BlockSpec((tf, d), lambda i, f: (f, 0)),
        ],
        out_specs=pl.BlockSpec((tm, d), lambda i, f: (i, 0)),
        out_shape=jax.ShapeDtypeStruct((s, d), F32),
        scratch_shapes=[pltpu.VMEM((tm, d), BF16)],
        compiler_params=_params(("parallel", "arbitrary")),
        name="ffn",
    )(x, g, wg, wu, wd)


def _pack_rows(y, dst):
    n = y.shape[0]
    for s in range(ROW_SUB):
        lo = y[:, s * 128:(s + 1) * 128].astype(BF16).astype(F32)
        hi = y[:, HALF + s * 128:HALF + (s + 1) * 128].astype(BF16).astype(F32)
        w = (lax.bitcast_convert_type(lo, U32) >> 16) | lax.bitcast_convert_type(hi, U32)
        dst[pl.ds(s, n, stride=ROW_SUB), :] = w


def _unpack_rows(src, n):
    lo, hi = [], []
    for s in range(ROW_SUB):
        w = src[pl.ds(s, n, stride=ROW_SUB), :]
        lo.append(lax.bitcast_convert_type(w << 16, F32))
        hi.append(lax.bitcast_convert_type(w & jnp.uint32(0xFFFF0000), F32))
    return lo + hi


META_I1, META_I2, META_R1, META_R2, META_G1, META_G2 = range(6)


def _router_kernel(x_ref, g_ref, wr_ref, hp_ref, meta_ref, cnt_ref, carry):
    @pl.when(pl.program_id(0) == 0)
    def _():
        carry[...] = jnp.zeros(carry.shape, F32)

    hn = _rms(x_ref[...], g_ref[...])
    _pack_rows(hn, hp_ref)
    logits = jnp.dot(hn, wr_ref[...], preferred_element_type=F32, precision=lax.Precision.HIGHEST)
    lane = lax.broadcasted_iota(jnp.int32, logits.shape, 1)
    logits = jnp.where(lane < N_EXPERTS, logits, NEG)
    v1 = jnp.max(logits, axis=-1, keepdims=True)
    i1 = jnp.min(jnp.where(logits == v1, lane, 128), axis=-1, keepdims=True)
    rest = jnp.where(lane == i1, NEG, logits)
    v2 = jnp.max(rest, axis=-1, keepdims=True)
    i2 = jnp.min(jnp.where(rest == v2, lane, 128), axis=-1, keepdims=True)
    e2 = jnp.exp(v2 - v1)
    g1 = 1.0 / (1.0 + e2)
    g2 = e2 / (1.0 + e2)

    tm = hn.shape[0]
    member = jnp.where(jnp.logical_or(lane == i1, lane == i2), 1.0, 0.0)
    row = lax.broadcasted_iota(jnp.int32, (tm, tm), 0)
    col = lax.broadcasted_iota(jnp.int32, (tm, tm), 1)
    earlier = jnp.where(col < row, 1.0, 0.0).astype(BF16)
    rank = carry[...] + jnp.dot(earlier, member.astype(BF16), preferred_element_type=F32)
    r1 = jnp.sum(jnp.where(lane == i1, rank, 0.0), axis=-1, keepdims=True)
    r2 = jnp.sum(jnp.where(lane == i2, rank, 0.0), axis=-1, keepdims=True)
    carry[...] += jnp.sum(member, axis=0, keepdims=True)
    cnt_ref[...] = carry[...]
    meta = jnp.zeros(logits.shape, F32)
    for pos, val in ((META_I1, i1.astype(F32)), (META_I2, i2.astype(F32)), (META_R1, r1),
                     (META_R2, r2), (META_G1, g1), (META_G2, g2)):
        meta = jnp.where(lane == pos, val, meta)
    meta_ref[...] = meta


def router(x, g, wr, *, tm):
    s, d = x.shape
    return pl.pallas_call(
        _router_kernel,
        grid=(s // tm,),
        in_specs=[
            pl.BlockSpec((tm, d), lambda i: (i, 0)),
            pl.BlockSpec((1, d), lambda i: (0, 0)),
            pl.BlockSpec((d, 128), lambda i: (0, 0)),
        ],
        out_specs=[
            pl.BlockSpec((tm * ROW_SUB, 128), lambda i: (i, 0)),
            pl.BlockSpec((tm, 128), lambda i: (i, 0)),
            pl.BlockSpec((1, 128), lambda i: (0, 0)),
        ],
        out_shape=[
            jax.ShapeDtypeStruct((s * ROW_SUB, 128), U32),
            jax.ShapeDtypeStruct((s, 128), F32),
            jax.ShapeDtypeStruct((1, 128), F32),
        ],
        scratch_shapes=[pltpu.VMEM((1, 128), F32)],
        compiler_params=_params(("arbitrary",)),
        name="router",
    )(x, g, wr)


def _row_tile(ref, r):
    return ref.at[pl.ds(pl.multiple_of(r * ROW_SUB, ROW_SUB), ROW_SUB), :]


def _dispatch_kernel(dest_ref, hp_ref, xs_in, xs_out, sem, *, tm):
    del xs_in
    base = pl.program_id(0) * tm

    def row_copy(t, k):
        return pltpu.make_async_copy(_row_tile(hp_ref, t), _row_tile(xs_out, dest_ref[2 * (base + t) + k]), sem)

    def issue(t, c):
        row_copy(t, 0).start()
        row_copy(t, 1).start()
        return c

    def drain(t, c):
        row_copy(t, 0).wait()
        row_copy(t, 1).wait()
        return c

    lax.fori_loop(0, tm, issue, 0)
    lax.fori_loop(0, tm, drain, 0)


def dispatch(dest, hp, n_rows, *, tm):
    s = hp.shape[0] // ROW_SUB
    xs0 = jnp.zeros((n_rows * ROW_SUB, 128), U32)
    return pl.pallas_call(
        functools.partial(_dispatch_kernel, tm=tm),
        grid_spec=pltpu.PrefetchScalarGridSpec(
            num_scalar_prefetch=1,
            grid=(s // tm,),
            in_specs=[
                pl.BlockSpec((tm * ROW_SUB, 128), lambda i, dest: (i, 0)),
                pl.BlockSpec(memory_space=pl.ANY),
            ],
            out_specs=pl.BlockSpec(memory_space=pl.ANY),
            scratch_shapes=[pltpu.SemaphoreType.DMA(())],
        ),
        out_shape=jax.ShapeDtypeStruct(xs0.shape, U32),
        input_output_aliases={2: 0},
        compiler_params=_params(("arbitrary",)),
        name="moe_dispatch",
    )(dest, hp, xs0)


def _moe_ffn_kernel(te_ref, nu_ref, xs_ref, wg_ref, wu_ref, wd_ref, ys_ref, xb, acc, *, tm):
    del te_ref
    f = pl.program_id(1)
    used = pl.program_id(0) < nu_ref[0]
    last = f == pl.num_programs(1) - 1

    @pl.when(jnp.logical_and(used, f == 0))
    def _():
        for c, v in enumerate(_unpack_rows(xs_ref, tm)):
            xb[:, c * 128:(c + 1) * 128] = v.astype(BF16)
        acc[...] = jnp.zeros(acc.shape, F32)

    @pl.when(used)
    def _():
        acc[...] += _swiglu_step(xb[...], wg_ref[0], wu_ref[0], wd_ref[0])

    @pl.when(jnp.logical_and(used, last))
    def _():
        _pack_rows(acc[...], ys_ref)

    @pl.when(jnp.logical_and(jnp.logical_not(used), last))
    def _():
        ys_ref[...] = jnp.zeros(ys_ref.shape, U32)


def moe_ffn(tile_expert, n_used, xs, wg, wu, wd, *, tm, tf):
    nt = tile_expert.shape[0]
    _, d, fdim = wg.shape
    nf = fdim // tf

    def f_idx(j, f, nu):
        return jnp.where(j < nu[0], f, nf - 1)

    return pl.pallas_call(
        functools.partial(_moe_ffn_kernel, tm=tm),
        grid_spec=pltpu.PrefetchScalarGridSpec(
            num_scalar_prefetch=2,
            grid=(nt, nf),
            in_specs=[
                pl.BlockSpec((tm * ROW_SUB, 128), lambda j, f, te, nu: (jnp.minimum(j, nu[0] - 1), 0)),
                pl.BlockSpec((1, d, tf), lambda j, f, te, nu: (te[j], 0, f_idx(j, f, nu))),
                pl.BlockSpec((1, d, tf), lambda j, f, te, nu: (te[j], 0, f_idx(j, f, nu))),
                pl.BlockSpec((1, tf, d), lambda j, f, te, nu: (te[j], f_idx(j, f, nu), 0)),
            ],
            out_specs=pl.BlockSpec((tm * ROW_SUB, 128), lambda j, f, te, nu: (j, 0)),
            scratch_shapes=[pltpu.VMEM((tm, d), BF16), pltpu.VMEM((tm, d), F32)],
        ),
        out_shape=jax.ShapeDtypeStruct(xs.shape, U32),
        compiler_params=_params(("parallel", "arbitrary")),
        name="moe_ffn",
    )(tile_expert, n_used, xs, wg, wu, wd)


def _combine_kernel(dest_ref, x_ref, meta_ref, ys_hbm, o_ref, buf, sem, *, tm):
    base = pl.program_id(0) * tm

    def row_copy(t, k):
        return pltpu.make_async_copy(_row_tile(ys_hbm, dest_ref[2 * (base + t) + k]), _row_tile(buf.at[k], t), sem)

    def issue(t, c):
        row_copy(t, 0).start()
        row_copy(t, 1).start()
        return c

    def drain(t, c):
        row_copy(t, 0).wait()
        row_copy(t, 1).wait()
        return c

    lax.fori_loop(0, tm, issue, 0)
    lax.fori_loop(0, tm, drain, 0)
    meta = meta_ref[...]
    g1 = meta[:, META_G1:META_G1 + 1]
    g2 = meta[:, META_G2:META_G2 + 1]
    y1 = _unpack_rows(buf.at[0], tm)
    y2 = _unpack_rows(buf.at[1], tm)
    for c in range(len(y1)):
        sl = slice(c * 128, (c + 1) * 128)
        o_ref[:, sl] = x_ref[:, sl] + g1 * y1[c] + g2 * y2[c]


def combine(dest, x, meta, ys, *, tm):
    s, d = x.shape
    return pl.pallas_call(
        functools.partial(_combine_kernel, tm=tm),
        grid_spec=pltpu.PrefetchScalarGridSpec(
            num_scalar_prefetch=1,
            grid=(s // tm,),
            in_specs=[
                pl.BlockSpec((tm, d), lambda i, dest: (i, 0)),
                pl.BlockSpec((tm, 128), lambda i, dest: (i, 0)),
                pl.BlockSpec(memory_space=pl.ANY),
            ],
            out_specs=pl.BlockSpec((tm, d), lambda i, dest: (i, 0)),
            scratch_shapes=[pltpu.VMEM((2, tm * ROW_SUB, 128), U32), pltpu.SemaphoreType.DMA(())],
        ),
        out_shape=jax.ShapeDtypeStruct((s, d), F32),
        compiler_params=_params(("arbitrary",)),
        name="moe_combine",
    )(dest, x, meta, ys)


def moe(x, g, w_router, wg, wu, wd, *, tm_rows, tf):
    s = x.shape[0]
    hp, meta, cnt = router(x, g, w_router, tm=512)
    i1, i2 = meta[:, META_I1].astype(jnp.int32), meta[:, META_I2].astype(jnp.int32)
    r1, r2 = meta[:, META_R1].astype(jnp.int32), meta[:, META_R2].astype(jnp.int32)
    counts = cnt[0, :N_EXPERTS].astype(jnp.int32)
    padded = (counts + tm_rows - 1) // tm_rows * tm_rows
    ends = jnp.cumsum(padded)
    offs = ends - padded
    eids = jnp.arange(N_EXPERTS, dtype=jnp.int32)
    off1 = jnp.sum(jnp.where(i1[:, None] == eids, offs, 0), axis=1)
    off2 = jnp.sum(jnp.where(i2[:, None] == eids, offs, 0), axis=1)
    dest = jnp.stack([off1 + r1, off2 + r2], axis=1).reshape(-1)
    n_tiles = (TOP_K * s) // tm_rows + N_EXPERTS
    n_used = (ends[-1] // tm_rows).reshape(1)
    tile_start = jnp.minimum(jnp.arange(n_tiles, dtype=jnp.int32), n_used - 1) * tm_rows
    tile_expert = jnp.sum(tile_start[:, None] >= ends[None, :], axis=1).astype(jnp.int32)
    xs = dispatch(dest, hp, n_tiles * tm_rows, tm=256)
    ys = moe_ffn(tile_expert, n_used, xs, wg, wu, wd, tm=tm_rows, tf=tf)
    return combine(dest, x, meta, ys, tm=256)


def _pad_cols(w, n):
    return jnp.pad(w, ((0, 0), (0, n - w.shape[1])))


def _mla_weights(w_q_b, w_kv_b):
    wq = w_q_b.reshape(Q_LORA, N_HEADS_B, QK_DIM_B)
    wq = jnp.pad(wq, ((0, 0), (0, 0), (0, QK_PAD_B - QK_DIM_B))).reshape(Q_LORA, N_HEADS_B * QK_PAD_B)
    wkv = w_kv_b.reshape(KV_LORA, N_HEADS_B, NOPE_DIM + V_DIM)
    wk = wkv[:, :, :NOPE_DIM].reshape(KV_LORA, N_HEADS_B * NOPE_DIM)
    wv = wkv[:, :, NOPE_DIM:].reshape(KV_LORA, N_HEADS_B * V_DIM)
    return wq.astype(BF16), wk.astype(BF16), wv.astype(BF16)


def _rope_tables(positions):
    half = ROPE_DIM // 2
    inv = ROPE_THETA ** (-jnp.arange(half, dtype=F32) / half)
    ang = positions.astype(F32)[:, None] * inv
    cos, sin = jnp.cos(ang), jnp.sin(ang)
    z = jnp.zeros_like(cos)
    c = jnp.concatenate([cos, cos, z, z], axis=-1)
    s1 = jnp.concatenate([-sin, z, z, z], axis=-1)
    s2 = jnp.concatenate([z, sin, z, z], axis=-1)
    return c, s1, s2


def kernel(x, positions, rel_bias_table, norm_mix_g, w_in, q_a_norm_g, kv_a_norm_g, w_q_b, w_kv_b, q_norm_a_g, k_norm_a_g, q_norm_b_g, k_norm_b_g, w_out, norm_ffn_g, w_ff_gate, w_ff_up, w_ff_down, w_router, w_exp_gate, w_exp_up, w_exp_down):
    batch, seq, d = x.shape
    depth = w_in.shape[0]
    outs = []
    bias = _dilated_bias(rel_bias_table)
    for bi in range(batch):
        xs = x[bi]
        rope_c, rope_s1, rope_s2 = _rope_tables(positions[bi])
        for l in range(depth):
            w_in_l = _pad_cols(w_in[l], PROJ_COLS).astype(BF16)
            proj = norm_matmul(xs, norm_mix_g[l][None], w_in_l, tm=1024, tn=1024, out_dtype=F32)
            a = dilated_mixer(proj, bias, q_norm_a_g[l][None], k_norm_a_g[l][None])
            wq, wk, wv = _mla_weights(w_q_b[l], w_kv_b[l])
            qg = _pad_cols(q_norm_b_g[l][None] * (QK_DIM_B ** -0.5 * LOG2E), QK_PAD_B)
            kg = _pad_cols(k_norm_b_g[l][None], QK_PAD_B)
            qb, kb, vb = mla_prep(proj, q_a_norm_g[l][None], kv_a_norm_g[l][None], wq, wk, wv,
                                  qg, kg, rope_c, rope_s1, rope_s2, tm=512)
            b = mla_flash(qb, kb, vb, tq=1024, tk=512)
            w_out_l = w_out[l].astype(BF16)
            xs = out_proj(xs, a, b, w_out_l[:WIDTH_A], w_out_l[WIDTH_A:], tm=1024, tn=1024)
            gf = norm_ffn_g[l][None]
            if l % 2 == 0:
                i = l // 2
                xs = ffn(xs, gf, w_ff_gate[i].astype(BF16), w_ff_up[i].astype(BF16),
                         w_ff_down[i].astype(BF16), tm=512, tf=512)
            else:
                i = l // 2
                xs = moe(xs, gf, _pad_cols(w_router[i], 128), w_exp_gate[i].astype(BF16),
                         w_exp_up[i].astype(BF16), w_exp_down[i].astype(BF16), tm_rows=512, tf=512)
        outs.append(xs)
    return jnp.stack(outs, axis=0)
```

```python
import functools
import math

import jax
import jax.numpy as jnp
from jax import lax
from jax.experimental import pallas as pl
from jax.experimental.pallas import tpu as pltpu

F32 = jnp.float32
BF16 = jnp.bfloat16

D_MODEL = 2048
HEAD_DIM = 128
N_HEADS_A = 8
DILATED_PATTERNS = ((128, 1), (512, 4), (2048, 16))
BLK = 128
N_BUCKETS = 32
MAX_DISTANCE = 2048
N_HEADS_B = 8
Q_LORA = 512
KV_LORA = 256
NOPE_DIM = 128
ROPE_DIM = 64
V_DIM = 128
QK_DIM_B = NOPE_DIM + ROPE_DIM
QK_PAD_B = 256
VT_ROWS = V_DIM + 16
LOG2E = math.log2(math.e)
ROPE_THETA = 10000.0
WIDTH_A = N_HEADS_A * HEAD_DIM
WIDTH_B = N_HEADS_B * V_DIM
IN_COLS = 3 * WIDTH_A + Q_LORA + KV_LORA + ROPE_DIM
PROJ_COLS = 4096
D_FF = 5632
N_EXPERTS = 8
TOP_K = 2
ROW_SUB = 8
HALF = D_MODEL // 2
U32 = jnp.uint32
EPS = 1e-6
NEG = -1e30

SUPER = 2048
DILATED_UNROLL = 16
VMEM_LIMIT = 56 * 1024 * 1024


def _params(sem, vmem=VMEM_LIMIT):
    return pltpu.CompilerParams(dimension_semantics=sem, vmem_limit_bytes=vmem)


def _rms(x, g, n=None):
    ss = jnp.sum(x * x, axis=-1, keepdims=True)
    n = x.shape[-1] if n is None else n
    return x * lax.rsqrt(ss * (1.0 / n) + EPS) * g


def _norm_matmul_kernel(x_ref, g_ref, w_ref, o_ref, hn_ref):
    @pl.when(pl.program_id(1) == 0)
    def _():
        hn_ref[...] = _rms(x_ref[...], g_ref[...]).astype(BF16)

    o_ref[...] = jnp.dot(hn_ref[...], w_ref[...], preferred_element_type=F32).astype(o_ref.dtype)


def norm_matmul(x, g, w, *, tm, tn, out_dtype):
    s, k = x.shape
    n = w.shape[1]
    return pl.pallas_call(
        _norm_matmul_kernel,
        grid=(s // tm, n // tn),
        in_specs=[
            pl.BlockSpec((tm, k), lambda i, j: (i, 0)),
            pl.BlockSpec((1, k), lambda i, j: (0, 0)),
            pl.BlockSpec((k, tn), lambda i, j: (0, j)),
        ],
        out_specs=pl.BlockSpec((tm, tn), lambda i, j: (i, j)),
        out_shape=jax.ShapeDtypeStruct((s, n), out_dtype),
        scratch_shapes=[pltpu.VMEM((tm, k), BF16)],
        compiler_params=_params(("parallel", "arbitrary")),
        name="norm_matmul",
    )(x, g, w)


def _t5_bucket(n):
    max_exact = N_BUCKETS // 2
    nf = jnp.maximum(n, 1).astype(F32)
    large = max_exact + (jnp.log(nf / max_exact) / math.log(MAX_DISTANCE / max_exact)
                         * (N_BUCKETS - max_exact)).astype(jnp.int32)
    large = jnp.minimum(large, N_BUCKETS - 1)
    return jnp.where(n < max_exact, n, large)


def _dilated_bias(rel_bias):
    i = jnp.arange(BLK)[:, None]
    j = jnp.arange(2 * BLK)[None, :]
    delta = i + BLK - j
    out = []
    for window, dilation in DILATED_PATTERNS:
        band = (delta >= 0) & (delta <= window // dilation)
        bucket = _t5_bucket(jnp.maximum(delta, 0) * dilation)
        onehot = (bucket[None] == jnp.arange(N_BUCKETS)[:, None, None]).astype(F32)
        bias = jnp.einsum('nij,nh->hij', onehot, rel_bias.astype(F32), precision=lax.Precision.HIGHEST)
        out.append(jnp.where(band[None], bias * LOG2E, NEG))
    return jnp.stack(out, axis=0)


def _dilated_kernel(q_ref, kp_ref, kc_ref, vp_ref, vc_ref, bias_ref, qg_ref, kg_ref, o_ref,
                    qn_s, kn_s, v_s, op_s, lse_s):
    sb = pl.program_id(1)
    qn_s[...] = _rms(q_ref[...], qg_ref[...]) * (HEAD_DIM ** -0.5 * LOG2E)
    kn_s[0:SUPER, :] = _rms(kp_ref[...], kg_ref[...])
    kn_s[SUPER:2 * SUPER, :] = _rms(kc_ref[...], kg_ref[...])
    v_s[0:SUPER, :] = vp_ref[...]
    v_s[SUPER:2 * SUPER, :] = vc_ref[...]
    col = lax.broadcasted_iota(jnp.int32, (BLK, 2 * BLK), 1)

    for pi, (_, d) in enumerate(DILATED_PATTERNS):
        def rows(start, size, d=d):
            return pl.ds(start, size) if d == 1 else pl.ds(start, size, stride=d)

        def block(idx, pi=pi, d=d, rows=rows):
            r = idx % d
            b = idx // d
            q0 = r + d * BLK * b
            k0 = SUPER + q0 - d * BLK
            qb = qn_s[rows(q0, BLK), :].astype(BF16)
            k2 = kn_s[rows(k0, 2 * BLK), :].astype(BF16)
            v2 = v_s[rows(k0, 2 * BLK), :].astype(BF16)
            s = lax.dot_general(qb, k2, (((1,), (1,)), ((), ())), preferred_element_type=F32)
            s = s + bias_ref[pi, 0]
            first = jnp.logical_and(sb == 0, b == 0)
            s = jnp.where(jnp.logical_and(first, col < BLK), NEG, s)
            m = jnp.max(s, axis=-1, keepdims=True)
            p = jnp.exp2(s - m)
            den = jnp.sum(p, axis=-1, keepdims=True)
            o = jnp.dot(p.astype(BF16), v2, preferred_element_type=F32) / den
            lse = m + jnp.log2(den)
            op_s[pi, rows(q0, BLK), :] = o
            lse_s[pi, rows(q0, BLK), :] = jnp.broadcast_to(lse, (BLK, HEAD_DIM))

        def body(it, carry, block=block):
            for u in range(DILATED_UNROLL):
                block(it * DILATED_UNROLL + u)
            return carry

        lax.fori_loop(0, SUPER // BLK // DILATED_UNROLL, body, 0)

    chunk = 256
    for c in range(SUPER // chunk):
        sl = slice(c * chunk, (c + 1) * chunk)
        l0, l1, l2 = lse_s[0, sl, :], lse_s[1, sl, :], lse_s[2, sl, :]
        mx = jnp.maximum(jnp.maximum(l0, l1), l2)
        e0, e1, e2 = jnp.exp2(l0 - mx), jnp.exp2(l1 - mx), jnp.exp2(l2 - mx)
        num = e0 * op_s[0, sl, :] + e1 * op_s[1, sl, :] + e2 * op_s[2, sl, :]
        o_ref[sl, :] = (num / (e0 + e1 + e2)).astype(o_ref.dtype)


def dilated_mixer(proj, bias, q_g, k_g):
    s = proj.shape[0]
    nsb = s // SUPER
    h8 = N_HEADS_A
    blk = (SUPER, HEAD_DIM)
    return pl.pallas_call(
        _dilated_kernel,
        grid=(h8, nsb),
        in_specs=[
            pl.BlockSpec(blk, lambda h, i: (i, h)),
            pl.BlockSpec(blk, lambda h, i: (jnp.maximum(i - 1, 0), h8 + h)),
            pl.BlockSpec(blk, lambda h, i: (i, h8 + h)),
            pl.BlockSpec(blk, lambda h, i: (jnp.maximum(i - 1, 0), 2 * h8 + h)),
            pl.BlockSpec(blk, lambda h, i: (i, 2 * h8 + h)),
            pl.BlockSpec((len(DILATED_PATTERNS), 1, BLK, 2 * BLK), lambda h, i: (0, h, 0, 0)),
            pl.BlockSpec((1, HEAD_DIM), lambda h, i: (0, 0)),
            pl.BlockSpec((1, HEAD_DIM), lambda h, i: (0, 0)),
        ],
        out_specs=pl.BlockSpec(blk, lambda h, i: (i, h)),
        out_shape=jax.ShapeDtypeStruct((s, WIDTH_A), BF16),
        scratch_shapes=[
            pltpu.VMEM((SUPER, HEAD_DIM), F32),
            pltpu.VMEM((2 * SUPER, HEAD_DIM), F32),
            pltpu.VMEM((2 * SUPER, HEAD_DIM), F32),
            pltpu.VMEM((len(DILATED_PATTERNS), SUPER, HEAD_DIM), F32),
            pltpu.VMEM((len(DILATED_PATTERNS), SUPER, HEAD_DIM), F32),
        ],
        compiler_params=_params(("parallel", "arbitrary")),
        name="dilated_mixer",
    )(proj, proj, proj, proj, proj, bias, q_g, k_g)


def _mla_prep_kernel(cq_ref, ckv_ref, kr_ref, qag_ref, kvag_ref, wq_ref, wk_ref, wv_ref,
                     qg_ref, kg_ref, c_ref, s1_ref, s2_ref, q_out, k_out, v_out):
    c, s1, s2 = c_ref[...], s1_ref[...], s2_ref[...]

    def rope(x):
        return x * c + pltpu.roll(x, 96, 1) * s1 + pltpu.roll(x, 32, 1) * s2

    cqn = _rms(cq_ref[...], qag_ref[...]).astype(BF16)
    ckvn = _rms(ckv_ref[...], kvag_ref[...]).astype(BF16)
    qpre = jnp.dot(cqn, wq_ref[...], preferred_element_type=F32)
    knope = jnp.dot(ckvn, wk_ref[...], preferred_element_type=F32)
    v = jnp.dot(ckvn, wv_ref[...], preferred_element_type=F32)
    kr = kr_ref[...]
    kr_ss = jnp.sum(kr * kr, axis=-1, keepdims=True)
    qg, kg = qg_ref[...], kg_ref[...]
    kr_roped = rope(kr * kg[:, NOPE_DIM:])
    inv_n = 1.0 / QK_DIM_B
    ones = jnp.ones((VT_ROWS - V_DIM, cq_ref.shape[0]), BF16)
    for h in range(N_HEADS_B):
        qh = qpre[:, h * QK_PAD_B:(h + 1) * QK_PAD_B]
        rs = lax.rsqrt(jnp.sum(qh * qh, axis=-1, keepdims=True) * inv_n + EPS)
        q_out[h, 0:NOPE_DIM, :] = (qh[:, :NOPE_DIM] * rs * qg[:, :NOPE_DIM]).T.astype(BF16)
        q_out[h, NOPE_DIM:QK_PAD_B, :] = rope(qh[:, NOPE_DIM:] * rs * qg[:, NOPE_DIM:]).T.astype(BF16)
        kh = knope[:, h * NOPE_DIM:(h + 1) * NOPE_DIM]
        rs = lax.rsqrt((jnp.sum(kh * kh, axis=-1, keepdims=True) + kr_ss) * inv_n + EPS)
        k_out[h, :, 0:NOPE_DIM] = (kh * rs * kg[:, :NOPE_DIM]).astype(BF16)
        k_out[h, :, NOPE_DIM:QK_PAD_B] = (kr_roped * rs).astype(BF16)
        v_out[h, 0:V_DIM, :] = v[:, h * V_DIM:(h + 1) * V_DIM].T.astype(BF16)
        v_out[h, V_DIM:VT_ROWS, :] = ones


def mla_prep(proj, q_a_g, kv_a_g, wq, wk, wv, qg, kg, rope_c, rope_s1, rope_s2, *, tm):
    s = proj.shape[0]
    hb = N_HEADS_B
    full = lambda shape: pl.BlockSpec(shape, lambda i: (0,) * len(shape))
    cq_blk = (3 * WIDTH_A) // Q_LORA
    ckv_blk = (3 * WIDTH_A + Q_LORA) // KV_LORA
    kr_blk = (3 * WIDTH_A + Q_LORA + KV_LORA) // 128
    return pl.pallas_call(
        _mla_prep_kernel,
        grid=(s // tm,),
        in_specs=[
            pl.BlockSpec((tm, Q_LORA), lambda i: (i, cq_blk)),
            pl.BlockSpec((tm, KV_LORA), lambda i: (i, ckv_blk)),
            pl.BlockSpec((tm, 128), lambda i: (i, kr_blk)),
            full((1, Q_LORA)), full((1, KV_LORA)),
            full(wq.shape), full(wk.shape), full(wv.shape),
            full((1, QK_PAD_B)), full((1, QK_PAD_B)),
            pl.BlockSpec((tm, 128), lambda i: (i, 0)),
            pl.BlockSpec((tm, 128), lambda i: (i, 0)),
            pl.BlockSpec((tm, 128), lambda i: (i, 0)),
        ],
        out_specs=[
            pl.BlockSpec((hb, QK_PAD_B, tm), lambda i: (0, 0, i)),
            pl.BlockSpec((hb, tm, QK_PAD_B), lambda i: (0, i, 0)),
            pl.BlockSpec((hb, VT_ROWS, tm), lambda i: (0, 0, i)),
        ],
        out_shape=[
            jax.ShapeDtypeStruct((hb, QK_PAD_B, s), BF16),
            jax.ShapeDtypeStruct((hb, s, QK_PAD_B), BF16),
            jax.ShapeDtypeStruct((hb, VT_ROWS, s), BF16),
        ],
        compiler_params=_params(("parallel",)),
        name="mla_prep",
    )(proj, proj, proj, q_a_g, kv_a_g, wq, wk, wv, qg, kg, rope_c, rope_s1, rope_s2)


def _flash_kernel(qt_ref, k_ref, vt_ref, o_ref, s_a, s_b, m_s, acc_s, *, tq, tk):
    i = pl.program_id(1)
    m_s[...] = jnp.full(m_s.shape, NEG, F32)
    acc_s[...] = jnp.zeros(acc_s.shape, F32)
    per_tile = tq // tk
    assert per_tile % 2 == 0

    def compute(c, dst, q_lo=0):
        start = pl.multiple_of(c * tk, tk)
        dst[:, q_lo:] = jnp.dot(k_ref[0, pl.ds(start, tk), :], qt_ref[0, :, q_lo:],
                                preferred_element_type=F32)

    def process(c, src, q_lo=0, masked=False):
        s = src[:, q_lo:]
        if masked:
            key = c * tk + lax.broadcasted_iota(jnp.int32, s.shape, 0)
            qry = i * tq + q_lo + lax.broadcasted_iota(jnp.int32, s.shape, 1)
            s = jnp.where(key <= qry, s, NEG)
        m_prev = m_s[:, q_lo:]
        m_new = jnp.maximum(m_prev, jnp.max(s, axis=0, keepdims=True))
        alpha = jnp.exp2(m_prev - m_new)
        p = jnp.exp2(s - m_new).astype(BF16)
        start = pl.multiple_of(c * tk, tk)
        pv = jnp.dot(vt_ref[0, :, pl.ds(start, tk)], p, preferred_element_type=F32)
        acc_s[:, q_lo:] = alpha * acc_s[:, q_lo:] + pv
        m_s[:, q_lo:] = m_new

    bufs = (s_a, s_b)
    compute(0, s_a)
    n_full = per_tile * i

    def trip(t, carry):
        for u in range(per_tile):
            compute(t * per_tile + u + 1, bufs[(u + 1) % 2])
            process(t * per_tile + u, bufs[u % 2])
        return carry

    lax.fori_loop(0, i, trip, 0)
    for d in range(per_tile):
        if d + 1 < per_tile:
            compute(n_full + d + 1, bufs[(d + 1) % 2], (d + 1) * tk)
        process(n_full + d, bufs[d % 2], d * tk, masked=True)
    acc = acc_s[...]
    o_t = acc[0:V_DIM, :] / acc[V_DIM:V_DIM + 1, :]
    o_ref[...] = o_t.T.astype(o_ref.dtype)


def mla_flash(qt, k, vt, *, tq, tk):
    hb, s, _ = k.shape
    return pl.pallas_call(
        functools.partial(_flash_kernel, tq=tq, tk=tk),
        grid=(hb, s // tq),
        in_specs=[
            pl.BlockSpec((1, QK_PAD_B, tq), lambda h, i: (h, 0, i)),
            pl.BlockSpec((1, s, QK_PAD_B), lambda h, i: (h, 0, 0)),
            pl.BlockSpec((1, VT_ROWS, s), lambda h, i: (h, 0, 0)),
        ],
        out_specs=pl.BlockSpec((tq, V_DIM), lambda h, i: (i, h)),
        out_shape=jax.ShapeDtypeStruct((s, WIDTH_B), BF16),
        scratch_shapes=[
            pltpu.VMEM((tk, tq), F32),
            pltpu.VMEM((tk, tq), F32),
            pltpu.VMEM((1, tq), F32),
            pltpu.VMEM((VT_ROWS, tq), F32),
        ],
        compiler_params=_params(("parallel", "arbitrary")),
        name="mla_flash",
    )(qt, k, vt)


def _out_proj_kernel(x_ref, a_ref, b_ref, wa_ref, wb_ref, o_ref):
    acc = jnp.dot(a_ref[...], wa_ref[...], preferred_element_type=F32)
    acc = acc + jnp.dot(b_ref[...], wb_ref[...], preferred_element_type=F32)
    o_ref[...] = x_ref[...] + acc


def out_proj(x, a, b, wa, wb, *, tm, tn):
    s, n = x.shape
    ka, kb = a.shape[1], b.shape[1]
    return pl.pallas_call(
        _out_proj_kernel,
        grid=(s // tm, n // tn),
        in_specs=[
            pl.BlockSpec((tm, tn), lambda i, j: (i, j)),
            pl.BlockSpec((tm, ka), lambda i, j: (i, 0)),
            pl.BlockSpec((tm, kb), lambda i, j: (i, 0)),
            pl.BlockSpec((ka, tn), lambda i, j: (0, j)),
            pl.BlockSpec((kb, tn), lambda i, j: (0, j)),
        ],
        out_specs=pl.BlockSpec((tm, tn), lambda i, j: (i, j)),
        out_shape=jax.ShapeDtypeStruct((s, n), F32),
        compiler_params=_params(("parallel", "arbitrary")),
        name="out_proj",
    )(x, a, b, wa, wb)


def _swiglu_step(h, wg, wu, wd):
    gate = jnp.dot(h, wg, preferred_element_type=F32)
    up = jnp.dot(h, wu, preferred_element_type=F32)
    act = gate * jax.nn.sigmoid(gate) * up
    return jnp.dot(act.astype(BF16), wd, preferred_element_type=F32)


def _ffn_kernel(x_ref, g_ref, wg_ref, wu_ref, wd_ref, o_ref, hn_ref):
    @pl.when(pl.program_id(1) == 0)
    def _():
        hn_ref[...] = _rms(x_ref[...], g_ref[...]).astype(BF16)
        o_ref[...] = x_ref[...]

    o_ref[...] += _swiglu_step(hn_ref[...], wg_ref[...], wu_ref[...], wd_ref[...])


def ffn(x, g, wg, wu, wd, *, tm, tf):
    s, d = x.shape
    fdim = wg.shape[1]
    return pl.pallas_call(
        _ffn_kernel,
        grid=(s // tm, fdim // tf),
        in_specs=[
            pl.BlockSpec((tm, d), lambda i, f: (i, 0), pipeline_mode=pl.Buffered(1)),
            pl.BlockSpec((1, d), lambda i, f: (0, 0)),
            pl.BlockSpec((d, tf), lambda i, f: (0, f)),
            pl.BlockSpec((d, tf), lambda i, f: (0, f)),
            pl.BlockSpec((tf, d), lambda i, f: (f, 0)),
        ],
        out_specs=pl.BlockSpec((tm, d), lambda i, f: (i, 0)),
        out_shape=jax.ShapeDtypeStruct((s, d), F32),
        scratch_shapes=[pltpu.VMEM((tm, d), BF16)],
        compiler_params=_params(("parallel", "arbitrary")),
        name="ffn",
    )(x, g, wg, wu, wd)


def _pack_rows(y, dst):
    n = y.shape[0]
    for s in range(ROW_SUB):
        lo = y[:, s * 128:(s + 1) * 128].astype(BF16).astype(F32)
        hi = y[:, HALF + s * 128:HALF + (s + 1) * 128].astype(BF16).astype(F32)
        w = (lax.bitcast_convert_type(lo, U32) >> 16) | lax.bitcast_convert_type(hi, U32)
        dst[pl.ds(s, n, stride=ROW_SUB), :] = w


def _unpack_rows(src, n):
    lo, hi = [], []
    for s in range(ROW_SUB):
        w = src[pl.ds(s, n, stride=ROW_SUB), :]
        lo.append(lax.bitcast_convert_type(w << 16, F32))
        hi.append(lax.bitcast_convert_type(w & jnp.uint32(0xFFFF0000), F32))
    return lo + hi


META_I1, META_I2, META_R1, META_R2, META_G1, META_G2 = range(6)


def _router_kernel(x_ref, g_ref, wr_ref, hp_ref, meta_ref, cnt_ref, carry):
    @pl.when(pl.program_id(0) == 0)
    def _():
        carry[...] = jnp.zeros(carry.shape, F32)

    hn = _rms(x_ref[...], g_ref[...])
    _pack_rows(hn, hp_ref)
    logits = jnp.dot(hn, wr_ref[...], preferred_element_type=F32, precision=lax.Precision.HIGHEST)
    lane = lax.broadcasted_iota(jnp.int32, logits.shape, 1)
    logits = jnp.where(lane < N_EXPERTS, logits, NEG)
    v1 = jnp.max(logits, axis=-1, keepdims=True)
    i1 = jnp.min(jnp.where(logits == v1, lane, 128), axis=-1, keepdims=True)
    rest = jnp.where(lane == i1, NEG, logits)
    v2 = jnp.max(rest, axis=-1, keepdims=True)
    i2 = jnp.min(jnp.where(rest == v2, lane, 128), axis=-1, keepdims=True)
    e2 = jnp.exp(v2 - v1)
    g1 = 1.0 / (1.0 + e2)
    g2 = e2 / (1.0 + e2)

    tm = hn.shape[0]
    member = jnp.where(jnp.logical_or(lane == i1, lane == i2), 1.0, 0.0)
    row = lax.broadcasted_iota(jnp.int32, (tm, tm), 0)
    col = lax.broadcasted_iota(jnp.int32, (tm, tm), 1)
    earlier = jnp.where(col < row, 1.0, 0.0).astype(BF16)
    rank = carry[...] + jnp.dot(earlier, member.astype(BF16), preferred_element_type=F32)
    r1 = jnp.sum(jnp.where(lane == i1, rank, 0.0), axis=-1, keepdims=True)
    r2 = jnp.sum(jnp.where(lane == i2, rank, 0.0), axis=-1, keepdims=True)
    carry[...] += jnp.sum(member, axis=0, keepdims=True)
    cnt_ref[...] = carry[...]
    meta = jnp.zeros(logits.shape, F32)
    for pos, val in ((META_I1, i1.astype(F32)), (META_I2, i2.astype(F32)), (META_R1, r1),
                     (META_R2, r2), (META_G1, g1), (META_G2, g2)):
        meta = jnp.where(lane == pos, val, meta)
    meta_ref[...] = meta


def router(x, g, wr, *, tm):
    s, d = x.shape
    return pl.pallas_call(
        _router_kernel,
        grid=(s // tm,),
        in_specs=[
            pl.BlockSpec((tm, d), lambda i: (i, 0)),
            pl.BlockSpec((1, d), lambda i: (0, 0)),
            pl.BlockSpec((d, 128), lambda i: (0, 0)),
        ],
        out_specs=[
            pl.BlockSpec((tm * ROW_SUB, 128), lambda i: (i, 0)),
            pl.BlockSpec((tm, 128), lambda i: (i, 0)),
            pl.BlockSpec((1, 128), lambda i: (0, 0)),
        ],
        out_shape=[
            jax.ShapeDtypeStruct((s * ROW_SUB, 128), U32),
            jax.ShapeDtypeStruct((s, 128), F32),
            jax.ShapeDtypeStruct((1, 128), F32),
        ],
        scratch_shapes=[pltpu.VMEM((1, 128), F32)],
        compiler_params=_params(("arbitrary",)),
        name="router",
    )(x, g, wr)


def _row_tile(ref, r):
    return ref.at[pl.ds(pl.multiple_of(r * ROW_SUB, ROW_SUB), ROW_SUB), :]


def _dispatch_kernel(dest_ref, hp_hbm, xs_in, xs_out, sem, *, n_tok, window):
    del xs_in

    def row_copy(t, k):
        return pltpu.make_async_copy(_row_tile(hp_hbm, t), _row_tile(xs_out, dest_ref[2 * t + k]), sem)

    def issue_window(w):
        def issue(t, c):
            row_copy(w * window + t, 0).start()
            row_copy(w * window + t, 1).start()
            return c
        lax.fori_loop(0, window, issue, 0)

    def drain_window():
        rows = pl.ds(0, window * ROW_SUB)
        for _ in range(TOP_K):
            pltpu.make_async_copy(hp_hbm.at[rows, :], xs_out.at[rows, :], sem).wait()

    issue_window(0)

    def body(w, c):
        issue_window(w)
        drain_window()
        return c

    lax.fori_loop(1, n_tok // window, body, 0)
    drain_window()


def dispatch(dest, hp, n_rows, *, window):
    s = hp.shape[0] // ROW_SUB
    xs0 = jnp.zeros((n_rows * ROW_SUB, 128), U32)
    return pl.pallas_call(
        functools.partial(_dispatch_kernel, n_tok=s, window=window),
        grid_spec=pltpu.PrefetchScalarGridSpec(
            num_scalar_prefetch=1,
            grid=(1,),
            in_specs=[pl.BlockSpec(memory_space=pl.ANY), pl.BlockSpec(memory_space=pl.ANY)],
            out_specs=pl.BlockSpec(memory_space=pl.ANY),
            scratch_shapes=[pltpu.SemaphoreType.DMA(())],
        ),
        out_shape=jax.ShapeDtypeStruct(xs0.shape, U32),
        input_output_aliases={2: 0},
        compiler_params=_params(("arbitrary",)),
        name="moe_dispatch",
    )(dest, hp, xs0)


def _moe_ffn_kernel(te_ref, nu_ref, xs_ref, wg_ref, wu_ref, wd_ref, ys_ref, xb, acc, *, tm):
    del te_ref
    f = pl.program_id(1)
    used = pl.program_id(0) < nu_ref[0]
    last = f == pl.num_programs(1) - 1

    @pl.when(jnp.logical_and(used, f == 0))
    def _():
        for c, v in enumerate(_unpack_rows(xs_ref, tm)):
            xb[:, c * 128:(c + 1) * 128] = v.astype(BF16)
        acc[...] = jnp.zeros(acc.shape, F32)

    @pl.when(used)
    def _():
        acc[...] += _swiglu_step(xb[...], wg_ref[0], wu_ref[0], wd_ref[0])

    @pl.when(jnp.logical_and(used, last))
    def _():
        _pack_rows(acc[...], ys_ref)

    @pl.when(jnp.logical_and(jnp.logical_not(used), last))
    def _():
        ys_ref[...] = jnp.zeros(ys_ref.shape, U32)


def moe_ffn(tile_expert, n_used, xs, wg, wu, wd, *, tm, tf):
    nt = tile_expert.shape[0]
    _, d, fdim = wg.shape
    nf = fdim // tf

    def f_idx(j, f, nu):
        return jnp.where(j < nu[0], f, nf - 1)

    return pl.pallas_call(
        functools.partial(_moe_ffn_kernel, tm=tm),
        grid_spec=pltpu.PrefetchScalarGridSpec(
            num_scalar_prefetch=2,
            grid=(nt, nf),
            in_specs=[
                pl.BlockSpec((tm * ROW_SUB, 128), lambda j, f, te, nu: (jnp.minimum(j, nu[0] - 1), 0)),
                pl.BlockSpec((1, d, tf), lambda j, f, te, nu: (te[j], 0, f_idx(j, f, nu))),
                pl.BlockSpec((1, d, tf), lambda j, f, te, nu: (te[j], 0, f_idx(j, f, nu))),
                pl.BlockSpec((1, tf, d), lambda j, f, te, nu: (te[j], f_idx(j, f, nu), 0)),
            ],
            out_specs=pl.BlockSpec((tm * ROW_SUB, 128), lambda j, f, te, nu: (j, 0)),
            scratch_shapes=[pltpu.VMEM((tm, d), BF16), pltpu.VMEM((tm, d), F32)],
        ),
        out_shape=jax.ShapeDtypeStruct(xs.shape, U32),
        compiler_params=_params(("parallel", "arbitrary")),
        name="moe_ffn",
    )(tile_expert, n_used, xs, wg, wu, wd)


def _combine_kernel(dest_ref, x_ref, meta_ref, ys_hbm, o_ref, buf, sem, *, tm):
    i = pl.program_id(0)

    def issue_step(step, slot):
        def issue(t, c):
            for k in range(TOP_K):
                pltpu.make_async_copy(_row_tile(ys_hbm, dest_ref[2 * (step * tm + t) + k]),
                                      _row_tile(buf.at[slot, k], t), sem.at[slot]).start()
            return c
        lax.fori_loop(0, tm, issue, 0)

    @pl.when(i == 0)
    def _():
        issue_step(0, 0)

    @pl.when(i + 1 < pl.num_programs(0))
    def _():
        issue_step(i + 1, (i + 1) % 2)

    slot = i % 2
    for k in range(TOP_K):
        pltpu.make_async_copy(ys_hbm.at[pl.ds(0, tm * ROW_SUB), :], buf.at[slot, k], sem.at[slot]).wait()
    meta = meta_ref[...]
    g1 = meta[:, META_G1:META_G1 + 1]
    g2 = meta[:, META_G2:META_G2 + 1]
    y1 = _unpack_rows(buf.at[slot, 0], tm)
    y2 = _unpack_rows(buf.at[slot, 1], tm)
    for c in range(len(y1)):
        sl = slice(c * 128, (c + 1) * 128)
        o_ref[:, sl] = x_ref[:, sl] + g1 * y1[c] + g2 * y2[c]


def combine(dest, x, meta, ys, *, tm):
    s, d = x.shape
    return pl.pallas_call(
        functools.partial(_combine_kernel, tm=tm),
        grid_spec=pltpu.PrefetchScalarGridSpec(
            num_scalar_prefetch=1,
            grid=(s // tm,),
            in_specs=[
                pl.BlockSpec((tm, d), lambda i, dest: (i, 0)),
                pl.BlockSpec((tm, 128), lambda i, dest: (i, 0)),
                pl.BlockSpec(memory_space=pl.ANY),
            ],
            out_specs=pl.BlockSpec((tm, d), lambda i, dest: (i, 0)),
            scratch_shapes=[pltpu.VMEM((2, TOP_K, tm * ROW_SUB, 128), U32), pltpu.SemaphoreType.DMA((2,))],
        ),
        out_shape=jax.ShapeDtypeStruct((s, d), F32),
        compiler_params=_params(("arbitrary",)),
        name="moe_combine",
    )(dest, x, meta, ys)


def moe(x, g, w_router, wg, wu, wd, *, tm_rows, tf):
    s = x.shape[0]
    hp, meta, cnt = router(x, g, w_router, tm=512)
    i1, i2 = meta[:, META_I1].astype(jnp.int32), meta[:, META_I2].astype(jnp.int32)
    r1, r2 = meta[:, META_R1].astype(jnp.int32), meta[:, META_R2].astype(jnp.int32)
    counts = cnt[0, :N_EXPERTS].astype(jnp.int32)
    padded = (counts + tm_rows - 1) // tm_rows * tm_rows
    ends = jnp.cumsum(padded)
    offs = ends - padded
    eids = jnp.arange(N_EXPERTS, dtype=jnp.int32)
    off1 = jnp.sum(jnp.where(i1[:, None] == eids, offs, 0), axis=1)
    off2 = jnp.sum(jnp.where(i2[:, None] == eids, offs, 0), axis=1)
    dest = jnp.stack([off1 + r1, off2 + r2], axis=1).reshape(-1)
    n_tiles = (TOP_K * s) // tm_rows + N_EXPERTS
    n_used = (ends[-1] // tm_rows).reshape(1)
    tile_start = jnp.minimum(jnp.arange(n_tiles, dtype=jnp.int32), n_used - 1) * tm_rows
    tile_expert = jnp.sum(tile_start[:, None] >= ends[None, :], axis=1).astype(jnp.int32)
    xs = dispatch(dest, hp, n_tiles * tm_rows, window=512)
    ys = moe_ffn(tile_expert, n_used, xs, wg, wu, wd, tm=tm_rows, tf=tf)
    return combine(dest, x, meta, ys, tm=256)


def _pad_cols(w, n):
    return jnp.pad(w, ((0, 0), (0, n - w.shape[1])))


def _mla_weights(w_q_b, w_kv_b):
    wq = w_q_b.reshape(Q_LORA, N_HEADS_B, QK_DIM_B)
    wq = jnp.pad(wq, ((0, 0), (0, 0), (0, QK_PAD_B - QK_DIM_B))).reshape(Q_LORA, N_HEADS_B * QK_PAD_B)
    wkv = w_kv_b.reshape(KV_LORA, N_HEADS_B, NOPE_DIM + V_DIM)
    wk = wkv[:, :, :NOPE_DIM].reshape(KV_LORA, N_HEADS_B * NOPE_DIM)
    wv = wkv[:, :, NOPE_DIM:].reshape(KV_LORA, N_HEADS_B * V_DIM)
    return wq.astype(BF16), wk.astype(BF16), wv.astype(BF16)


def _rope_tables(positions):
    half = ROPE_DIM // 2
    inv = ROPE_THETA ** (-jnp.arange(half, dtype=F32) / half)
    ang = positions.astype(F32)[:, None] * inv
    cos, sin = jnp.cos(ang), jnp.sin(ang)
    z = jnp.zeros_like(cos)
    c = jnp.concatenate([cos, cos, z, z], axis=-1)
    s1 = jnp.concatenate([-sin, z, z, z], axis=-1)
    s2 = jnp.concatenate([z, sin, z, z], axis=-1)
    return c, s1, s2


def kernel(x, positions, rel_bias_table, norm_mix_g, w_in, q_a_norm_g, kv_a_norm_g, w_q_b, w_kv_b, q_norm_a_g, k_norm_a_g, q_norm_b_g, k_norm_b_g, w_out, norm_ffn_g, w_ff_gate, w_ff_up, w_ff_down, w_router, w_exp_gate, w_exp_up, w_exp_down):
    batch, seq, d = x.shape
    depth = w_in.shape[0]
    outs = []
    bias = _dilated_bias(rel_bias_table)
    for bi in range(batch):
        xs = x[bi]
        rope_c, rope_s1, rope_s2 = _rope_tables(positions[bi])
        for l in range(depth):
            w_in_l = _pad_cols(w_in[l], PROJ_COLS).astype(BF16)
            proj = norm_matmul(xs, norm_mix_g[l][None], w_in_l, tm=1024, tn=1024, out_dtype=F32)
            a = dilated_mixer(proj, bias, q_norm_a_g[l][None], k_norm_a_g[l][None])
            wq, wk, wv = _mla_weights(w_q_b[l], w_kv_b[l])
            qg = _pad_cols(q_norm_b_g[l][None] * (QK_DIM_B ** -0.5 * LOG2E), QK_PAD_B)
            kg = _pad_cols(k_norm_b_g[l][None], QK_PAD_B)
            qb, kb, vb = mla_prep(proj, q_a_norm_g[l][None], kv_a_norm_g[l][None], wq, wk, wv,
                                  qg, kg, rope_c, rope_s1, rope_s2, tm=512)
            b = mla_flash(qb, kb, vb, tq=2048, tk=512)
            w_out_l = w_out[l].astype(BF16)
            xs = out_proj(xs, a, b, w_out_l[:WIDTH_A], w_out_l[WIDTH_A:], tm=1024, tn=1024)
            gf = norm_ffn_g[l][None]
            if l % 2 == 0:
                i = l // 2
                xs = ffn(xs, gf, w_ff_gate[i].astype(BF16), w_ff_up[i].astype(BF16),
                         w_ff_down[i].astype(BF16), tm=1024, tf=512)
            else:
                i = l // 2
                xs = moe(xs, gf, _pad_cols(w_router[i], 128), w_exp_gate[i].astype(BF16),
                         w_exp_up[i].astype(BF16), w_exp_down[i].astype(BF16), tm_rows=512, tf=512)
        outs.append(xs)
    return jnp.stack(outs, axis=0)
```

```python
import functools
import math

import jax
import jax.numpy as jnp
from jax import lax
from jax.experimental import pallas as pl
from jax.experimental.pallas import tpu as pltpu

F32 = jnp.float32
BF16 = jnp.bfloat16

D_MODEL = 2048
HEAD_DIM = 128
N_HEADS_A = 8
DILATED_PATTERNS = ((128, 1), (512, 4), (2048, 16))
BLK = 128
N_BUCKETS = 32
MAX_DISTANCE = 2048
N_HEADS_B = 8
Q_LORA = 512
KV_LORA = 256
NOPE_DIM = 128
ROPE_DIM = 64
V_DIM = 128
QK_DIM_B = NOPE_DIM + ROPE_DIM
QK_PAD_B = 256
VT_ROWS = V_DIM + 16
LOG2E = math.log2(math.e)
ROPE_THETA = 10000.0
WIDTH_A = N_HEADS_A * HEAD_DIM
WIDTH_B = N_HEADS_B * V_DIM
IN_COLS = 3 * WIDTH_A + Q_LORA + KV_LORA + ROPE_DIM
PROJ_COLS = 4096
D_FF = 5632
N_EXPERTS = 8
TOP_K = 2
ROW_SUB = 8
HALF = D_MODEL // 2
U32 = jnp.uint32
EPS = 1e-6
NEG = -1e30

SUPER = 2048
DILATED_UNROLL = 16
VMEM_LIMIT = 56 * 1024 * 1024


def _params(sem, vmem=VMEM_LIMIT):
    return pltpu.CompilerParams(dimension_semantics=sem, vmem_limit_bytes=vmem)


def _rms(x, g, n=None):
    ss = jnp.sum(x * x, axis=-1, keepdims=True)
    n = x.shape[-1] if n is None else n
    return x * lax.rsqrt(ss * (1.0 / n) + EPS) * g


def _norm_matmul_kernel(x_ref, g_ref, w_ref, o_ref, hn_ref):
    @pl.when(pl.program_id(1) == 0)
    def _():
        hn_ref[...] = _rms(x_ref[...], g_ref[...]).astype(BF16)

    o_ref[...] = jnp.dot(hn_ref[...], w_ref[...], preferred_element_type=F32).astype(o_ref.dtype)


def norm_matmul(x, g, w, *, tm, tn, out_dtype):
    s, k = x.shape
    n = w.shape[1]
    return pl.pallas_call(
        _norm_matmul_kernel,
        grid=(s // tm, n // tn),
        in_specs=[
            pl.BlockSpec((tm, k), lambda i, j: (i, 0)),
            pl.BlockSpec((1, k), lambda i, j: (0, 0)),
            pl.BlockSpec((k, tn), lambda i, j: (0, j)),
        ],
        out_specs=pl.BlockSpec((tm, tn), lambda i, j: (i, j)),
        out_shape=jax.ShapeDtypeStruct((s, n), out_dtype),
        scratch_shapes=[pltpu.VMEM((tm, k), BF16)],
        compiler_params=_params(("parallel", "arbitrary")),
        name="norm_matmul",
    )(x, g, w)


def _t5_bucket(n):
    max_exact = N_BUCKETS // 2
    nf = jnp.maximum(n, 1).astype(F32)
    large = max_exact + (jnp.log(nf / max_exact) / math.log(MAX_DISTANCE / max_exact)
                         * (N_BUCKETS - max_exact)).astype(jnp.int32)
    large = jnp.minimum(large, N_BUCKETS - 1)
    return jnp.where(n < max_exact, n, large)


def _dilated_bias(rel_bias):
    i = jnp.arange(BLK)[:, None]
    j = jnp.arange(2 * BLK)[None, :]
    delta = i + BLK - j
    out = []
    for window, dilation in DILATED_PATTERNS:
        band = (delta >= 0) & (delta <= window // dilation)
        bucket = _t5_bucket(jnp.maximum(delta, 0) * dilation)
        onehot = (bucket[None] == jnp.arange(N_BUCKETS)[:, None, None]).astype(F32)
        bias = jnp.einsum('nij,nh->hij', onehot, rel_bias.astype(F32), precision=lax.Precision.HIGHEST)
        out.append(jnp.where(band[None], bias * LOG2E, NEG))
    return jnp.stack(out, axis=0)


def _dilated_kernel(q_ref, kp_ref, kc_ref, vp_ref, vc_ref, bias_ref, qg_ref, kg_ref, o_ref,
                    qn_s, kn_s, v_s, op_s, lse_s):
    sb = pl.program_id(1)
    qn_s[...] = _rms(q_ref[...], qg_ref[...]) * (HEAD_DIM ** -0.5 * LOG2E)
    kn_s[0:SUPER, :] = _rms(kp_ref[...], kg_ref[...])
    kn_s[SUPER:2 * SUPER, :] = _rms(kc_ref[...], kg_ref[...])
    v_s[0:SUPER, :] = vp_ref[...]
    v_s[SUPER:2 * SUPER, :] = vc_ref[...]
    col = lax.broadcasted_iota(jnp.int32, (BLK, 2 * BLK), 1)

    for pi, (_, d) in enumerate(DILATED_PATTERNS):
        def rows(start, size, d=d):
            return pl.ds(start, size) if d == 1 else pl.ds(start, size, stride=d)

        def block(idx, pi=pi, d=d, rows=rows):
            r = idx % d
            b = idx // d
            q0 = r + d * BLK * b
            k0 = SUPER + q0 - d * BLK
            qb = qn_s[rows(q0, BLK), :].astype(BF16)
            k2 = kn_s[rows(k0, 2 * BLK), :].astype(BF16)
            v2 = v_s[rows(k0, 2 * BLK), :].astype(BF16)
            s = lax.dot_general(qb, k2, (((1,), (1,)), ((), ())), preferred_element_type=F32)
            s = s + bias_ref[pi, 0]
            first = jnp.logical_and(sb == 0, b == 0)
            s = jnp.where(jnp.logical_and(first, col < BLK), NEG, s)
            m = jnp.max(s, axis=-1, keepdims=True)
            p = jnp.exp2(s - m)
            den = jnp.sum(p, axis=-1, keepdims=True)
            o = jnp.dot(p.astype(BF16), v2, preferred_element_type=F32) / den
            lse = m + jnp.log2(den)
            op_s[pi, rows(q0, BLK), :] = o
            lse_s[pi, rows(q0, BLK), :] = jnp.broadcast_to(lse, (BLK, HEAD_DIM))

        def body(it, carry, block=block):
            for u in range(DILATED_UNROLL):
                block(it * DILATED_UNROLL + u)
            return carry

        lax.fori_loop(0, SUPER // BLK // DILATED_UNROLL, body, 0)

    chunk = 256
    for c in range(SUPER // chunk):
        sl = slice(c * chunk, (c + 1) * chunk)
        l0, l1, l2 = lse_s[0, sl, :], lse_s[1, sl, :], lse_s[2, sl, :]
        mx = jnp.maximum(jnp.maximum(l0, l1), l2)
        e0, e1, e2 = jnp.exp2(l0 - mx), jnp.exp2(l1 - mx), jnp.exp2(l2 - mx)
        num = e0 * op_s[0, sl, :] + e1 * op_s[1, sl, :] + e2 * op_s[2, sl, :]
        o_ref[sl, :] = (num / (e0 + e1 + e2)).astype(o_ref.dtype)


def dilated_mixer(proj, bias, q_g, k_g):
    s = proj.shape[0]
    nsb = s // SUPER
    h8 = N_HEADS_A
    blk = (SUPER, HEAD_DIM)
    return pl.pallas_call(
        _dilated_kernel,
        grid=(h8, nsb),
        in_specs=[
            pl.BlockSpec(blk, lambda h, i: (i, h)),
            pl.BlockSpec(blk, lambda h, i: (jnp.maximum(i - 1, 0), h8 + h)),
            pl.BlockSpec(blk, lambda h, i: (i, h8 + h)),
            pl.BlockSpec(blk, lambda h, i: (jnp.maximum(i - 1, 0), 2 * h8 + h)),
            pl.BlockSpec(blk, lambda h, i: (i, 2 * h8 + h)),
            pl.BlockSpec((len(DILATED_PATTERNS), 1, BLK, 2 * BLK), lambda h, i: (0, h, 0, 0)),
            pl.BlockSpec((1, HEAD_DIM), lambda h, i: (0, 0)),
            pl.BlockSpec((1, HEAD_DIM), lambda h, i: (0, 0)),
        ],
        out_specs=pl.BlockSpec(blk, lambda h, i: (i, h)),
        out_shape=jax.ShapeDtypeStruct((s, WIDTH_A), BF16),
        scratch_shapes=[
            pltpu.VMEM((SUPER, HEAD_DIM), F32),
            pltpu.VMEM((2 * SUPER, HEAD_DIM), F32),
            pltpu.VMEM((2 * SUPER, HEAD_DIM), F32),
            pltpu.VMEM((len(DILATED_PATTERNS), SUPER, HEAD_DIM), F32),
            pltpu.VMEM((len(DILATED_PATTERNS), SUPER, HEAD_DIM), F32),
        ],
        compiler_params=_params(("parallel", "arbitrary")),
        name="dilated_mixer",
    )(proj, proj, proj, proj, proj, bias, q_g, k_g)


def _mla_prep_kernel(cq_ref, ckv_ref, kr_ref, qag_ref, kvag_ref, wq_ref, wk_ref, wv_ref,
                     qg_ref, kg_ref, c_ref, s1_ref, s2_ref, q_out, k_out, v_out):
    c, s1, s2 = c_ref[...], s1_ref[...], s2_ref[...]

    def rope(x):
        return x * c + pltpu.roll(x, 96, 1) * s1 + pltpu.roll(x, 32, 1) * s2

    cqn = _rms(cq_ref[...], qag_ref[...]).astype(BF16)
    ckvn = _rms(ckv_ref[...], kvag_ref[...]).astype(BF16)
    qpre = jnp.dot(cqn, wq_ref[...], preferred_element_type=F32)
    knope = jnp.dot(ckvn, wk_ref[...], preferred_element_type=F32)
    v = jnp.dot(ckvn, wv_ref[...], preferred_element_type=F32)
    kr = kr_ref[...]
    kr_ss = jnp.sum(kr * kr, axis=-1, keepdims=True)
    qg, kg = qg_ref[...], kg_ref[...]
    kr_roped = rope(kr * kg[:, NOPE_DIM:])
    inv_n = 1.0 / QK_DIM_B
    ones = jnp.ones((VT_ROWS - V_DIM, cq_ref.shape[0]), BF16)
    for h in range(N_HEADS_B):
        qh = qpre[:, h * QK_PAD_B:(h + 1) * QK_PAD_B]
        rs = lax.rsqrt(jnp.sum(qh * qh, axis=-1, keepdims=True) * inv_n + EPS)
        q_out[h, 0:NOPE_DIM, :] = (qh[:, :NOPE_DIM] * rs * qg[:, :NOPE_DIM]).T.astype(BF16)
        q_out[h, NOPE_DIM:QK_PAD_B, :] = rope(qh[:, NOPE_DIM:] * rs * qg[:, NOPE_DIM:]).T.astype(BF16)
        kh = knope[:, h * NOPE_DIM:(h + 1) * NOPE_DIM]
        rs = lax.rsqrt((jnp.sum(kh * kh, axis=-1, keepdims=True) + kr_ss) * inv_n + EPS)
        k_out[h, :, 0:NOPE_DIM] = (kh * rs * kg[:, :NOPE_DIM]).astype(BF16)
        k_out[h, :, NOPE_DIM:QK_PAD_B] = (kr_roped * rs).astype(BF16)
        v_out[h, 0:V_DIM, :] = v[:, h * V_DIM:(h + 1) * V_DIM].T.astype(BF16)
        v_out[h, V_DIM:VT_ROWS, :] = ones


def mla_prep(proj, q_a_g, kv_a_g, wq, wk, wv, qg, kg, rope_c, rope_s1, rope_s2, *, tm):
    s = proj.shape[0]
    hb = N_HEADS_B
    full = lambda shape: pl.BlockSpec(shape, lambda i: (0,) * len(shape))
    cq_blk = (3 * WIDTH_A) // Q_LORA
    ckv_blk = (3 * WIDTH_A + Q_LORA) // KV_LORA
    kr_blk = (3 * WIDTH_A + Q_LORA + KV_LORA) // 128
    return pl.pallas_call(
        _mla_prep_kernel,
        grid=(s // tm,),
        in_specs=[
            pl.BlockSpec((tm, Q_LORA), lambda i: (i, cq_blk)),
            pl.BlockSpec((tm, KV_LORA), lambda i: (i, ckv_blk)),
            pl.BlockSpec((tm, 128), lambda i: (i, kr_blk)),
            full((1, Q_LORA)), full((1, KV_LORA)),
            full(wq.shape), full(wk.shape), full(wv.shape),
            full((1, QK_PAD_B)), full((1, QK_PAD_B)),
            pl.BlockSpec((tm, 128), lambda i: (i, 0)),
            pl.BlockSpec((tm, 128), lambda i: (i, 0)),
            pl.BlockSpec((tm, 128), lambda i: (i, 0)),
        ],
        out_specs=[
            pl.BlockSpec((hb, QK_PAD_B, tm), lambda i: (0, 0, i)),
            pl.BlockSpec((hb, tm, QK_PAD_B), lambda i: (0, i, 0)),
            pl.BlockSpec((hb, VT_ROWS, tm), lambda i: (0, 0, i)),
        ],
        out_shape=[
            jax.ShapeDtypeStruct((hb, QK_PAD_B, s), BF16),
            jax.ShapeDtypeStruct((hb, s, QK_PAD_B), BF16),
            jax.ShapeDtypeStruct((hb, VT_ROWS, s), BF16),
        ],
        compiler_params=_params(("parallel",)),
        name="mla_prep",
    )(proj, proj, proj, q_a_g, kv_a_g, wq, wk, wv, qg, kg, rope_c, rope_s1, rope_s2)


def _flash_kernel(qt_ref, k_ref, vt_ref, o_ref, s_a, s_b, m_s, acc_s, *, tq, tk):
    i = pl.program_id(1)
    m_s[...] = jnp.full(m_s.shape, NEG, F32)
    acc_s[...] = jnp.zeros(acc_s.shape, F32)
    per_tile = tq // tk
    assert per_tile % 2 == 0

    def compute(c, dst, q_lo=0):
        start = pl.multiple_of(c * tk, tk)
        dst[:, q_lo:] = jnp.dot(k_ref[0, pl.ds(start, tk), :], qt_ref[0, :, q_lo:],
                                preferred_element_type=F32)

    def process(c, src, q_lo=0, masked=False):
        s = src[:, q_lo:]
        if masked:
            key = c * tk + lax.broadcasted_iota(jnp.int32, s.shape, 0)
            qry = i * tq + q_lo + lax.broadcasted_iota(jnp.int32, s.shape, 1)
            s = jnp.where(key <= qry, s, NEG)
        m_prev = m_s[:, q_lo:]
        m_new = jnp.maximum(m_prev, jnp.max(s, axis=0, keepdims=True))
        alpha = jnp.exp2(m_prev - m_new)
        p = jnp.exp2(s - m_new).astype(BF16)
        start = pl.multiple_of(c * tk, tk)
        pv = jnp.dot(vt_ref[0, :, pl.ds(start, tk)], p, preferred_element_type=F32)
        acc_s[:, q_lo:] = alpha * acc_s[:, q_lo:] + pv
        m_s[:, q_lo:] = m_new

    bufs = (s_a, s_b)
    compute(0, s_a)
    n_full = per_tile * i

    def trip(t, carry):
        for u in range(per_tile):
            compute(t * per_tile + u + 1, bufs[(u + 1) % 2])
            process(t * per_tile + u, bufs[u % 2])
        return carry

    lax.fori_loop(0, i, trip, 0)
    for d in range(per_tile):
        if d + 1 < per_tile:
            compute(n_full + d + 1, bufs[(d + 1) % 2], (d + 1) * tk)
        process(n_full + d, bufs[d % 2], d * tk, masked=True)
    acc = acc_s[...]
    o_t = acc[0:V_DIM, :] / acc[V_DIM:V_DIM + 1, :]
    o_ref[...] = o_t.T.astype(o_ref.dtype)


def mla_flash(qt, k, vt, *, tq, tk):
    hb, s, _ = k.shape
    return pl.pallas_call(
        functools.partial(_flash_kernel, tq=tq, tk=tk),
        grid=(hb, s // tq),
        in_specs=[
            pl.BlockSpec((1, QK_PAD_B, tq), lambda h, i: (h, 0, i)),
            pl.BlockSpec((1, s, QK_PAD_B), lambda h, i: (h, 0, 0)),
            pl.BlockSpec((1, VT_ROWS, s), lambda h, i: (h, 0, 0)),
        ],
        out_specs=pl.BlockSpec((tq, V_DIM), lambda h, i: (i, h)),
        out_shape=jax.ShapeDtypeStruct((s, WIDTH_B), BF16),
        scratch_shapes=[
            pltpu.VMEM((tk, tq), F32),
            pltpu.VMEM((tk, tq), F32),
            pltpu.VMEM((1, tq), F32),
            pltpu.VMEM((VT_ROWS, tq), F32),
        ],
        compiler_params=_params(("parallel", "arbitrary")),
        name="mla_flash",
    )(qt, k, vt)


def _out_proj_kernel(x_ref, a_ref, b_ref, wa_ref, wb_ref, o_ref):
    acc = jnp.dot(a_ref[...], wa_ref[...], preferred_element_type=F32)
    acc = acc + jnp.dot(b_ref[...], wb_ref[...], preferred_element_type=F32)
    o_ref[...] = x_ref[...] + acc


def out_proj(x, a, b, wa, wb, *, tm, tn):
    s, n = x.shape
    ka, kb = a.shape[1], b.shape[1]
    return pl.pallas_call(
        _out_proj_kernel,
        grid=(s // tm, n // tn),
        in_specs=[
            pl.BlockSpec((tm, tn), lambda i, j: (i, j)),
            pl.BlockSpec((tm, ka), lambda i, j: (i, 0)),
            pl.BlockSpec((tm, kb), lambda i, j: (i, 0)),
            pl.BlockSpec((ka, tn), lambda i, j: (0, j)),
            pl.BlockSpec((kb, tn), lambda i, j: (0, j)),
        ],
        out_specs=pl.BlockSpec((tm, tn), lambda i, j: (i, j)),
        out_shape=jax.ShapeDtypeStruct((s, n), F32),
        compiler_params=_params(("parallel", "arbitrary")),
        name="out_proj",
    )(x, a, b, wa, wb)


def _swiglu_step(h, wg, wu, wd):
    gate = jnp.dot(h, wg, preferred_element_type=F32)
    up = jnp.dot(h, wu, preferred_element_type=F32)
    act = gate * jax.nn.sigmoid(gate) * up
    return jnp.dot(act.astype(BF16), wd, preferred_element_type=F32)


def _ffn_kernel(x_ref, g_ref, wg_ref, wu_ref, wd_ref, o_ref, hn_ref):
    @pl.when(pl.program_id(1) == 0)
    def _():
        hn_ref[...] = _rms(x_ref[...], g_ref[...]).astype(BF16)
        o_ref[...] = x_ref[...]

    o_ref[...] += _swiglu_step(hn_ref[...], wg_ref[...], wu_ref[...], wd_ref[...])


def ffn(x, g, wg, wu, wd, *, tm, tf):
    s, d = x.shape
    fdim = wg.shape[1]
    return pl.pallas_call(
        _ffn_kernel,
        grid=(s // tm, fdim // tf),
        in_specs=[
            pl.BlockSpec((tm, d), lambda i, f: (i, 0)),
            pl.BlockSpec((1, d), lambda i, f: (0, 0)),
            pl.BlockSpec((d, tf), lambda i, f: (0, f)),
            pl.BlockSpec((d, tf), lambda i, f: (0, f)),
            pl.BlockSpec((tf, d), lambda i, f: (f, 0)),
        ],
        out_specs=pl.BlockSpec((tm, d), lambda i, f: (i, 0)),
        out_shape=jax.ShapeDtypeStruct((s, d), F32),
        scratch_shapes=[pltpu.VMEM((tm, d), BF16)],
        compiler_params=_params(("parallel", "arbitrary")),
        name="ffn",
    )(x, g, wg, wu, wd)


def _pack_rows(y, dst):
    n = y.shape[0]
    for s in range(ROW_SUB):
        lo = y[:, s * 128:(s + 1) * 128].astype(BF16).astype(F32)
        hi = y[:, HALF + s * 128:HALF + (s + 1) * 128].astype(BF16).astype(F32)
        w = (lax.bitcast_convert_type(lo, U32) >> 16) | lax.bitcast_convert_type(hi, U32)
        dst[pl.ds(s, n, stride=ROW_SUB), :] = w


def _unpack_rows(src, n):
    lo, hi = [], []
    for s in range(ROW_SUB):
        w = src[pl.ds(s, n, stride=ROW_SUB), :]
        lo.append(lax.bitcast_convert_type(w << 16, F32))
        hi.append(lax.bitcast_convert_type(w & jnp.uint32(0xFFFF0000), F32))
    return lo + hi


META_I1, META_I2, META_R1, META_R2, META_G1, META_G2 = range(6)


def _router_kernel(x_ref, g_ref, wr_ref, hp_ref, meta_ref, cnt_ref, carry):
    @pl.when(pl.program_id(0) == 0)
    def _():
        carry[...] = jnp.zeros(carry.shape, F32)

    hn = _rms(x_ref[...], g_ref[...])
    _pack_rows(hn, hp_ref)
    h_hi = hn.astype(BF16)
    h_lo = (hn - h_hi.astype(F32)).astype(BF16)
    w_hi, w_lo = wr_ref[0], wr_ref[1]
    logits = (jnp.dot(h_hi, w_hi, preferred_element_type=F32)
              + jnp.dot(h_hi, w_lo, preferred_element_type=F32)
              + jnp.dot(h_lo, w_hi, preferred_element_type=F32))
    lane = lax.broadcasted_iota(jnp.int32, logits.shape, 1)
    logits = jnp.where(lane < N_EXPERTS, logits, NEG)
    v1 = jnp.max(logits, axis=-1, keepdims=True)
    i1 = jnp.min(jnp.where(logits == v1, lane, 128), axis=-1, keepdims=True)
    rest = jnp.where(lane == i1, NEG, logits)
    v2 = jnp.max(rest, axis=-1, keepdims=True)
    i2 = jnp.min(jnp.where(rest == v2, lane, 128), axis=-1, keepdims=True)
    e2 = jnp.exp(v2 - v1)
    g1 = 1.0 / (1.0 + e2)
    g2 = e2 / (1.0 + e2)

    tm = hn.shape[0]
    member = jnp.where(jnp.logical_or(lane == i1, lane == i2), 1.0, 0.0)
    row = lax.broadcasted_iota(jnp.int32, (tm, tm), 0)
    col = lax.broadcasted_iota(jnp.int32, (tm, tm), 1)
    earlier = jnp.where(col < row, 1.0, 0.0).astype(BF16)
    rank = carry[...] + jnp.dot(earlier, member.astype(BF16), preferred_element_type=F32)
    r1 = jnp.sum(jnp.where(lane == i1, rank, 0.0), axis=-1, keepdims=True)
    r2 = jnp.sum(jnp.where(lane == i2, rank, 0.0), axis=-1, keepdims=True)
    carry[...] += jnp.sum(member, axis=0, keepdims=True)
    cnt_ref[...] = carry[...]
    meta = jnp.zeros(logits.shape, F32)
    for pos, val in ((META_I1, i1.astype(F32)), (META_I2, i2.astype(F32)), (META_R1, r1),
                     (META_R2, r2), (META_G1, g1), (META_G2, g2)):
        meta = jnp.where(lane == pos, val, meta)
    meta_ref[...] = meta


def router(x, g, wr, *, tm):
    s, d = x.shape
    return pl.pallas_call(
        _router_kernel,
        grid=(s // tm,),
        in_specs=[
            pl.BlockSpec((tm, d), lambda i: (i, 0)),
            pl.BlockSpec((1, d), lambda i: (0, 0)),
            pl.BlockSpec((2, d, 128), lambda i: (0, 0, 0)),
        ],
        out_specs=[
            pl.BlockSpec((tm * ROW_SUB, 128), lambda i: (i, 0)),
            pl.BlockSpec((tm, 128), lambda i: (i, 0)),
            pl.BlockSpec((1, 128), lambda i: (0, 0)),
        ],
        out_shape=[
            jax.ShapeDtypeStruct((s * ROW_SUB, 128), U32),
            jax.ShapeDtypeStruct((s, 128), F32),
            jax.ShapeDtypeStruct((1, 128), F32),
        ],
        scratch_shapes=[pltpu.VMEM((1, 128), F32)],
        compiler_params=_params(("arbitrary",)),
        name="router",
    )(x, g, wr)


def _row_tile(ref, r):
    return ref.at[pl.ds(pl.multiple_of(r * ROW_SUB, ROW_SUB), ROW_SUB), :]


def _dispatch_kernel(dest_ref, hp_hbm, xs_in, xs_out, stage, in_sem, out_sem, *, tm):
    del xs_in
    i = pl.program_id(0)
    last = pl.num_programs(0) - 1
    block_rows = tm * ROW_SUB

    def fetch(step):
        src = hp_hbm.at[pl.ds(pl.multiple_of(step * block_rows, block_rows), block_rows), :]
        return pltpu.make_async_copy(src, stage.at[step % 3], in_sem.at[step % 3])

    def drain(step):
        for _ in range(TOP_K):
            pltpu.make_async_copy(stage.at[step % 3], xs_out.at[pl.ds(0, block_rows), :],
                                  out_sem.at[step % 2]).wait()

    @pl.when(i == 0)
    def _():
        fetch(0).start()

    @pl.when(i < last)
    def _():
        fetch(i + 1).start()

    fetch(i).wait()
    src = stage.at[i % 3]

    def issue(t, c):
        for k in range(TOP_K):
            pltpu.make_async_copy(_row_tile(src, t), _row_tile(xs_out, dest_ref[2 * (i * tm + t) + k]),
                                  out_sem.at[i % 2]).start()
        return c

    lax.fori_loop(0, tm, issue, 0)

    @pl.when(i > 0)
    def _():
        drain(i - 1)

    @pl.when(i == last)
    def _():
        drain(i)


def dispatch(dest, hp, n_rows, *, tm):
    s = hp.shape[0] // ROW_SUB
    xs0 = jnp.zeros((n_rows * ROW_SUB, 128), U32)
    return pl.pallas_call(
        functools.partial(_dispatch_kernel, tm=tm),
        grid_spec=pltpu.PrefetchScalarGridSpec(
            num_scalar_prefetch=1,
            grid=(s // tm,),
            in_specs=[pl.BlockSpec(memory_space=pl.ANY), pl.BlockSpec(memory_space=pl.ANY)],
            out_specs=pl.BlockSpec(memory_space=pl.ANY),
            scratch_shapes=[pltpu.VMEM((3, tm * ROW_SUB, 128), U32),
                            pltpu.SemaphoreType.DMA((3,)), pltpu.SemaphoreType.DMA((2,))],
        ),
        out_shape=jax.ShapeDtypeStruct(xs0.shape, U32),
        input_output_aliases={2: 0},
        compiler_params=_params(("arbitrary",)),
        name="moe_dispatch",
    )(dest, hp, xs0)


def _moe_ffn_kernel(te_ref, nu_ref, xs_ref, wg_ref, wu_ref, wd_ref, ys_ref, xb, acc, *, tm):
    del te_ref
    f = pl.program_id(1)
    used = pl.program_id(0) < nu_ref[0]
    last = f == pl.num_programs(1) - 1

    @pl.when(jnp.logical_and(used, f == 0))
    def _():
        for c, v in enumerate(_unpack_rows(xs_ref, tm)):
            xb[:, c * 128:(c + 1) * 128] = v.astype(BF16)
        acc[...] = jnp.zeros(acc.shape, F32)

    @pl.when(used)
    def _():
        acc[...] += _swiglu_step(xb[...], wg_ref[0], wu_ref[0], wd_ref[0])

    @pl.when(jnp.logical_and(used, last))
    def _():
        _pack_rows(acc[...], ys_ref)

    @pl.when(jnp.logical_and(jnp.logical_not(used), last))
    def _():
        ys_ref[...] = jnp.zeros(ys_ref.shape, U32)


def moe_ffn(tile_expert, n_used, xs, wg, wu, wd, *, tm, tf):
    nt = tile_expert.shape[0]
    _, d, fdim = wg.shape
    nf = fdim // tf

    def f_idx(j, f, nu):
        return jnp.where(j < nu[0], f, nf - 1)

    return pl.pallas_call(
        functools.partial(_moe_ffn_kernel, tm=tm),
        grid_spec=pltpu.PrefetchScalarGridSpec(
            num_scalar_prefetch=2,
            grid=(nt, nf),
            in_specs=[
                pl.BlockSpec((tm * ROW_SUB, 128), lambda j, f, te, nu: (jnp.minimum(j, nu[0] - 1), 0)),
                pl.BlockSpec((1, d, tf), lambda j, f, te, nu: (te[j], 0, f_idx(j, f, nu))),
                pl.BlockSpec((1, d, tf), lambda j, f, te, nu: (te[j], 0, f_idx(j, f, nu))),
                pl.BlockSpec((1, tf, d), lambda j, f, te, nu: (te[j], f_idx(j, f, nu), 0)),
            ],
            out_specs=pl.BlockSpec((tm * ROW_SUB, 128), lambda j, f, te, nu: (j, 0)),
            scratch_shapes=[pltpu.VMEM((tm, d), BF16), pltpu.VMEM((tm, d), F32)],
        ),
        out_shape=jax.ShapeDtypeStruct(xs.shape, U32),
        compiler_params=_params(("parallel", "arbitrary")),
        name="moe_ffn",
    )(tile_expert, n_used, xs, wg, wu, wd)


def _combine_kernel(dest_ref, x_ref, meta_ref, ys_hbm, o_ref, buf, sem, *, tm):
    i = pl.program_id(0)

    def issue_step(step, slot):
        def issue(t, c):
            for k in range(TOP_K):
                pltpu.make_async_copy(_row_tile(ys_hbm, dest_ref[2 * (step * tm + t) + k]),
                                      _row_tile(buf.at[slot, k], t), sem.at[slot]).start()
            return c
        lax.fori_loop(0, tm, issue, 0)

    @pl.when(i == 0)
    def _():
        issue_step(0, 0)

    @pl.when(i + 1 < pl.num_programs(0))
    def _():
        issue_step(i + 1, (i + 1) % 2)

    slot = i % 2
    for k in range(TOP_K):
        pltpu.make_async_copy(ys_hbm.at[pl.ds(0, tm * ROW_SUB), :], buf.at[slot, k], sem.at[slot]).wait()
    meta = meta_ref[...]
    g1 = meta[:, META_G1:META_G1 + 1]
    g2 = meta[:, META_G2:META_G2 + 1]
    y1 = _unpack_rows(buf.at[slot, 0], tm)
    y2 = _unpack_rows(buf.at[slot, 1], tm)
    for c in range(len(y1)):
        sl = slice(c * 128, (c + 1) * 128)
        o_ref[:, sl] = x_ref[:, sl] + g1 * y1[c] + g2 * y2[c]


def combine(dest, x, meta, ys, *, tm):
    s, d = x.shape
    return pl.pallas_call(
        functools.partial(_combine_kernel, tm=tm),
        grid_spec=pltpu.PrefetchScalarGridSpec(
            num_scalar_prefetch=1,
            grid=(s // tm,),
            in_specs=[
                pl.BlockSpec((tm, d), lambda i, dest: (i, 0)),
                pl.BlockSpec((tm, 128), lambda i, dest: (i, 0)),
                pl.BlockSpec(memory_space=pl.ANY),
            ],
            out_specs=pl.BlockSpec((tm, d), lambda i, dest: (i, 0)),
            scratch_shapes=[pltpu.VMEM((2, TOP_K, tm * ROW_SUB, 128), U32), pltpu.SemaphoreType.DMA((2,))],
        ),
        out_shape=jax.ShapeDtypeStruct((s, d), F32),
        compiler_params=_params(("arbitrary",)),
        name="moe_combine",
    )(dest, x, meta, ys)


def moe(x, g, w_router, wg, wu, wd, *, tm_rows, tf):
    s = x.shape[0]
    wr = _pad_cols(w_router.astype(F32), 128)
    wr_hi = wr.astype(BF16)
    wr_lo = (wr - wr_hi.astype(F32)).astype(BF16)
    hp, meta, cnt = router(x, g, jnp.stack([wr_hi, wr_lo]), tm=512)
    i1, i2 = meta[:, META_I1].astype(jnp.int32), meta[:, META_I2].astype(jnp.int32)
    r1, r2 = meta[:, META_R1].astype(jnp.int32), meta[:, META_R2].astype(jnp.int32)
    counts = cnt[0, :N_EXPERTS].astype(jnp.int32)
    padded = (counts + tm_rows - 1) // tm_rows * tm_rows
    ends = jnp.cumsum(padded)
    offs = ends - padded
    eids = jnp.arange(N_EXPERTS, dtype=jnp.int32)
    off1 = jnp.sum(jnp.where(i1[:, None] == eids, offs, 0), axis=1)
    off2 = jnp.sum(jnp.where(i2[:, None] == eids, offs, 0), axis=1)
    dest = jnp.stack([off1 + r1, off2 + r2], axis=1).reshape(-1)
    n_tiles = (TOP_K * s) // tm_rows + N_EXPERTS
    n_used = (ends[-1] // tm_rows).reshape(1)
    tile_start = jnp.minimum(jnp.arange(n_tiles, dtype=jnp.int32), n_used - 1) * tm_rows
    tile_expert = jnp.sum(tile_start[:, None] >= ends[None, :], axis=1).astype(jnp.int32)
    xs = dispatch(dest, hp, n_tiles * tm_rows, tm=256)
    ys = moe_ffn(tile_expert, n_used, xs, wg, wu, wd, tm=tm_rows, tf=tf)
    return combine(dest, x, meta, ys, tm=256)


def _pad_cols(w, n):
    return jnp.pad(w, ((0, 0), (0, n - w.shape[1])))


def _mla_weights(w_q_b, w_kv_b):
    wq = w_q_b.reshape(Q_LORA, N_HEADS_B, QK_DIM_B)
    wq = jnp.pad(wq, ((0, 0), (0, 0), (0, QK_PAD_B - QK_DIM_B))).reshape(Q_LORA, N_HEADS_B * QK_PAD_B)
    wkv = w_kv_b.reshape(KV_LORA, N_HEADS_B, NOPE_DIM + V_DIM)
    wk = wkv[:, :, :NOPE_DIM].reshape(KV_LORA, N_HEADS_B * NOPE_DIM)
    wv = wkv[:, :, NOPE_DIM:].reshape(KV_LORA, N_HEADS_B * V_DIM)
    return wq.astype(BF16), wk.astype(BF16), wv.astype(BF16)


def _rope_tables(positions):
    half = ROPE_DIM // 2
    inv = ROPE_THETA ** (-jnp.arange(half, dtype=F32) / half)
    ang = positions.astype(F32)[:, None] * inv
    cos, sin = jnp.cos(ang), jnp.sin(ang)
    z = jnp.zeros_like(cos)
    c = jnp.concatenate([cos, cos, z, z], axis=-1)
    s1 = jnp.concatenate([-sin, z, z, z], axis=-1)
    s2 = jnp.concatenate([z, sin, z, z], axis=-1)
    return c, s1, s2


def kernel(x, positions, rel_bias_table, norm_mix_g, w_in, q_a_norm_g, kv_a_norm_g, w_q_b, w_kv_b, q_norm_a_g, k_norm_a_g, q_norm_b_g, k_norm_b_g, w_out, norm_ffn_g, w_ff_gate, w_ff_up, w_ff_down, w_router, w_exp_gate, w_exp_up, w_exp_down):
    batch, seq, d = x.shape
    depth = w_in.shape[0]
    outs = []
    bias = _dilated_bias(rel_bias_table)
    for bi in range(batch):
        xs = x.reshape(seq, d) if batch == 1 else x[bi]
        rope_c, rope_s1, rope_s2 = _rope_tables(positions[bi])
        for l in range(depth):
            w_in_l = _pad_cols(w_in[l], PROJ_COLS).astype(BF16)
            proj = norm_matmul(xs, norm_mix_g[l][None], w_in_l, tm=1024, tn=1024, out_dtype=F32)
            a = dilated_mixer(proj, bias, q_norm_a_g[l][None], k_norm_a_g[l][None])
            wq, wk, wv = _mla_weights(w_q_b[l], w_kv_b[l])
            qg = _pad_cols(q_norm_b_g[l][None] * (QK_DIM_B ** -0.5 * LOG2E), QK_PAD_B)
            kg = _pad_cols(k_norm_b_g[l][None], QK_PAD_B)
            qb, kb, vb = mla_prep(proj, q_a_norm_g[l][None], kv_a_norm_g[l][None], wq, wk, wv,
                                  qg, kg, rope_c, rope_s1, rope_s2, tm=512)
            b = mla_flash(qb, kb, vb, tq=2048, tk=512)
            w_out_l = w_out[l].astype(BF16)
            xs = out_proj(xs, a, b, w_out_l[:WIDTH_A], w_out_l[WIDTH_A:], tm=1024, tn=1024)
            gf = norm_ffn_g[l][None]
            if l % 2 == 0:
                i = l // 2
                xs = ffn(xs, gf, w_ff_gate[i].astype(BF16), w_ff_up[i].astype(BF16),
                         w_ff_down[i].astype(BF16), tm=512, tf=512)
            else:
                i = l // 2
                xs = moe(xs, gf, w_router[i], w_exp_gate[i].astype(BF16),
                         w_exp_up[i].astype(BF16), w_exp_down[i].astype(BF16), tm_rows=512, tf=512)
        outs.append(xs)
    return outs[0].reshape(1, seq, d) if batch == 1 else jnp.stack(outs, axis=0)
```

```python
import functools
import math

import jax
import jax.numpy as jnp
from jax import lax
from jax.experimental import pallas as pl
from jax.experimental.pallas import tpu as pltpu

F32 = jnp.float32
BF16 = jnp.bfloat16

D_MODEL = 2048
HEAD_DIM = 128
N_HEADS_A = 8
DILATED_PATTERNS = ((128, 1), (512, 4), (2048, 16))
BLK = 128
N_BUCKETS = 32
MAX_DISTANCE = 2048
N_HEADS_B = 8
Q_LORA = 512
KV_LORA = 256
NOPE_DIM = 128
ROPE_DIM = 64
V_DIM = 128
QK_DIM_B = NOPE_DIM + ROPE_DIM
QK_PAD_B = 256
VT_ROWS = V_DIM + 16
LOG2E = math.log2(math.e)
ROPE_THETA = 10000.0
WIDTH_A = N_HEADS_A * HEAD_DIM
WIDTH_B = N_HEADS_B * V_DIM
IN_COLS = 3 * WIDTH_A + Q_LORA + KV_LORA + ROPE_DIM
PROJ_COLS = 4096
D_FF = 5632
N_EXPERTS = 8
TOP_K = 2
ROW_SUB = 8
MOE_ROW_GROUP = 128
HALF = D_MODEL // 2
U32 = jnp.uint32
EPS = 1e-6
NEG = -1e30

SUPER = 2048
VMEM_LIMIT = 56 * 1024 * 1024


def _params(sem, vmem=VMEM_LIMIT):
    return pltpu.CompilerParams(dimension_semantics=sem, vmem_limit_bytes=vmem)


def _rms(x, g, n=None):
    ss = jnp.sum(x * x, axis=-1, keepdims=True)
    n = x.shape[-1] if n is None else n
    return x * lax.rsqrt(ss * (1.0 / n) + EPS) * g


def _norm_matmul_kernel(x_ref, g_ref, w_ref, o_ref, hn_ref):
    @pl.when(pl.program_id(1) == 0)
    def _():
        hn_ref[...] = _rms(x_ref[...], g_ref[...]).astype(BF16)

    o_ref[...] = jnp.dot(hn_ref[...], w_ref[...], preferred_element_type=F32).astype(o_ref.dtype)


def norm_matmul(x, g, w, *, tm, tn, out_dtype):
    s, k = x.shape
    n = w.shape[1]
    return pl.pallas_call(
        _norm_matmul_kernel,
        grid=(s // tm, n // tn),
        in_specs=[
            pl.BlockSpec((tm, k), lambda i, j: (i, 0)),
            pl.BlockSpec((1, k), lambda i, j: (0, 0)),
            pl.BlockSpec((k, tn), lambda i, j: (0, j)),
        ],
        out_specs=pl.BlockSpec((tm, tn), lambda i, j: (i, j)),
        out_shape=jax.ShapeDtypeStruct((s, n), out_dtype),
        scratch_shapes=[pltpu.VMEM((tm, k), BF16)],
        compiler_params=_params(("parallel", "arbitrary")),
        name="norm_matmul",
    )(x, g, w)


def _t5_bucket(n):
    max_exact = N_BUCKETS // 2
    nf = jnp.maximum(n, 1).astype(F32)
    large = max_exact + (jnp.log(nf / max_exact) / math.log(MAX_DISTANCE / max_exact)
                         * (N_BUCKETS - max_exact)).astype(jnp.int32)
    large = jnp.minimum(large, N_BUCKETS - 1)
    return jnp.where(n < max_exact, n, large)


def _dilated_bias(rel_bias):
    i = jnp.arange(BLK)[:, None]
    j = jnp.arange(2 * BLK)[None, :]
    delta = i + BLK - j
    out = []
    for window, dilation in DILATED_PATTERNS:
        band = (delta >= 0) & (delta <= window // dilation)
        bucket = _t5_bucket(jnp.maximum(delta, 0) * dilation)
        onehot = (bucket[None] == jnp.arange(N_BUCKETS)[:, None, None]).astype(F32)
        bias = jnp.einsum('nij,nh->hij', onehot, rel_bias.astype(F32), precision=lax.Precision.HIGHEST)
        out.append(jnp.where(band[None], bias * LOG2E, NEG))
    return jnp.stack(out, axis=0)


def _dilated_kernel(q_ref, kp_ref, kc_ref, vp_ref, vc_ref, bias_ref, qg_ref, kg_ref, o_ref,
                    qn_s, kn_s, v_s, op_s, lse_s):
    sb = pl.program_id(1)
    qn_s[...] = _rms(q_ref[...], qg_ref[...]) * (HEAD_DIM ** -0.5 * LOG2E)
    kn_s[0:SUPER, :] = _rms(kp_ref[...], kg_ref[...])
    kn_s[SUPER:2 * SUPER, :] = _rms(kc_ref[...], kg_ref[...])
    v_s[0:SUPER, :] = vp_ref[...]
    v_s[SUPER:2 * SUPER, :] = vc_ref[...]
    col = lax.broadcasted_iota(jnp.int32, (BLK, 2 * BLK), 1)
    no_prev = jnp.logical_and(sb == 0, col < BLK)

    for pi, (_, d) in enumerate(DILATED_PATTERNS):
        def rows(start, size, d=d):
            return pl.ds(start, size) if d == 1 else pl.ds(start, size, stride=d)

        bias = bias_ref[pi, 0]
        bias_first = jnp.where(no_prev, NEG, bias)
        for r in range(d):
            for b in range(SUPER // (d * BLK)):
                q0 = r + d * BLK * b
                k0 = SUPER + q0 - d * BLK
                qb = qn_s[rows(q0, BLK), :].astype(BF16)
                k2 = kn_s[rows(k0, 2 * BLK), :].astype(BF16)
                v2 = v_s[rows(k0, 2 * BLK), :].astype(BF16)
                s = lax.dot_general(qb, k2, (((1,), (1,)), ((), ())), preferred_element_type=F32)
                s = s + (bias_first if b == 0 else bias)
                m = jnp.max(s, axis=-1, keepdims=True)
                p = jnp.exp2(s - m)
                den = jnp.sum(p, axis=-1, keepdims=True)
                o = jnp.dot(p.astype(BF16), v2, preferred_element_type=F32) / den
                lse = m + jnp.log2(den)
                op_s[pi, rows(q0, BLK), :] = o
                lse_s[pi, rows(q0, BLK), :] = jnp.broadcast_to(lse, (BLK, HEAD_DIM))

    chunk = 256
    for c in range(SUPER // chunk):
        sl = slice(c * chunk, (c + 1) * chunk)
        l0, l1, l2 = lse_s[0, sl, :], lse_s[1, sl, :], lse_s[2, sl, :]
        mx = jnp.maximum(jnp.maximum(l0, l1), l2)
        e0, e1, e2 = jnp.exp2(l0 - mx), jnp.exp2(l1 - mx), jnp.exp2(l2 - mx)
        num = e0 * op_s[0, sl, :] + e1 * op_s[1, sl, :] + e2 * op_s[2, sl, :]
        o_ref[sl, :] = (num / (e0 + e1 + e2)).astype(o_ref.dtype)


def dilated_mixer(proj, bias, q_g, k_g):
    s = proj.shape[0]
    nsb = s // SUPER
    h8 = N_HEADS_A
    blk = (SUPER, HEAD_DIM)
    return pl.pallas_call(
        _dilated_kernel,
        grid=(h8, nsb),
        in_specs=[
            pl.BlockSpec(blk, lambda h, i: (i, h)),
            pl.BlockSpec(blk, lambda h, i: (jnp.maximum(i - 1, 0), h8 + h)),
            pl.BlockSpec(blk, lambda h, i: (i, h8 + h)),
            pl.BlockSpec(blk, lambda h, i: (jnp.maximum(i - 1, 0), 2 * h8 + h)),
            pl.BlockSpec(blk, lambda h, i: (i, 2 * h8 + h)),
            pl.BlockSpec((len(DILATED_PATTERNS), 1, BLK, 2 * BLK), lambda h, i: (0, h, 0, 0)),
            pl.BlockSpec((1, HEAD_DIM), lambda h, i: (0, 0)),
            pl.BlockSpec((1, HEAD_DIM), lambda h, i: (0, 0)),
        ],
        out_specs=pl.BlockSpec(blk, lambda h, i: (i, h)),
        out_shape=jax.ShapeDtypeStruct((s, WIDTH_A), BF16),
        scratch_shapes=[
            pltpu.VMEM((SUPER, HEAD_DIM), F32),
            pltpu.VMEM((2 * SUPER, HEAD_DIM), F32),
            pltpu.VMEM((2 * SUPER, HEAD_DIM), F32),
            pltpu.VMEM((len(DILATED_PATTERNS), SUPER, HEAD_DIM), F32),
            pltpu.VMEM((len(DILATED_PATTERNS), SUPER, HEAD_DIM), F32),
        ],
        compiler_params=_params(("parallel", "arbitrary")),
        name="dilated_mixer",
    )(proj, proj, proj, proj, proj, bias, q_g, k_g)


def _mla_prep_kernel(cq_ref, ckv_ref, kr_ref, qag_ref, kvag_ref, wq_ref, wk_ref, wv_ref,
                     qg_ref, kg_ref, c_ref, s1_ref, s2_ref, q_out, k_out, v_out):
    c, s1, s2 = c_ref[...], s1_ref[...], s2_ref[...]

    def rope(x):
        return x * c + pltpu.roll(x, 96, 1) * s1 + pltpu.roll(x, 32, 1) * s2

    cqn = _rms(cq_ref[...], qag_ref[...]).astype(BF16)
    ckvn = _rms(ckv_ref[...], kvag_ref[...]).astype(BF16)
    qpre = jnp.dot(cqn, wq_ref[...], preferred_element_type=F32)
    knope = jnp.dot(ckvn, wk_ref[...], preferred_element_type=F32)
    v = jnp.dot(ckvn, wv_ref[...], preferred_element_type=F32)
    kr = kr_ref[...]
    kr_ss = jnp.sum(kr * kr, axis=-1, keepdims=True)
    qg, kg = qg_ref[...], kg_ref[...]
    kr_roped = rope(kr * kg[:, NOPE_DIM:])
    inv_n = 1.0 / QK_DIM_B
    ones = jnp.ones((VT_ROWS - V_DIM, cq_ref.shape[0]), BF16)
    for h in range(N_HEADS_B):
        qh = qpre[:, h * QK_PAD_B:(h + 1) * QK_PAD_B]
        rs = lax.rsqrt(jnp.sum(qh * qh, axis=-1, keepdims=True) * inv_n + EPS)
        q_out[h, 0:NOPE_DIM, :] = (qh[:, :NOPE_DIM] * rs * qg[:, :NOPE_DIM]).T.astype(BF16)
        q_out[h, NOPE_DIM:QK_PAD_B, :] = rope(qh[:, NOPE_DIM:] * rs * qg[:, NOPE_DIM:]).T.astype(BF16)
        kh = knope[:, h * NOPE_DIM:(h + 1) * NOPE_DIM]
        rs = lax.rsqrt((jnp.sum(kh * kh, axis=-1, keepdims=True) + kr_ss) * inv_n + EPS)
        k_out[h, :, 0:NOPE_DIM] = (kh * rs * kg[:, :NOPE_DIM]).astype(BF16)
        k_out[h, :, NOPE_DIM:QK_PAD_B] = (kr_roped * rs).astype(BF16)
        v_out[h, 0:V_DIM, :] = v[:, h * V_DIM:(h + 1) * V_DIM].T.astype(BF16)
        v_out[h, V_DIM:VT_ROWS, :] = ones


def mla_prep(proj, q_a_g, kv_a_g, wq, wk, wv, qg, kg, rope_c, rope_s1, rope_s2, *, tm):
    s = proj.shape[0]
    hb = N_HEADS_B
    full = lambda shape: pl.BlockSpec(shape, lambda i: (0,) * len(shape))
    cq_blk = (3 * WIDTH_A) // Q_LORA
    ckv_blk = (3 * WIDTH_A + Q_LORA) // KV_LORA
    kr_blk = (3 * WIDTH_A + Q_LORA + KV_LORA) // 128
    return pl.pallas_call(
        _mla_prep_kernel,
        grid=(s // tm,),
        in_specs=[
            pl.BlockSpec((tm, Q_LORA), lambda i: (i, cq_blk)),
            pl.BlockSpec((tm, KV_LORA), lambda i: (i, ckv_blk)),
            pl.BlockSpec((tm, 128), lambda i: (i, kr_blk)),
            full((1, Q_LORA)), full((1, KV_LORA)),
            full(wq.shape), full(wk.shape), full(wv.shape),
            full((1, QK_PAD_B)), full((1, QK_PAD_B)),
            pl.BlockSpec((tm, 128), lambda i: (i, 0)),
            pl.BlockSpec((tm, 128), lambda i: (i, 0)),
            pl.BlockSpec((tm, 128), lambda i: (i, 0)),
        ],
        out_specs=[
            pl.BlockSpec((hb, QK_PAD_B, tm), lambda i: (0, 0, i)),
            pl.BlockSpec((hb, tm, QK_PAD_B), lambda i: (0, i, 0)),
            pl.BlockSpec((hb, VT_ROWS, tm), lambda i: (0, 0, i)),
        ],
        out_shape=[
            jax.ShapeDtypeStruct((hb, QK_PAD_B, s), BF16),
            jax.ShapeDtypeStruct((hb, s, QK_PAD_B), BF16),
            jax.ShapeDtypeStruct((hb, VT_ROWS, s), BF16),
        ],
        compiler_params=_params(("parallel",)),
        name="mla_prep",
    )(proj, proj, proj, q_a_g, kv_a_g, wq, wk, wv, qg, kg, rope_c, rope_s1, rope_s2)


def _flash_kernel(qt_ref, k_ref, vt_ref, o_ref, s_a, s_b, m_s, acc_s, *, tq, tk):
    i = pl.program_id(1)
    m_s[...] = jnp.full(m_s.shape, NEG, F32)
    acc_s[...] = jnp.zeros(acc_s.shape, F32)
    per_tile = tq // tk
    assert per_tile % 2 == 0

    def compute(c, dst, q_lo=0):
        start = pl.multiple_of(c * tk, tk)
        dst[:, q_lo:] = jnp.dot(k_ref[0, pl.ds(start, tk), :], qt_ref[0, :, q_lo:],
                                preferred_element_type=F32)

    def process(c, src, q_lo=0, masked=False):
        s = src[:, q_lo:]
        if masked:
            key = c * tk + lax.broadcasted_iota(jnp.int32, s.shape, 0)
            qry = i * tq + q_lo + lax.broadcasted_iota(jnp.int32, s.shape, 1)
            s = jnp.where(key <= qry, s, NEG)
        m_prev = m_s[:, q_lo:]
        m_new = jnp.maximum(m_prev, jnp.max(s, axis=0, keepdims=True))
        alpha = jnp.exp2(m_prev - m_new)
        p = jnp.exp2(s - m_new).astype(BF16)
        start = pl.multiple_of(c * tk, tk)
        pv = jnp.dot(vt_ref[0, :, pl.ds(start, tk)], p, preferred_element_type=F32)
        acc_s[:, q_lo:] = alpha * acc_s[:, q_lo:] + pv
        m_s[:, q_lo:] = m_new

    bufs = (s_a, s_b)
    compute(0, s_a)
    n_full = per_tile * i

    def trip(t, carry):
        for u in range(per_tile):
            compute(t * per_tile + u + 1, bufs[(u + 1) % 2])
            process(t * per_tile + u, bufs[u % 2])
        return carry

    lax.fori_loop(0, i, trip, 0)
    for d in range(per_tile):
        if d + 1 < per_tile:
            compute(n_full + d + 1, bufs[(d + 1) % 2], (d + 1) * tk)
        process(n_full + d, bufs[d % 2], d * tk, masked=True)
    acc = acc_s[...]
    o_t = acc[0:V_DIM, :] / acc[V_DIM:V_DIM + 1, :]
    o_ref[...] = o_t.T.astype(o_ref.dtype)


def mla_flash(qt, k, vt, *, tq, tk):
    hb, s, _ = k.shape
    return pl.pallas_call(
        functools.partial(_flash_kernel, tq=tq, tk=tk),
        grid=(hb, s // tq),
        in_specs=[
            pl.BlockSpec((1, QK_PAD_B, tq), lambda h, i: (h, 0, i)),
            pl.BlockSpec((1, s, QK_PAD_B), lambda h, i: (h, 0, 0)),
            pl.BlockSpec((1, VT_ROWS, s), lambda h, i: (h, 0, 0)),
        ],
        out_specs=pl.BlockSpec((tq, V_DIM), lambda h, i: (i, h)),
        out_shape=jax.ShapeDtypeStruct((s, WIDTH_B), BF16),
        scratch_shapes=[
            pltpu.VMEM((tk, tq), F32),
            pltpu.VMEM((tk, tq), F32),
            pltpu.VMEM((1, tq), F32),
            pltpu.VMEM((VT_ROWS, tq), F32),
        ],
        compiler_params=_params(("parallel", "arbitrary")),
        name="mla_flash",
    )(qt, k, vt)


def _out_proj_kernel(x_ref, a_ref, b_ref, wa_ref, wb_ref, o_ref):
    acc = jnp.dot(a_ref[...], wa_ref[...], preferred_element_type=F32)
    acc = acc + jnp.dot(b_ref[...], wb_ref[...], preferred_element_type=F32)
    o_ref[...] = x_ref[...] + acc


def out_proj(x, a, b, wa, wb, *, tm, tn):
    s, n = x.shape
    ka, kb = a.shape[1], b.shape[1]
    return pl.pallas_call(
        _out_proj_kernel,
        grid=(s // tm, n // tn),
        in_specs=[
            pl.BlockSpec((tm, tn), lambda i, j: (i, j)),
            pl.BlockSpec((tm, ka), lambda i, j: (i, 0)),
            pl.BlockSpec((tm, kb), lambda i, j: (i, 0)),
            pl.BlockSpec((ka, tn), lambda i, j: (0, j)),
            pl.BlockSpec((kb, tn), lambda i, j: (0, j)),
        ],
        out_specs=pl.BlockSpec((tm, tn), lambda i, j: (i, j)),
        out_shape=jax.ShapeDtypeStruct((s, n), F32),
        compiler_params=_params(("parallel", "arbitrary")),
        name="out_proj",
    )(x, a, b, wa, wb)


def _swiglu_step(h, wg, wu, wd):
    gate = jnp.dot(h, wg, preferred_element_type=F32)
    up = jnp.dot(h, wu, preferred_element_type=F32)
    act = gate * jax.nn.sigmoid(gate) * up
    return jnp.dot(act.astype(BF16), wd, preferred_element_type=F32)


def _ffn_kernel(x_ref, g_ref, wg_ref, wu_ref, wd_ref, o_ref, hn_ref):
    @pl.when(pl.program_id(1) == 0)
    def _():
        hn_ref[...] = _rms(x_ref[...], g_ref[...]).astype(BF16)
        o_ref[...] = x_ref[...]

    o_ref[...] += _swiglu_step(hn_ref[...], wg_ref[...], wu_ref[...], wd_ref[...])


def ffn(x, g, wg, wu, wd, *, tm, tf):
    s, d = x.shape
    fdim = wg.shape[1]
    return pl.pallas_call(
        _ffn_kernel,
        grid=(s // tm, fdim // tf),
        in_specs=[
            pl.BlockSpec((tm, d), lambda i, f: (i, 0)),
            pl.BlockSpec((1, d), lambda i, f: (0, 0)),
            pl.BlockSpec((d, tf), lambda i, f: (0, f)),
            pl.BlockSpec((d, tf), lambda i, f: (0, f)),
            pl.BlockSpec((tf, d), lambda i, f: (f, 0)),
        ],
        out_specs=pl.BlockSpec((tm, d), lambda i, f: (i, 0)),
        out_shape=jax.ShapeDtypeStruct((s, d), F32),
        scratch_shapes=[pltpu.VMEM((tm, d), BF16)],
        compiler_params=_params(("parallel", "arbitrary")),
        name="ffn",
    )(x, g, wg, wu, wd)


def _pack_rows(y, dst):
    n = y.shape[0]
    for s in range(ROW_SUB):
        lo = y[:, s * 128:(s + 1) * 128].astype(BF16).astype(F32)
        hi = y[:, HALF + s * 128:HALF + (s + 1) * 128].astype(BF16).astype(F32)
        w = (lax.bitcast_convert_type(lo, U32) >> 16) | lax.bitcast_convert_type(hi, U32)
        dst[pl.ds(s, n, stride=ROW_SUB), :] = w


def _unpack_rows(src, n):
    lo, hi = [], []
    for s in range(ROW_SUB):
        w = src[pl.ds(s, n, stride=ROW_SUB), :]
        lo.append(lax.bitcast_convert_type(w << 16, F32))
        hi.append(lax.bitcast_convert_type(w & jnp.uint32(0xFFFF0000), F32))
    return lo + hi


META_I1, META_I2, META_R1, META_R2, META_G1, META_G2 = range(6)


def _router_kernel(x_ref, g_ref, wr_ref, hp_ref, meta_ref, cnt_ref, carry):
    @pl.when(pl.program_id(0) == 0)
    def _():
        carry[...] = jnp.zeros(carry.shape, F32)

    hn = _rms(x_ref[...], g_ref[...])
    _pack_rows(hn, hp_ref)
    h_hi = hn.astype(BF16)
    h_lo = (hn - h_hi.astype(F32)).astype(BF16)
    w_hi, w_lo = wr_ref[0], wr_ref[1]
    logits = (jnp.dot(h_hi, w_hi, preferred_element_type=F32)
              + jnp.dot(h_hi, w_lo, preferred_element_type=F32)
              + jnp.dot(h_lo, w_hi, preferred_element_type=F32))
    lane = lax.broadcasted_iota(jnp.int32, logits.shape, 1)
    logits = jnp.where(lane < N_EXPERTS, logits, NEG)
    v1 = jnp.max(logits, axis=-1, keepdims=True)
    i1 = jnp.min(jnp.where(logits == v1, lane, 128), axis=-1, keepdims=True)
    rest = jnp.where(lane == i1, NEG, logits)
    v2 = jnp.max(rest, axis=-1, keepdims=True)
    i2 = jnp.min(jnp.where(rest == v2, lane, 128), axis=-1, keepdims=True)
    e2 = jnp.exp(v2 - v1)
    g1 = 1.0 / (1.0 + e2)
    g2 = e2 / (1.0 + e2)

    tm = hn.shape[0]
    member = jnp.where(jnp.logical_or(lane == i1, lane == i2), 1.0, 0.0)
    row = lax.broadcasted_iota(jnp.int32, (tm, tm), 0)
    col = lax.broadcasted_iota(jnp.int32, (tm, tm), 1)
    earlier = jnp.where(col < row, 1.0, 0.0).astype(BF16)
    rank = carry[...] + jnp.dot(earlier, member.astype(BF16), preferred_element_type=F32)
    r1 = jnp.sum(jnp.where(lane == i1, rank, 0.0), axis=-1, keepdims=True)
    r2 = jnp.sum(jnp.where(lane == i2, rank, 0.0), axis=-1, keepdims=True)
    carry[...] += jnp.sum(member, axis=0, keepdims=True)
    cnt_ref[...] = carry[...]
    meta = jnp.zeros(logits.shape, F32)
    for pos, val in ((META_I1, i1.astype(F32)), (META_I2, i2.astype(F32)), (META_R1, r1),
                     (META_R2, r2), (META_G1, g1), (META_G2, g2)):
        meta = jnp.where(lane == pos, val, meta)
    meta_ref[...] = meta


def router(x, g, wr, *, tm):
    s, d = x.shape
    return pl.pallas_call(
        _router_kernel,
        grid=(s // tm,),
        in_specs=[
            pl.BlockSpec((tm, d), lambda i: (i, 0)),
            pl.BlockSpec((1, d), lambda i: (0, 0)),
            pl.BlockSpec((2, d, 128), lambda i: (0, 0, 0)),
        ],
        out_specs=[
            pl.BlockSpec((tm * ROW_SUB, 128), lambda i: (i, 0)),
            pl.BlockSpec((tm, 128), lambda i: (i, 0)),
            pl.BlockSpec((1, 128), lambda i: (0, 0)),
        ],
        out_shape=[
            jax.ShapeDtypeStruct((s * ROW_SUB, 128), U32),
            jax.ShapeDtypeStruct((s, 128), F32),
            jax.ShapeDtypeStruct((1, 128), F32),
        ],
        scratch_shapes=[pltpu.VMEM((1, 128), F32)],
        compiler_params=_params(("arbitrary",)),
        name="router",
    )(x, g, wr)


def _row_tile(ref, r):
    return ref.at[pl.ds(pl.multiple_of(r * ROW_SUB, ROW_SUB), ROW_SUB), :]


def _dispatch_kernel(dest_ref, hp_hbm, xs_in, xs_out, stage, in_sem, out_sem, *, tm):
    del xs_in
    i = pl.program_id(0)
    last = pl.num_programs(0) - 1
    block_rows = tm * ROW_SUB

    def fetch(step):
        src = hp_hbm.at[pl.ds(pl.multiple_of(step * block_rows, block_rows), block_rows), :]
        return pltpu.make_async_copy(src, stage.at[step % 3], in_sem.at[step % 3])

    def drain(step):
        for _ in range(TOP_K):
            pltpu.make_async_copy(stage.at[step % 3], xs_out.at[pl.ds(0, block_rows), :],
                                  out_sem.at[step % 2]).wait()

    @pl.when(i == 0)
    def _():
        fetch(0).start()

    @pl.when(i < last)
    def _():
        fetch(i + 1).start()

    fetch(i).wait()
    src = stage.at[i % 3]

    def issue(t, c):
        for k in range(TOP_K):
            pltpu.make_async_copy(_row_tile(src, t), _row_tile(xs_out, dest_ref[2 * (i * tm + t) + k]),
                                  out_sem.at[i % 2]).start()
        return c

    lax.fori_loop(0, tm, issue, 0)

    @pl.when(i > 0)
    def _():
        drain(i - 1)

    @pl.when(i == last)
    def _():
        drain(i)


def dispatch(dest, hp, n_rows, *, tm):
    s = hp.shape[0] // ROW_SUB
    xs0 = jnp.zeros((n_rows * ROW_SUB, 128), U32)
    return pl.pallas_call(
        functools.partial(_dispatch_kernel, tm=tm),
        grid_spec=pltpu.PrefetchScalarGridSpec(
            num_scalar_prefetch=1,
            grid=(s // tm,),
            in_specs=[pl.BlockSpec(memory_space=pl.ANY), pl.BlockSpec(memory_space=pl.ANY)],
            out_specs=pl.BlockSpec(memory_space=pl.ANY),
            scratch_shapes=[pltpu.VMEM((3, tm * ROW_SUB, 128), U32),
                            pltpu.SemaphoreType.DMA((3,)), pltpu.SemaphoreType.DMA((2,))],
        ),
        out_shape=jax.ShapeDtypeStruct(xs0.shape, U32),
        input_output_aliases={2: 0},
        compiler_params=_params(("arbitrary",)),
        name="moe_dispatch",
    )(dest, hp, xs0)


def _moe_ffn_kernel(te_ref, nu_ref, fill_ref, xs_ref, wg_ref, wu_ref, wd_ref, ys_ref, xb, acc, *, tm):
    del te_ref, nu_ref
    f = pl.program_id(1)
    fill = fill_ref[pl.program_id(0)]
    used = fill > 0
    last = f == pl.num_programs(1) - 1

    @pl.when(jnp.logical_and(used, f == 0))
    def _():
        for c, v in enumerate(_unpack_rows(xs_ref, tm)):
            xb[:, c * 128:(c + 1) * 128] = v.astype(BF16)
        acc[...] = jnp.zeros(acc.shape, F32)

    for groups in range(1, tm // MOE_ROW_GROUP + 1):
        @pl.when(fill == groups)
        def _(rows=groups * MOE_ROW_GROUP):
            acc[0:rows, :] += _swiglu_step(xb[0:rows, :], wg_ref[0], wu_ref[0], wd_ref[0])

    @pl.when(jnp.logical_and(used, last))
    def _():
        _pack_rows(acc[...], ys_ref)

    @pl.when(jnp.logical_and(jnp.logical_not(used), last))
    def _():
        ys_ref[...] = jnp.zeros(ys_ref.shape, U32)


def moe_ffn(tile_expert, n_used, tile_fill, xs, wg, wu, wd, *, tm, tf):
    nt = tile_expert.shape[0]
    _, d, fdim = wg.shape
    nf = fdim // tf

    def f_idx(j, f, nu):
        return jnp.where(j < nu[0], f, nf - 1)

    return pl.pallas_call(
        functools.partial(_moe_ffn_kernel, tm=tm),
        grid_spec=pltpu.PrefetchScalarGridSpec(
            num_scalar_prefetch=3,
            grid=(nt, nf),
            in_specs=[
                pl.BlockSpec((tm * ROW_SUB, 128), lambda j, f, te, nu, fl: (jnp.minimum(j, nu[0] - 1), 0)),
                pl.BlockSpec((1, d, tf), lambda j, f, te, nu, fl: (te[j], 0, f_idx(j, f, nu))),
                pl.BlockSpec((1, d, tf), lambda j, f, te, nu, fl: (te[j], 0, f_idx(j, f, nu))),
                pl.BlockSpec((1, tf, d), lambda j, f, te, nu, fl: (te[j], f_idx(j, f, nu), 0)),
            ],
            out_specs=pl.BlockSpec((tm * ROW_SUB, 128), lambda j, f, te, nu, fl: (j, 0)),
            scratch_shapes=[pltpu.VMEM((tm, d), BF16), pltpu.VMEM((tm, d), F32)],
        ),
        out_shape=jax.ShapeDtypeStruct(xs.shape, U32),
        compiler_params=_params(("parallel", "arbitrary")),
        name="moe_ffn",
    )(tile_expert, n_used, tile_fill, xs, wg, wu, wd)


def _combine_kernel(dest_ref, x_ref, meta_ref, ys_hbm, o_ref, buf, sem, *, tm):
    i = pl.program_id(0)

    def issue_step(step, slot):
        def issue(t, c):
            for k in range(TOP_K):
                pltpu.make_async_copy(_row_tile(ys_hbm, dest_ref[2 * (step * tm + t) + k]),
                                      _row_tile(buf.at[slot, k], t), sem.at[slot]).start()
            return c
        lax.fori_loop(0, tm, issue, 0)

    @pl.when(i == 0)
    def _():
        issue_step(0, 0)

    @pl.when(i + 1 < pl.num_programs(0))
    def _():
        issue_step(i + 1, (i + 1) % 2)

    slot = i % 2
    for k in range(TOP_K):
        pltpu.make_async_copy(ys_hbm.at[pl.ds(0, tm * ROW_SUB), :], buf.at[slot, k], sem.at[slot]).wait()
    meta = meta_ref[...]
    g1 = meta[:, META_G1:META_G1 + 1]
    g2 = meta[:, META_G2:META_G2 + 1]
    y1 = _unpack_rows(buf.at[slot, 0], tm)
    y2 = _unpack_rows(buf.at[slot, 1], tm)
    for c in range(len(y1)):
        sl = slice(c * 128, (c + 1) * 128)
        o_ref[:, sl] = x_ref[:, sl] + g1 * y1[c] + g2 * y2[c]


def combine(dest, x, meta, ys, *, tm):
    s, d = x.shape
    return pl.pallas_call(
        functools.partial(_combine_kernel, tm=tm),
        grid_spec=pltpu.PrefetchScalarGridSpec(
            num_scalar_prefetch=1,
            grid=(s // tm,),
            in_specs=[
                pl.BlockSpec((tm, d), lambda i, dest: (i, 0)),
                pl.BlockSpec((tm, 128), lambda i, dest: (i, 0)),
                pl.BlockSpec(memory_space=pl.ANY),
            ],
            out_specs=pl.BlockSpec((tm, d), lambda i, dest: (i, 0)),
            scratch_shapes=[pltpu.VMEM((2, TOP_K, tm * ROW_SUB, 128), U32), pltpu.SemaphoreType.DMA((2,))],
        ),
        out_shape=jax.ShapeDtypeStruct((s, d), F32),
        compiler_params=_params(("arbitrary",)),
        name="moe_combine",
    )(dest, x, meta, ys)


def moe(x, g, w_router, wg, wu, wd, *, tm_rows, tf):
    s = x.shape[0]
    wr = _pad_cols(w_router.astype(F32), 128)
    wr_hi = wr.astype(BF16)
    wr_lo = (wr - wr_hi.astype(F32)).astype(BF16)
    hp, meta, cnt = router(x, g, jnp.stack([wr_hi, wr_lo]), tm=512)
    i1, i2 = meta[:, META_I1].astype(jnp.int32), meta[:, META_I2].astype(jnp.int32)
    r1, r2 = meta[:, META_R1].astype(jnp.int32), meta[:, META_R2].astype(jnp.int32)
    counts = cnt[0, :N_EXPERTS].astype(jnp.int32)
    padded = (counts + tm_rows - 1) // tm_rows * tm_rows
    ends = jnp.cumsum(padded)
    offs = ends - padded
    eids = jnp.arange(N_EXPERTS, dtype=jnp.int32)
    off1 = jnp.sum(jnp.where(i1[:, None] == eids, offs, 0), axis=1)
    off2 = jnp.sum(jnp.where(i2[:, None] == eids, offs, 0), axis=1)
    dest = jnp.stack([off1 + r1, off2 + r2], axis=1).reshape(-1)
    n_tiles = (TOP_K * s) // tm_rows + N_EXPERTS
    n_used = (ends[-1] // tm_rows).reshape(1)
    tile_start = jnp.minimum(jnp.arange(n_tiles, dtype=jnp.int32), n_used - 1) * tm_rows
    tile_expert = jnp.sum(tile_start[:, None] >= ends[None, :], axis=1).astype(jnp.int32)
    tile_ids = jnp.arange(n_tiles, dtype=jnp.int32)
    rows_end = jnp.sum(jnp.where(tile_expert[:, None] == eids, offs + counts, 0), axis=1)
    tile_rows = jnp.where(tile_ids < n_used, jnp.clip(rows_end - tile_ids * tm_rows, 0, tm_rows), 0)
    tile_fill = ((tile_rows + MOE_ROW_GROUP - 1) // MOE_ROW_GROUP).astype(jnp.int32)
    xs = dispatch(dest, hp, n_tiles * tm_rows, tm=256)
    ys = moe_ffn(tile_expert, n_used, tile_fill, xs, wg, wu, wd, tm=tm_rows, tf=tf)
    return combine(dest, x, meta, ys, tm=256)


def _pad_cols(w, n):
    return jnp.pad(w, ((0, 0), (0, n - w.shape[1])))


def _mla_weights(w_q_b, w_kv_b):
    wq = w_q_b.reshape(Q_LORA, N_HEADS_B, QK_DIM_B)
    wq = jnp.pad(wq, ((0, 0), (0, 0), (0, QK_PAD_B - QK_DIM_B))).reshape(Q_LORA, N_HEADS_B * QK_PAD_B)
    wkv = w_kv_b.reshape(KV_LORA, N_HEADS_B, NOPE_DIM + V_DIM)
    wk = wkv[:, :, :NOPE_DIM].reshape(KV_LORA, N_HEADS_B * NOPE_DIM)
    wv = wkv[:, :, NOPE_DIM:].reshape(KV_LORA, N_HEADS_B * V_DIM)
    return wq.astype(BF16), wk.astype(BF16), wv.astype(BF16)


def _rope_tables(positions):
    half = ROPE_DIM // 2
    inv = ROPE_THETA ** (-jnp.arange(half, dtype=F32) / half)
    ang = positions.astype(F32)[:, None] * inv
    cos, sin = jnp.cos(ang), jnp.sin(ang)
    z = jnp.zeros_like(cos)
    c = jnp.concatenate([cos, cos, z, z], axis=-1)
    s1 = jnp.concatenate([-sin, z, z, z], axis=-1)
    s2 = jnp.concatenate([z, sin, z, z], axis=-1)
    return c, s1, s2


def kernel(x, positions, rel_bias_table, norm_mix_g, w_in, q_a_norm_g, kv_a_norm_g, w_q_b, w_kv_b, q_norm_a_g, k_norm_a_g, q_norm_b_g, k_norm_b_g, w_out, norm_ffn_g, w_ff_gate, w_ff_up, w_ff_down, w_router, w_exp_gate, w_exp_up, w_exp_down):
    batch, seq, d = x.shape
    depth = w_in.shape[0]
    outs = []
    bias = _dilated_bias(rel_bias_table)
    for bi in range(batch):
        xs = x.reshape(seq, d) if batch == 1 else x[bi]
        rope_c, rope_s1, rope_s2 = _rope_tables(positions[bi])
        for l in range(depth):
            w_in_l = _pad_cols(w_in[l], PROJ_COLS).astype(BF16)
            proj = norm_matmul(xs, norm_mix_g[l][None], w_in_l, tm=1024, tn=1024, out_dtype=F32)
            a = dilated_mixer(proj, bias, q_norm_a_g[l][None], k_norm_a_g[l][None])
            wq, wk, wv = _mla_weights(w_q_b[l], w_kv_b[l])
            qg = _pad_cols(q_norm_b_g[l][None] * (QK_DIM_B ** -0.5 * LOG2E), QK_PAD_B)
            kg = _pad_cols(k_norm_b_g[l][None], QK_PAD_B)
            qb, kb, vb = mla_prep(proj, q_a_norm_g[l][None], kv_a_norm_g[l][None], wq, wk, wv,
                                  qg, kg, rope_c, rope_s1, rope_s2, tm=512)
            b = mla_flash(qb, kb, vb, tq=2048, tk=512)
            w_out_l = w_out[l].astype(BF16)
            xs = out_proj(xs, a, b, w_out_l[:WIDTH_A], w_out_l[WIDTH_A:], tm=1024, tn=1024)
            gf = norm_ffn_g[l][None]
            if l % 2 == 0:
                i = l // 2
                xs = ffn(xs, gf, w_ff_gate[i].astype(BF16), w_ff_up[i].astype(BF16),
                         w_ff_down[i].astype(BF16), tm=512, tf=512)
            else:
                i = l // 2
                xs = moe(xs, gf, w_router[i], w_exp_gate[i].astype(BF16),
                         w_exp_up[i].astype(BF16), w_exp_down[i].astype(BF16), tm_rows=512, tf=512)
        outs.append(xs)
    return outs[0].reshape(1, seq, d) if batch == 1 else jnp.stack(outs, axis=0)
```

```python
import functools
import math
from typing import NamedTuple

import jax
import jax.numpy as jnp
from jax import lax
from jax.experimental import pallas as pl
from jax.experimental.pallas import tpu as pltpu

F32 = jnp.float32
BF16 = jnp.bfloat16

D_MODEL = 2048
HEAD_DIM = 128
N_HEADS_A = 8
DILATED_PATTERNS = ((128, 1), (512, 4), (2048, 16))
BLK = 128
N_BUCKETS = 32
MAX_DISTANCE = 2048
N_HEADS_B = 8
Q_LORA = 512
KV_LORA = 256
NOPE_DIM = 128
ROPE_DIM = 64
V_DIM = 128
QK_DIM_B = NOPE_DIM + ROPE_DIM
QK_PAD_B = 256
VT_ROWS = V_DIM + 16
LOG2E = math.log2(math.e)
ROPE_THETA = 10000.0
WIDTH_A = N_HEADS_A * HEAD_DIM
WIDTH_B = N_HEADS_B * V_DIM
IN_COLS = 3 * WIDTH_A + Q_LORA + KV_LORA + ROPE_DIM
PROJ_COLS = 4096
D_FF = 5632
N_EXPERTS = 8
TOP_K = 2
ROW_SUB = 8
MOE_ROW_GROUP = 128
HALF = D_MODEL // 2
U32 = jnp.uint32
EPS = 1e-6
NEG = -1e30

SUPER = 2048
VMEM_LIMIT = 56 * 1024 * 1024


class _Tiles(NamedTuple):
    proj_rows: int = 1024
    prep_rows: int = 512
    flash_q: int = 2048
    flash_k: int = 512
    out_rows: int = 1024
    out_cols: int = 1024
    ffn_rows: int = 512
    ffn_cols: int = 512
    route_rows: int = 512
    move_rows: int = 256


T = _Tiles()


def _params(sem, vmem=VMEM_LIMIT):
    return pltpu.CompilerParams(dimension_semantics=sem, vmem_limit_bytes=vmem)


def _rms(x, g, n=None):
    ss = jnp.sum(x * x, axis=-1, keepdims=True)
    n = x.shape[-1] if n is None else n
    return x * lax.rsqrt(ss * (1.0 / n) + EPS) * g


def _norm_matmul_kernel(x_ref, g_ref, w_ref, wt_ref, o_ref, hn_ref, *, n_main):
    j = pl.program_id(1)

    @pl.when(j == 0)
    def _():
        hn_ref[...] = _rms(x_ref[...], g_ref[...]).astype(BF16)

    @pl.when(j < n_main)
    def _():
        o_ref[...] = jnp.dot(hn_ref[...], w_ref[...], preferred_element_type=F32).astype(o_ref.dtype)

    @pl.when(j == n_main)
    def _():
        o_ref[...] = jnp.dot(hn_ref[...], wt_ref[...], preferred_element_type=F32).astype(o_ref.dtype)


def norm_matmul(x, g, w, w_tail, *, tm, out_dtype):
    s, k = x.shape
    tn = w_tail.shape[1]
    n_main = w.shape[1] // tn
    return pl.pallas_call(
        functools.partial(_norm_matmul_kernel, n_main=n_main),
        grid=(s // tm, n_main + 1),
        in_specs=[
            pl.BlockSpec((tm, k), lambda i, j: (i, 0)),
            pl.BlockSpec((1, k), lambda i, j: (0, 0)),
            pl.BlockSpec((k, tn), lambda i, j: (0, jnp.minimum(j, n_main - 1))),
            pl.BlockSpec((k, tn), lambda i, j: (0, 0)),
        ],
        out_specs=pl.BlockSpec((tm, tn), lambda i, j: (i, j)),
        out_shape=jax.ShapeDtypeStruct((s, (n_main + 1) * tn), out_dtype),
        scratch_shapes=[pltpu.VMEM((tm, k), BF16)],
        compiler_params=_params(("parallel", "arbitrary")),
        name="norm_matmul",
    )(x, g, w, w_tail)


def _t5_bucket(n):
    max_exact = N_BUCKETS // 2
    nf = jnp.maximum(n, 1).astype(F32)
    large = max_exact + (jnp.log(nf / max_exact) / math.log(MAX_DISTANCE / max_exact)
                         * (N_BUCKETS - max_exact)).astype(jnp.int32)
    large = jnp.minimum(large, N_BUCKETS - 1)
    return jnp.where(n < max_exact, n, large)


def _dilated_bias(rel_bias):
    i = jnp.arange(BLK)[:, None]
    j = jnp.arange(2 * BLK)[None, :]
    delta = i + BLK - j
    out = []
    for window, dilation in DILATED_PATTERNS:
        band = (delta >= 0) & (delta <= window // dilation)
        bucket = _t5_bucket(jnp.maximum(delta, 0) * dilation)
        onehot = (bucket[None] == jnp.arange(N_BUCKETS)[:, None, None]).astype(F32)
        bias = jnp.einsum('nij,nh->hij', onehot, rel_bias.astype(F32), precision=lax.Precision.HIGHEST)
        out.append(jnp.where(band[None], bias * LOG2E, NEG))
    return jnp.stack(out, axis=0)


def _dilated_kernel(q_ref, kp_ref, kc_ref, vp_ref, vc_ref, bias_ref, qg_ref, kg_ref, o_ref,
                    qn_s, kn_s, v_s, op_s, lse_s):
    sb = pl.program_id(1)
    qn_s[...] = _rms(q_ref[...], qg_ref[...]) * (HEAD_DIM ** -0.5 * LOG2E)
    kn_s[0:SUPER, :] = _rms(kp_ref[...], kg_ref[...])
    kn_s[SUPER:2 * SUPER, :] = _rms(kc_ref[...], kg_ref[...])
    v_s[0:SUPER, :] = vp_ref[...]
    v_s[SUPER:2 * SUPER, :] = vc_ref[...]
    col = lax.broadcasted_iota(jnp.int32, (BLK, 2 * BLK), 1)
    no_prev = jnp.logical_and(sb == 0, col < BLK)

    for pi, (_, d) in enumerate(DILATED_PATTERNS):
        def rows(start, size, d=d):
            return pl.ds(start, size) if d == 1 else pl.ds(start, size, stride=d)

        bias = bias_ref[pi, 0]
        bias_first = jnp.where(no_prev, NEG, bias)
        for r in range(d):
            for b in range(SUPER // (d * BLK)):
                q0 = r + d * BLK * b
                k0 = SUPER + q0 - d * BLK
                qb = qn_s[rows(q0, BLK), :].astype(BF16)
                k2 = kn_s[rows(k0, 2 * BLK), :].astype(BF16)
                v2 = v_s[rows(k0, 2 * BLK), :].astype(BF16)
                s = lax.dot_general(qb, k2, (((1,), (1,)), ((), ())), preferred_element_type=F32)
                s = s + (bias_first if b == 0 else bias)
                m = jnp.max(s, axis=-1, keepdims=True)
                p = jnp.exp2(s - m)
                den = jnp.sum(p, axis=-1, keepdims=True)
                o = jnp.dot(p.astype(BF16), v2, preferred_element_type=F32) / den
                lse = m + jnp.log2(den)
                op_s[pi, rows(q0, BLK), :] = o
                lse_s[pi, rows(q0, BLK), :] = jnp.broadcast_to(lse, (BLK, HEAD_DIM))

    chunk = 256
    for c in range(SUPER // chunk):
        sl = slice(c * chunk, (c + 1) * chunk)
        l0, l1, l2 = lse_s[0, sl, :], lse_s[1, sl, :], lse_s[2, sl, :]
        mx = jnp.maximum(jnp.maximum(l0, l1), l2)
        e0, e1, e2 = jnp.exp2(l0 - mx), jnp.exp2(l1 - mx), jnp.exp2(l2 - mx)
        num = e0 * op_s[0, sl, :] + e1 * op_s[1, sl, :] + e2 * op_s[2, sl, :]
        o_ref[sl, :] = (num / (e0 + e1 + e2)).astype(o_ref.dtype)


def dilated_mixer(proj, bias, q_g, k_g):
    s = proj.shape[0]
    nsb = s // SUPER
    h8 = N_HEADS_A
    blk = (SUPER, HEAD_DIM)
    return pl.pallas_call(
        _dilated_kernel,
        grid=(h8, nsb),
        in_specs=[
            pl.BlockSpec(blk, lambda h, i: (i, h)),
            pl.BlockSpec(blk, lambda h, i: (jnp.maximum(i - 1, 0), h8 + h)),
            pl.BlockSpec(blk, lambda h, i: (i, h8 + h)),
            pl.BlockSpec(blk, lambda h, i: (jnp.maximum(i - 1, 0), 2 * h8 + h)),
            pl.BlockSpec(blk, lambda h, i: (i, 2 * h8 + h)),
            pl.BlockSpec((len(DILATED_PATTERNS), 1, BLK, 2 * BLK), lambda h, i: (0, h, 0, 0)),
            pl.BlockSpec((1, HEAD_DIM), lambda h, i: (0, 0)),
            pl.BlockSpec((1, HEAD_DIM), lambda h, i: (0, 0)),
        ],
        out_specs=pl.BlockSpec(blk, lambda h, i: (i, h)),
        out_shape=jax.ShapeDtypeStruct((s, WIDTH_A), BF16),
        scratch_shapes=[
            pltpu.VMEM((SUPER, HEAD_DIM), F32),
            pltpu.VMEM((2 * SUPER, HEAD_DIM), F32),
            pltpu.VMEM((2 * SUPER, HEAD_DIM), F32),
            pltpu.VMEM((len(DILATED_PATTERNS), SUPER, HEAD_DIM), F32),
            pltpu.VMEM((len(DILATED_PATTERNS), SUPER, HEAD_DIM), F32),
        ],
        compiler_params=_params(("parallel", "arbitrary")),
        name="dilated_mixer",
    )(proj, proj, proj, proj, proj, bias, q_g, k_g)


def _mla_prep_kernel(cq_ref, ckv_ref, kr_ref, qag_ref, kvag_ref, wq_ref, wk_ref, wv_ref,
                     qg_ref, kg_ref, c_ref, s1_ref, s2_ref, q_out, k_out, v_out):
    c, s1, s2 = c_ref[...], s1_ref[...], s2_ref[...]

    def rope(x):
        return x * c + pltpu.roll(x, 96, 1) * s1 + pltpu.roll(x, 32, 1) * s2

    cqn = _rms(cq_ref[...], qag_ref[...]).astype(BF16)
    ckvn = _rms(ckv_ref[...], kvag_ref[...]).astype(BF16)
    qpre = jnp.dot(cqn, wq_ref[...], preferred_element_type=F32)
    knope = jnp.dot(ckvn, wk_ref[...], preferred_element_type=F32)
    v = jnp.dot(ckvn, wv_ref[...], preferred_element_type=F32)
    kr = kr_ref[...]
    kr_ss = jnp.sum(kr * kr, axis=-1, keepdims=True)
    qg, kg = qg_ref[...], kg_ref[...]
    kr_roped = rope(kr * kg[:, NOPE_DIM:])
    inv_n = 1.0 / QK_DIM_B
    ones = jnp.ones((VT_ROWS - V_DIM, cq_ref.shape[0]), BF16)
    for h in range(N_HEADS_B):
        qh = qpre[:, h * QK_PAD_B:(h + 1) * QK_PAD_B]
        rs = lax.rsqrt(jnp.sum(qh * qh, axis=-1, keepdims=True) * inv_n + EPS)
        q_out[h, 0:NOPE_DIM, :] = (qh[:, :NOPE_DIM] * rs * qg[:, :NOPE_DIM]).T.astype(BF16)
        q_out[h, NOPE_DIM:QK_PAD_B, :] = rope(qh[:, NOPE_DIM:] * rs * qg[:, NOPE_DIM:]).T.astype(BF16)
        kh = knope[:, h * NOPE_DIM:(h + 1) * NOPE_DIM]
        rs = lax.rsqrt((jnp.sum(kh * kh, axis=-1, keepdims=True) + kr_ss) * inv_n + EPS)
        k_out[h, :, 0:NOPE_DIM] = (kh * rs * kg[:, :NOPE_DIM]).astype(BF16)
        k_out[h, :, NOPE_DIM:QK_PAD_B] = (kr_roped * rs).astype(BF16)
        v_out[h, 0:V_DIM, :] = v[:, h * V_DIM:(h + 1) * V_DIM].T.astype(BF16)
        v_out[h, V_DIM:VT_ROWS, :] = ones


def mla_prep(proj, q_a_g, kv_a_g, wq, wk, wv, qg, kg, rope_c, rope_s1, rope_s2, *, tm):
    s = proj.shape[0]
    hb = N_HEADS_B
    full = lambda shape: pl.BlockSpec(shape, lambda i: (0,) * len(shape))
    cq_blk = (3 * WIDTH_A) // Q_LORA
    ckv_blk = (3 * WIDTH_A + Q_LORA) // KV_LORA
    kr_blk = (3 * WIDTH_A + Q_LORA + KV_LORA) // 128
    return pl.pallas_call(
        _mla_prep_kernel,
        grid=(s // tm,),
        in_specs=[
            pl.BlockSpec((tm, Q_LORA), lambda i: (i, cq_blk)),
            pl.BlockSpec((tm, KV_LORA), lambda i: (i, ckv_blk)),
            pl.BlockSpec((tm, 128), lambda i: (i, kr_blk)),
            full((1, Q_LORA)), full((1, KV_LORA)),
            full(wq.shape), full(wk.shape), full(wv.shape),
            full((1, QK_PAD_B)), full((1, QK_PAD_B)),
            pl.BlockSpec((tm, 128), lambda i: (i, 0)),
            pl.BlockSpec((tm, 128), lambda i: (i, 0)),
            pl.BlockSpec((tm, 128), lambda i: (i, 0)),
        ],
        out_specs=[
            pl.BlockSpec((hb, QK_PAD_B, tm), lambda i: (0, 0, i)),
            pl.BlockSpec((hb, tm, QK_PAD_B), lambda i: (0, i, 0)),
            pl.BlockSpec((hb, VT_ROWS, tm), lambda i: (0, 0, i)),
        ],
        out_shape=[
            jax.ShapeDtypeStruct((hb, QK_PAD_B, s), BF16),
            jax.ShapeDtypeStruct((hb, s, QK_PAD_B), BF16),
            jax.ShapeDtypeStruct((hb, VT_ROWS, s), BF16),
        ],
        compiler_params=_params(("parallel",)),
        name="mla_prep",
    )(proj, proj, proj, q_a_g, kv_a_g, wq, wk, wv, qg, kg, rope_c, rope_s1, rope_s2)


def _flash_kernel(qt_ref, k_ref, vt_ref, o_ref, s_a, s_b, m_s, acc_s, *, tq, tk):
    i = pl.program_id(1)
    m_s[...] = jnp.full(m_s.shape, NEG, F32)
    acc_s[...] = jnp.zeros(acc_s.shape, F32)
    per_tile = tq // tk
    assert per_tile % 2 == 0

    def compute(c, dst, q_lo=0):
        start = pl.multiple_of(c * tk, tk)
        dst[:, q_lo:] = jnp.dot(k_ref[0, pl.ds(start, tk), :], qt_ref[0, :, q_lo:],
                                preferred_element_type=F32)

    def process(c, src, q_lo=0, masked=False):
        s = src[:, q_lo:]
        if masked:
            key = c * tk + lax.broadcasted_iota(jnp.int32, s.shape, 0)
            qry = i * tq + q_lo + lax.broadcasted_iota(jnp.int32, s.shape, 1)
            s = jnp.where(key <= qry, s, NEG)
        m_prev = m_s[:, q_lo:]
        m_new = jnp.maximum(m_prev, jnp.max(s, axis=0, keepdims=True))
        alpha = jnp.exp2(m_prev - m_new)
        p = jnp.exp2(s - m_new).astype(BF16)
        start = pl.multiple_of(c * tk, tk)
        pv = jnp.dot(vt_ref[0, :, pl.ds(start, tk)], p, preferred_element_type=F32)
        acc_s[:, q_lo:] = alpha * acc_s[:, q_lo:] + pv
        m_s[:, q_lo:] = m_new

    bufs = (s_a, s_b)
    compute(0, s_a)
    n_full = per_tile * i

    def trip(t, carry):
        for u in range(per_tile):
            compute(t * per_tile + u + 1, bufs[(u + 1) % 2])
            process(t * per_tile + u, bufs[u % 2])
        return carry

    lax.fori_loop(0, i, trip, 0)
    for d in range(per_tile):
        if d + 1 < per_tile:
            compute(n_full + d + 1, bufs[(d + 1) % 2], (d + 1) * tk)
        process(n_full + d, bufs[d % 2], d * tk, masked=True)
    acc = acc_s[...]
    o_t = acc[0:V_DIM, :] / acc[V_DIM:V_DIM + 1, :]
    o_ref[...] = o_t.T.astype(o_ref.dtype)


def mla_flash(qt, k, vt, *, tq, tk):
    hb, s, _ = k.shape
    return pl.pallas_call(
        functools.partial(_flash_kernel, tq=tq, tk=tk),
        grid=(hb, s // tq),
        in_specs=[
            pl.BlockSpec((1, QK_PAD_B, tq), lambda h, i: (h, 0, i)),
            pl.BlockSpec((1, s, QK_PAD_B), lambda h, i: (h, 0, 0)),
            pl.BlockSpec((1, VT_ROWS, s), lambda h, i: (h, 0, 0)),
        ],
        out_specs=pl.BlockSpec((tq, V_DIM), lambda h, i: (i, h)),
        out_shape=jax.ShapeDtypeStruct((s, WIDTH_B), BF16),
        scratch_shapes=[
            pltpu.VMEM((tk, tq), F32),
            pltpu.VMEM((tk, tq), F32),
            pltpu.VMEM((1, tq), F32),
            pltpu.VMEM((VT_ROWS, tq), F32),
        ],
        compiler_params=_params(("parallel", "arbitrary")),
        name="mla_flash",
    )(qt, k, vt)


def _out_proj_kernel(x_ref, a_ref, b_ref, wa_ref, wb_ref, o_ref):
    acc = jnp.dot(a_ref[...], wa_ref[...], preferred_element_type=F32)
    acc = acc + jnp.dot(b_ref[...], wb_ref[...], preferred_element_type=F32)
    o_ref[...] = x_ref[...] + acc


def out_proj(x, a, b, wa, wb, *, tm, tn):
    s, n = x.shape
    ka, kb = a.shape[1], b.shape[1]
    return pl.pallas_call(
        _out_proj_kernel,
        grid=(s // tm, n // tn),
        in_specs=[
            pl.BlockSpec((tm, tn), lambda i, j: (i, j)),
            pl.BlockSpec((tm, ka), lambda i, j: (i, 0)),
            pl.BlockSpec((tm, kb), lambda i, j: (i, 0)),
            pl.BlockSpec((ka, tn), lambda i, j: (0, j)),
            pl.BlockSpec((kb, tn), lambda i, j: (0, j)),
        ],
        out_specs=pl.BlockSpec((tm, tn), lambda i, j: (i, j)),
        out_shape=jax.ShapeDtypeStruct((s, n), F32),
        compiler_params=_params(("parallel", "arbitrary")),
        name="out_proj",
    )(x, a, b, wa, wb)


def _swiglu_step(h, wg, wu, wd):
    gate = jnp.dot(h, wg, preferred_element_type=F32)
    up = jnp.dot(h, wu, preferred_element_type=F32)
    act = gate * jax.nn.sigmoid(gate) * up
    return jnp.dot(act.astype(BF16), wd, preferred_element_type=F32)


def _ffn_kernel(x_ref, g_ref, wg_ref, wu_ref, wd_ref, o_ref, hn_ref):
    @pl.when(pl.program_id(1) == 0)
    def _():
        hn_ref[...] = _rms(x_ref[...], g_ref[...]).astype(BF16)
        o_ref[...] = x_ref[...]

    o_ref[...] += _swiglu_step(hn_ref[...], wg_ref[...], wu_ref[...], wd_ref[...])


def ffn(x, g, wg, wu, wd, *, tm, tf):
    s, d = x.shape
    fdim = wg.shape[1]
    return pl.pallas_call(
        _ffn_kernel,
        grid=(s // tm, fdim // tf),
        in_specs=[
            pl.BlockSpec((tm, d), lambda i, f: (i, 0)),
            pl.BlockSpec((1, d), lambda i, f: (0, 0)),
            pl.BlockSpec((d, tf), lambda i, f: (0, f)),
            pl.BlockSpec((d, tf), lambda i, f: (0, f)),
            pl.BlockSpec((tf, d), lambda i, f: (f, 0)),
        ],
        out_specs=pl.BlockSpec((tm, d), lambda i, f: (i, 0)),
        out_shape=jax.ShapeDtypeStruct((s, d), F32),
        scratch_shapes=[pltpu.VMEM((tm, d), BF16)],
        compiler_params=_params(("parallel", "arbitrary")),
        name="ffn",
    )(x, g, wg, wu, wd)


def _pack_rows(y, dst):
    n = y.shape[0]
    for s in range(ROW_SUB):
        lo = y[:, s * 128:(s + 1) * 128].astype(BF16).astype(F32)
        hi = y[:, HALF + s * 128:HALF + (s + 1) * 128].astype(BF16).astype(F32)
        w = (lax.bitcast_convert_type(lo, U32) >> 16) | lax.bitcast_convert_type(hi, U32)
        dst[pl.ds(s, n, stride=ROW_SUB), :] = w


def _unpack_rows(src, n):
    lo, hi = [], []
    for s in range(ROW_SUB):
        w = src[pl.ds(s, n, stride=ROW_SUB), :]
        lo.append(lax.bitcast_convert_type(w << 16, F32))
        hi.append(lax.bitcast_convert_type(w & jnp.uint32(0xFFFF0000), F32))
    return lo + hi


META_I1, META_I2, META_R1, META_R2, META_G1, META_G2 = range(6)


def _router_kernel(x_ref, g_ref, wr_ref, hp_ref, meta_ref, cnt_ref, carry):
    @pl.when(pl.program_id(0) == 0)
    def _():
        carry[...] = jnp.zeros(carry.shape, F32)

    hn = _rms(x_ref[...], g_ref[...])
    _pack_rows(hn, hp_ref)
    h_hi = hn.astype(BF16)
    h_lo = (hn - h_hi.astype(F32)).astype(BF16)
    w_hi, w_lo = wr_ref[0], wr_ref[1]
    logits = (jnp.dot(h_hi, w_hi, preferred_element_type=F32)
              + jnp.dot(h_hi, w_lo, preferred_element_type=F32)
              + jnp.dot(h_lo, w_hi, preferred_element_type=F32))
    lane = lax.broadcasted_iota(jnp.int32, logits.shape, 1)
    logits = jnp.where(lane < N_EXPERTS, logits, NEG)
    v1 = jnp.max(logits, axis=-1, keepdims=True)
    i1 = jnp.min(jnp.where(logits == v1, lane, 128), axis=-1, keepdims=True)
    rest = jnp.where(lane == i1, NEG, logits)
    v2 = jnp.max(rest, axis=-1, keepdims=True)
    i2 = jnp.min(jnp.where(rest == v2, lane, 128), axis=-1, keepdims=True)
    e2 = jnp.exp(v2 - v1)
    g1 = 1.0 / (1.0 + e2)
    g2 = e2 / (1.0 + e2)

    tm = hn.shape[0]
    member = jnp.where(jnp.logical_or(lane == i1, lane == i2), 1.0, 0.0)
    row = lax.broadcasted_iota(jnp.int32, (tm, tm), 0)
    col = lax.broadcasted_iota(jnp.int32, (tm, tm), 1)
    earlier = jnp.where(col < row, 1.0, 0.0).astype(BF16)
    rank = carry[...] + jnp.dot(earlier, member.astype(BF16), preferred_element_type=F32)
    r1 = jnp.sum(jnp.where(lane == i1, rank, 0.0), axis=-1, keepdims=True)
    r2 = jnp.sum(jnp.where(lane == i2, rank, 0.0), axis=-1, keepdims=True)
    carry[...] += jnp.sum(member, axis=0, keepdims=True)
    cnt_ref[...] = carry[...]
    meta = jnp.zeros(logits.shape, F32)
    for pos, val in ((META_I1, i1.astype(F32)), (META_I2, i2.astype(F32)), (META_R1, r1),
                     (META_R2, r2), (META_G1, g1), (META_G2, g2)):
        meta = jnp.where(lane == pos, val, meta)
    meta_ref[...] = meta


def router(x, g, wr, *, tm):
    s, d = x.shape
    return pl.pallas_call(
        _router_kernel,
        grid=(s // tm,),
        in_specs=[
            pl.BlockSpec((tm, d), lambda i: (i, 0)),
            pl.BlockSpec((1, d), lambda i: (0, 0)),
            pl.BlockSpec((2, d, 128), lambda i: (0, 0, 0)),
        ],
        out_specs=[
            pl.BlockSpec((tm * ROW_SUB, 128), lambda i: (i, 0)),
            pl.BlockSpec((tm, 128), lambda i: (i, 0)),
            pl.BlockSpec((1, 128), lambda i: (0, 0)),
        ],
        out_shape=[
            jax.ShapeDtypeStruct((s * ROW_SUB, 128), U32),
            jax.ShapeDtypeStruct((s, 128), F32),
            jax.ShapeDtypeStruct((1, 128), F32),
        ],
        scratch_shapes=[pltpu.VMEM((1, 128), F32)],
        compiler_params=_params(("arbitrary",)),
        name="router",
    )(x, g, wr)


def _row_tile(ref, r):
    return ref.at[pl.ds(pl.multiple_of(r * ROW_SUB, ROW_SUB), ROW_SUB), :]


def _dispatch_kernel(dest_ref, hp_hbm, xs_in, xs_out, stage, in_sem, out_sem, *, tm):
    del xs_in
    i = pl.program_id(0)
    last = pl.num_programs(0) - 1
    block_rows = tm * ROW_SUB

    def fetch(step):
        src = hp_hbm.at[pl.ds(pl.multiple_of(step * block_rows, block_rows), block_rows), :]
        return pltpu.make_async_copy(src, stage.at[step % 3], in_sem.at[step % 3])

    def drain(step):
        for _ in range(TOP_K):
            pltpu.make_async_copy(stage.at[step % 3], xs_out.at[pl.ds(0, block_rows), :],
                                  out_sem.at[step % 2]).wait()

    @pl.when(i == 0)
    def _():
        fetch(0).start()

    @pl.when(i < last)
    def _():
        fetch(i + 1).start()

    fetch(i).wait()
    src = stage.at[i % 3]

    def issue(t, c):
        for k in range(TOP_K):
            pltpu.make_async_copy(_row_tile(src, t), _row_tile(xs_out, dest_ref[2 * (i * tm + t) + k]),
                                  out_sem.at[i % 2]).start()
        return c

    lax.fori_loop(0, tm, issue, 0)

    @pl.when(i > 0)
    def _():
        drain(i - 1)

    @pl.when(i == last)
    def _():
        drain(i)


def dispatch(dest, hp, n_rows, *, tm):
    s = hp.shape[0] // ROW_SUB
    xs0 = jnp.zeros((n_rows * ROW_SUB, 128), U32)
    return pl.pallas_call(
        functools.partial(_dispatch_kernel, tm=tm),
        grid_spec=pltpu.PrefetchScalarGridSpec(
            num_scalar_prefetch=1,
            grid=(s // tm,),
            in_specs=[pl.BlockSpec(memory_space=pl.ANY), pl.BlockSpec(memory_space=pl.ANY)],
            out_specs=pl.BlockSpec(memory_space=pl.ANY),
            scratch_shapes=[pltpu.VMEM((3, tm * ROW_SUB, 128), U32),
                            pltpu.SemaphoreType.DMA((3,)), pltpu.SemaphoreType.DMA((2,))],
        ),
        out_shape=jax.ShapeDtypeStruct(xs0.shape, U32),
        input_output_aliases={2: 0},
        compiler_params=_params(("arbitrary",)),
        name="moe_dispatch",
    )(dest, hp, xs0)


def _moe_ffn_kernel(te_ref, nu_ref, fill_ref, xs_ref, wg_ref, wu_ref, wd_ref, ys_ref, xb, acc, *, tm):
    del te_ref, nu_ref
    f = pl.program_id(1)
    fill = fill_ref[pl.program_id(0)]
    used = fill > 0
    last = f == pl.num_programs(1) - 1

    @pl.when(jnp.logical_and(used, f == 0))
    def _():
        for c, v in enumerate(_unpack_rows(xs_ref, tm)):
            xb[:, c * 128:(c + 1) * 128] = v.astype(BF16)
        acc[...] = jnp.zeros(acc.shape, F32)

    for groups in range(1, tm // MOE_ROW_GROUP + 1):
        @pl.when(fill == groups)
        def _(rows=groups * MOE_ROW_GROUP):
            acc[0:rows, :] += _swiglu_step(xb[0:rows, :], wg_ref[0], wu_ref[0], wd_ref[0])

    @pl.when(jnp.logical_and(used, last))
    def _():
        _pack_rows(acc[...], ys_ref)

    @pl.when(jnp.logical_and(jnp.logical_not(used), last))
    def _():
        ys_ref[...] = jnp.zeros(ys_ref.shape, U32)


def moe_ffn(tile_expert, n_used, tile_fill, xs, wg, wu, wd, *, tm, tf):
    nt = tile_expert.shape[0]
    _, d, fdim = wg.shape
    nf = fdim // tf

    def f_idx(j, f, nu):
        return jnp.where(j < nu[0], f, nf - 1)

    return pl.pallas_call(
        functools.partial(_moe_ffn_kernel, tm=tm),
        grid_spec=pltpu.PrefetchScalarGridSpec(
            num_scalar_prefetch=3,
            grid=(nt, nf),
            in_specs=[
                pl.BlockSpec((tm * ROW_SUB, 128), lambda j, f, te, nu, fl: (jnp.minimum(j, nu[0] - 1), 0)),
                pl.BlockSpec((1, d, tf), lambda j, f, te, nu, fl: (te[j], 0, f_idx(j, f, nu))),
                pl.BlockSpec((1, d, tf), lambda j, f, te, nu, fl: (te[j], 0, f_idx(j, f, nu))),
                pl.BlockSpec((1, tf, d), lambda j, f, te, nu, fl: (te[j], f_idx(j, f, nu), 0)),
            ],
            out_specs=pl.BlockSpec((tm * ROW_SUB, 128), lambda j, f, te, nu, fl: (j, 0)),
            scratch_shapes=[pltpu.VMEM((tm, d), BF16), pltpu.VMEM((tm, d), F32)],
        ),
        out_shape=jax.ShapeDtypeStruct(xs.shape, U32),
        compiler_params=_params(("parallel", "arbitrary")),
        name="moe_ffn",
    )(tile_expert, n_used, tile_fill, xs, wg, wu, wd)


def _combine_kernel(dest_ref, x_ref, meta_ref, ys_hbm, o_ref, buf, sem, *, tm):
    i = pl.program_id(0)

    def issue_step(step, slot):
        def issue(t, c):
            for k in range(TOP_K):
                pltpu.make_async_copy(_row_tile(ys_hbm, dest_ref[2 * (step * tm + t) + k]),
                                      _row_tile(buf.at[slot, k], t), sem.at[slot]).start()
            return c
        lax.fori_loop(0, tm, issue, 0)

    @pl.when(i == 0)
    def _():
        issue_step(0, 0)

    @pl.when(i + 1 < pl.num_programs(0))
    def _():
        issue_step(i + 1, (i + 1) % 2)

    slot = i % 2
    for k in range(TOP_K):
        pltpu.make_async_copy(ys_hbm.at[pl.ds(0, tm * ROW_SUB), :], buf.at[slot, k], sem.at[slot]).wait()
    meta = meta_ref[...]
    g1 = meta[:, META_G1:META_G1 + 1]
    g2 = meta[:, META_G2:META_G2 + 1]
    y1 = _unpack_rows(buf.at[slot, 0], tm)
    y2 = _unpack_rows(buf.at[slot, 1], tm)
    for c in range(len(y1)):
        sl = slice(c * 128, (c + 1) * 128)
        o_ref[:, sl] = x_ref[:, sl] + g1 * y1[c] + g2 * y2[c]


def combine(dest, x, meta, ys, *, tm):
    s, d = x.shape
    return pl.pallas_call(
        functools.partial(_combine_kernel, tm=tm),
        grid_spec=pltpu.PrefetchScalarGridSpec(
            num_scalar_prefetch=1,
            grid=(s // tm,),
            in_specs=[
                pl.BlockSpec((tm, d), lambda i, dest: (i, 0)),
                pl.BlockSpec((tm, 128), lambda i, dest: (i, 0)),
                pl.BlockSpec(memory_space=pl.ANY),
            ],
            out_specs=pl.BlockSpec((tm, d), lambda i, dest: (i, 0)),
            scratch_shapes=[pltpu.VMEM((2, TOP_K, tm * ROW_SUB, 128), U32), pltpu.SemaphoreType.DMA((2,))],
        ),
        out_shape=jax.ShapeDtypeStruct((s, d), F32),
        compiler_params=_params(("arbitrary",)),
        name="moe_combine",
    )(dest, x, meta, ys)


def moe(x, g, w_router, wg, wu, wd, *, tm_rows, tf):
    s = x.shape[0]
    wr = _pad_cols(w_router.astype(F32), 128)
    wr_hi = wr.astype(BF16)
    wr_lo = (wr - wr_hi.astype(F32)).astype(BF16)
    hp, meta, cnt = router(x, g, jnp.stack([wr_hi, wr_lo]), tm=T.route_rows)
    i1, i2 = meta[:, META_I1].astype(jnp.int32), meta[:, META_I2].astype(jnp.int32)
    r1, r2 = meta[:, META_R1].astype(jnp.int32), meta[:, META_R2].astype(jnp.int32)
    counts = cnt[0, :N_EXPERTS].astype(jnp.int32)
    padded = (counts + tm_rows - 1) // tm_rows * tm_rows
    ends = jnp.cumsum(padded)
    offs = ends - padded
    eids = jnp.arange(N_EXPERTS, dtype=jnp.int32)
    off1 = jnp.sum(jnp.where(i1[:, None] == eids, offs, 0), axis=1)
    off2 = jnp.sum(jnp.where(i2[:, None] == eids, offs, 0), axis=1)
    dest = jnp.stack([off1 + r1, off2 + r2], axis=1).reshape(-1)
    n_tiles = (TOP_K * s) // tm_rows + N_EXPERTS
    n_used = (ends[-1] // tm_rows).reshape(1)
    tile_start = jnp.minimum(jnp.arange(n_tiles, dtype=jnp.int32), n_used - 1) * tm_rows
    tile_expert = jnp.sum(tile_start[:, None] >= ends[None, :], axis=1).astype(jnp.int32)
    tile_ids = jnp.arange(n_tiles, dtype=jnp.int32)
    rows_end = jnp.sum(jnp.where(tile_expert[:, None] == eids, offs + counts, 0), axis=1)
    tile_rows = jnp.where(tile_ids < n_used, jnp.clip(rows_end - tile_ids * tm_rows, 0, tm_rows), 0)
    tile_fill = ((tile_rows + MOE_ROW_GROUP - 1) // MOE_ROW_GROUP).astype(jnp.int32)
    xs = dispatch(dest, hp, n_tiles * tm_rows, tm=T.move_rows)
    ys = moe_ffn(tile_expert, n_used, tile_fill, xs, wg, wu, wd, tm=tm_rows, tf=tf)
    return combine(dest, x, meta, ys, tm=T.move_rows)


def _pad_cols(w, n):
    return jnp.pad(w, ((0, 0), (0, n - w.shape[1])))


def _mla_weights(w_q_b, w_kv_b):
    wq = w_q_b.reshape(Q_LORA, N_HEADS_B, QK_DIM_B)
    wq = jnp.pad(wq, ((0, 0), (0, 0), (0, QK_PAD_B - QK_DIM_B))).reshape(Q_LORA, N_HEADS_B * QK_PAD_B)
    wkv = w_kv_b.reshape(KV_LORA, N_HEADS_B, NOPE_DIM + V_DIM)
    wk = wkv[:, :, :NOPE_DIM].reshape(KV_LORA, N_HEADS_B * NOPE_DIM)
    wv = wkv[:, :, NOPE_DIM:].reshape(KV_LORA, N_HEADS_B * V_DIM)
    return wq.astype(BF16), wk.astype(BF16), wv.astype(BF16)


def _rope_tables(positions):
    half = ROPE_DIM // 2
    inv = ROPE_THETA ** (-jnp.arange(half, dtype=F32) / half)
    ang = positions.astype(F32)[:, None] * inv
    cos, sin = jnp.cos(ang), jnp.sin(ang)
    z = jnp.zeros_like(cos)
    c = jnp.concatenate([cos, cos, z, z], axis=-1)
    s1 = jnp.concatenate([-sin, z, z, z], axis=-1)
    s2 = jnp.concatenate([z, sin, z, z], axis=-1)
    return c, s1, s2


def kernel(x, positions, rel_bias_table, norm_mix_g, w_in, q_a_norm_g, kv_a_norm_g, w_q_b, w_kv_b, q_norm_a_g, k_norm_a_g, q_norm_b_g, k_norm_b_g, w_out, norm_ffn_g, w_ff_gate, w_ff_up, w_ff_down, w_router, w_exp_gate, w_exp_up, w_exp_down):
    batch, seq, d = x.shape
    depth = w_in.shape[0]
    outs = []
    bias = _dilated_bias(rel_bias_table)
    for bi in range(batch):
        xs = x.reshape(seq, d) if batch == 1 else x[bi]
        rope_c, rope_s1, rope_s2 = _rope_tables(positions[bi])
        for l in range(depth):
            w_main = w_in[l][:, :3 * WIDTH_A].astype(BF16)
            w_tail = _pad_cols(w_in[l][:, 3 * WIDTH_A:], PROJ_COLS - 3 * WIDTH_A).astype(BF16)
            proj = norm_matmul(xs, norm_mix_g[l][None], w_main, w_tail, tm=T.proj_rows, out_dtype=F32)
            a = dilated_mixer(proj, bias, q_norm_a_g[l][None], k_norm_a_g[l][None])
            wq, wk, wv = _mla_weights(w_q_b[l], w_kv_b[l])
            qg = _pad_cols(q_norm_b_g[l][None] * (QK_DIM_B ** -0.5 * LOG2E), QK_PAD_B)
            kg = _pad_cols(k_norm_b_g[l][None], QK_PAD_B)
            qb, kb, vb = mla_prep(proj, q_a_norm_g[l][None], kv_a_norm_g[l][None], wq, wk, wv,
                                  qg, kg, rope_c, rope_s1, rope_s2, tm=T.prep_rows)
            b = mla_flash(qb, kb, vb, tq=T.flash_q, tk=T.flash_k)
            w_out_l = w_out[l].astype(BF16)
            xs = out_proj(xs, a, b, w_out_l[:WIDTH_A], w_out_l[WIDTH_A:], tm=T.out_rows, tn=T.out_cols)
            gf = norm_ffn_g[l][None]
            if l % 2 == 0:
                i = l // 2
                xs = ffn(xs, gf, w_ff_gate[i].astype(BF16), w_ff_up[i].astype(BF16),
                         w_ff_down[i].astype(BF16), tm=T.ffn_rows, tf=T.ffn_cols)
            else:
                i = l // 2
                xs = moe(xs, gf, w_router[i], w_exp_gate[i].astype(BF16),
                         w_exp_up[i].astype(BF16), w_exp_down[i].astype(BF16),
                         tm_rows=T.ffn_rows, tf=T.ffn_cols)
        outs.append(xs)
    return outs[0].reshape(1, seq, d) if batch == 1 else jnp.stack(outs, axis=0)
```

```python
import functools
import math
from typing import NamedTuple

import jax
import jax.numpy as jnp
from jax import lax
from jax.experimental import pallas as pl
from jax.experimental.pallas import tpu as pltpu

F32 = jnp.float32
BF16 = jnp.bfloat16

D_MODEL = 2048
HEAD_DIM = 128
N_HEADS_A = 8
DILATED_PATTERNS = ((128, 1), (512, 4), (2048, 16))
BLK = 128
N_BUCKETS = 32
MAX_DISTANCE = 2048
N_HEADS_B = 8
Q_LORA = 512
KV_LORA = 256
NOPE_DIM = 128
ROPE_DIM = 64
V_DIM = 128
QK_DIM_B = NOPE_DIM + ROPE_DIM
QK_PAD_B = 256
VT_ROWS = V_DIM + 16
LOG2E = math.log2(math.e)
ROPE_THETA = 10000.0
WIDTH_A = N_HEADS_A * HEAD_DIM
WIDTH_B = N_HEADS_B * V_DIM
IN_COLS = 3 * WIDTH_A + Q_LORA + KV_LORA + ROPE_DIM
PROJ_COLS = 4096
D_FF = 5632
N_EXPERTS = 8
TOP_K = 2
ROW_SUB = 8
MOE_ROW_GROUP = 128
HALF = D_MODEL // 2
U32 = jnp.uint32
EPS = 1e-6
NEG = -1e30

SUPER = 2048
VMEM_LIMIT = 56 * 1024 * 1024


class _Tiles(NamedTuple):
    proj_rows: int = 1024
    prep_rows: int = 512
    flash_q: int = 2048
    flash_k: int = 512
    out_rows: int = 1024
    out_cols: int = 1024
    ffn_rows: int = 512
    ffn_cols: int = 512
    route_rows: int = 512
    move_rows: int = 256


T = _Tiles()


def _params(sem, vmem=VMEM_LIMIT):
    return pltpu.CompilerParams(dimension_semantics=sem, vmem_limit_bytes=vmem)


def _rms(x, g, n=None):
    ss = jnp.sum(x * x, axis=-1, keepdims=True)
    n = x.shape[-1] if n is None else n
    return x * lax.rsqrt(ss * (1.0 / n) + EPS) * g


def _norm_matmul_kernel(x_ref, g_ref, w_ref, wt_ref, o_ref, hn_ref, *, n_main):
    j = pl.program_id(1)

    @pl.when(j == 0)
    def _():
        hn_ref[...] = _rms(x_ref[...], g_ref[...]).astype(BF16)

    @pl.when(j < n_main)
    def _():
        o_ref[...] = jnp.dot(hn_ref[...], w_ref[...], preferred_element_type=F32).astype(o_ref.dtype)

    @pl.when(j == n_main)
    def _():
        o_ref[...] = jnp.dot(hn_ref[...], wt_ref[...], preferred_element_type=F32).astype(o_ref.dtype)


def norm_matmul(x, g, w, w_tail, *, tm, out_dtype):
    s, k = x.shape
    tn = w_tail.shape[1]
    n_main = w.shape[1] // tn
    return pl.pallas_call(
        functools.partial(_norm_matmul_kernel, n_main=n_main),
        grid=(s // tm, n_main + 1),
        in_specs=[
            pl.BlockSpec((tm, k), lambda i, j: (i, 0)),
            pl.BlockSpec((1, k), lambda i, j: (0, 0)),
            pl.BlockSpec((k, tn), lambda i, j: (0, jnp.minimum(j, n_main - 1))),
            pl.BlockSpec((k, tn), lambda i, j: (0, 0)),
        ],
        out_specs=pl.BlockSpec((tm, tn), lambda i, j: (i, j)),
        out_shape=jax.ShapeDtypeStruct((s, (n_main + 1) * tn), out_dtype),
        scratch_shapes=[pltpu.VMEM((tm, k), BF16)],
        compiler_params=_params(("parallel", "arbitrary")),
        name="norm_matmul",
    )(x, g, w, w_tail)


def _t5_bucket(n):
    max_exact = N_BUCKETS // 2
    nf = jnp.maximum(n, 1).astype(F32)
    large = max_exact + (jnp.log(nf / max_exact) / math.log(MAX_DISTANCE / max_exact)
                         * (N_BUCKETS - max_exact)).astype(jnp.int32)
    large = jnp.minimum(large, N_BUCKETS - 1)
    return jnp.where(n < max_exact, n, large)


def _dilated_bias(rel_bias):
    i = jnp.arange(BLK)[:, None]
    j = jnp.arange(2 * BLK)[None, :]
    delta = i + BLK - j
    out = []
    for window, dilation in DILATED_PATTERNS:
        band = (delta >= 0) & (delta <= window // dilation)
        bucket = _t5_bucket(jnp.maximum(delta, 0) * dilation)
        onehot = (bucket[None] == jnp.arange(N_BUCKETS)[:, None, None]).astype(F32)
        bias = jnp.einsum('nij,nh->hij', onehot, rel_bias.astype(F32), precision=lax.Precision.HIGHEST)
        out.append(jnp.where(band[None], bias * LOG2E, NEG))
    return jnp.stack(out, axis=0)


def _dilated_kernel(q_ref, kp_ref, kc_ref, vp_ref, vc_ref, bias_ref, qg_ref, kg_ref, o_ref,
                    qn_s, kn_s, v_s, op_s, lse_s):
    sb = pl.program_id(1)
    qn_s[...] = _rms(q_ref[...], qg_ref[...]) * (HEAD_DIM ** -0.5 * LOG2E)
    kn_s[0:SUPER, :] = _rms(kp_ref[...], kg_ref[...])
    kn_s[SUPER:2 * SUPER, :] = _rms(kc_ref[...], kg_ref[...])
    v_s[0:SUPER, :] = vp_ref[...]
    v_s[SUPER:2 * SUPER, :] = vc_ref[...]
    col = lax.broadcasted_iota(jnp.int32, (BLK, 2 * BLK), 1)
    no_prev = jnp.logical_and(sb == 0, col < BLK)

    for pi, (_, d) in enumerate(DILATED_PATTERNS):
        def rows(start, size, d=d):
            return pl.ds(start, size) if d == 1 else pl.ds(start, size, stride=d)

        bias = bias_ref[pi, 0]
        bias_first = jnp.where(no_prev, NEG, bias)
        for r in range(d):
            for b in range(SUPER // (d * BLK)):
                q0 = r + d * BLK * b
                k0 = SUPER + q0 - d * BLK
                qb = qn_s[rows(q0, BLK), :].astype(BF16)
                k2 = kn_s[rows(k0, 2 * BLK), :].astype(BF16)
                v2 = v_s[rows(k0, 2 * BLK), :].astype(BF16)
                s = lax.dot_general(qb, k2, (((1,), (1,)), ((), ())), preferred_element_type=F32)
                s = s + (bias_first if b == 0 else bias)
                m = jnp.max(s, axis=-1, keepdims=True)
                p = jnp.exp2(s - m)
                den = jnp.sum(p, axis=-1, keepdims=True)
                o = jnp.dot(p.astype(BF16), v2, preferred_element_type=F32) / den
                lse = m + jnp.log2(den)
                op_s[pi, rows(q0, BLK), :] = o
                lse_s[pi, rows(q0, BLK), :] = jnp.broadcast_to(lse, (BLK, HEAD_DIM))

    chunk = 256
    for c in range(SUPER // chunk):
        sl = slice(c * chunk, (c + 1) * chunk)
        l0, l1, l2 = lse_s[0, sl, :], lse_s[1, sl, :], lse_s[2, sl, :]
        mx = jnp.maximum(jnp.maximum(l0, l1), l2)
        e0, e1, e2 = jnp.exp2(l0 - mx), jnp.exp2(l1 - mx), jnp.exp2(l2 - mx)
        num = e0 * op_s[0, sl, :] + e1 * op_s[1, sl, :] + e2 * op_s[2, sl, :]
        o_ref[sl, :] = (num / (e0 + e1 + e2)).astype(o_ref.dtype)


def dilated_mixer(proj, bias, q_g, k_g):
    s = proj.shape[0]
    nsb = s // SUPER
    h8 = N_HEADS_A
    blk = (SUPER, HEAD_DIM)
    return pl.pallas_call(
        _dilated_kernel,
        grid=(h8, nsb),
        in_specs=[
            pl.BlockSpec(blk, lambda h, i: (i, h)),
            pl.BlockSpec(blk, lambda h, i: (jnp.maximum(i - 1, 0), h8 + h)),
            pl.BlockSpec(blk, lambda h, i: (i, h8 + h)),
            pl.BlockSpec(blk, lambda h, i: (jnp.maximum(i - 1, 0), 2 * h8 + h)),
            pl.BlockSpec(blk, lambda h, i: (i, 2 * h8 + h)),
            pl.BlockSpec((len(DILATED_PATTERNS), 1, BLK, 2 * BLK), lambda h, i: (0, h, 0, 0)),
            pl.BlockSpec((1, HEAD_DIM), lambda h, i: (0, 0)),
            pl.BlockSpec((1, HEAD_DIM), lambda h, i: (0, 0)),
        ],
        out_specs=pl.BlockSpec(blk, lambda h, i: (i, h)),
        out_shape=jax.ShapeDtypeStruct((s, WIDTH_A), BF16),
        scratch_shapes=[
            pltpu.VMEM((SUPER, HEAD_DIM), F32),
            pltpu.VMEM((2 * SUPER, HEAD_DIM), F32),
            pltpu.VMEM((2 * SUPER, HEAD_DIM), F32),
            pltpu.VMEM((len(DILATED_PATTERNS), SUPER, HEAD_DIM), F32),
            pltpu.VMEM((len(DILATED_PATTERNS), SUPER, HEAD_DIM), F32),
        ],
        compiler_params=_params(("parallel", "arbitrary")),
        name="dilated_mixer",
    )(proj, proj, proj, proj, proj, bias, q_g, k_g)


def _mla_prep_kernel(cq_ref, ckv_ref, kr_ref, qag_ref, kvag_ref, wq_ref, wk_ref, wv_ref,
                     qg_ref, kg_ref, c_ref, s1_ref, s2_ref, q_out, k_out, v_out):
    c, s1, s2 = c_ref[...], s1_ref[...], s2_ref[...]

    def rope(x):
        return x * c + pltpu.roll(x, 96, 1) * s1 + pltpu.roll(x, 32, 1) * s2

    cqn = _rms(cq_ref[...], qag_ref[...]).astype(BF16)
    ckvn = _rms(ckv_ref[...], kvag_ref[...]).astype(BF16)
    qpre = jnp.dot(cqn, wq_ref[...], preferred_element_type=F32)
    knope = jnp.dot(ckvn, wk_ref[...], preferred_element_type=F32)
    v = jnp.dot(ckvn, wv_ref[...], preferred_element_type=F32)
    kr = kr_ref[...]
    kr_ss = jnp.sum(kr * kr, axis=-1, keepdims=True)
    qg, kg = qg_ref[...], kg_ref[...]
    kr_roped = rope(kr * kg[:, NOPE_DIM:])
    inv_n = 1.0 / QK_DIM_B
    ones = jnp.ones((VT_ROWS - V_DIM, cq_ref.shape[0]), BF16)
    for h in range(N_HEADS_B):
        qh = qpre[:, h * QK_PAD_B:(h + 1) * QK_PAD_B]
        rs = lax.rsqrt(jnp.sum(qh * qh, axis=-1, keepdims=True) * inv_n + EPS)
        q_out[h, 0:NOPE_DIM, :] = (qh[:, :NOPE_DIM] * rs * qg[:, :NOPE_DIM]).T.astype(BF16)
        q_out[h, NOPE_DIM:QK_PAD_B, :] = rope(qh[:, NOPE_DIM:] * rs * qg[:, NOPE_DIM:]).T.astype(BF16)
        kh = knope[:, h * NOPE_DIM:(h + 1) * NOPE_DIM]
        rs = lax.rsqrt((jnp.sum(kh * kh, axis=-1, keepdims=True) + kr_ss) * inv_n + EPS)
        k_out[h, :, 0:NOPE_DIM] = (kh * rs * kg[:, :NOPE_DIM]).astype(BF16)
        k_out[h, :, NOPE_DIM:QK_PAD_B] = (kr_roped * rs).astype(BF16)
        v_out[h, 0:V_DIM, :] = v[:, h * V_DIM:(h + 1) * V_DIM].T.astype(BF16)
        v_out[h, V_DIM:VT_ROWS, :] = ones


def mla_prep(proj, q_a_g, kv_a_g, wq, wk, wv, qg, kg, rope_c, rope_s1, rope_s2, *, tm):
    s = proj.shape[0]
    hb = N_HEADS_B
    full = lambda shape: pl.BlockSpec(shape, lambda i: (0,) * len(shape))
    cq_blk = (3 * WIDTH_A) // Q_LORA
    ckv_blk = (3 * WIDTH_A + Q_LORA) // KV_LORA
    kr_blk = (3 * WIDTH_A + Q_LORA + KV_LORA) // 128
    return pl.pallas_call(
        _mla_prep_kernel,
        grid=(s // tm,),
        in_specs=[
            pl.BlockSpec((tm, Q_LORA), lambda i: (i, cq_blk)),
            pl.BlockSpec((tm, KV_LORA), lambda i: (i, ckv_blk)),
            pl.BlockSpec((tm, 128), lambda i: (i, kr_blk)),
            full((1, Q_LORA)), full((1, KV_LORA)),
            full(wq.shape), full(wk.shape), full(wv.shape),
            full((1, QK_PAD_B)), full((1, QK_PAD_B)),
            pl.BlockSpec((tm, 128), lambda i: (i, 0)),
            pl.BlockSpec((tm, 128), lambda i: (i, 0)),
            pl.BlockSpec((tm, 128), lambda i: (i, 0)),
        ],
        out_specs=[
            pl.BlockSpec((hb, QK_PAD_B, tm), lambda i: (0, 0, i)),
            pl.BlockSpec((hb, tm, QK_PAD_B), lambda i: (0, i, 0)),
            pl.BlockSpec((hb, VT_ROWS, tm), lambda i: (0, 0, i)),
        ],
        out_shape=[
            jax.ShapeDtypeStruct((hb, QK_PAD_B, s), BF16),
            jax.ShapeDtypeStruct((hb, s, QK_PAD_B), BF16),
            jax.ShapeDtypeStruct((hb, VT_ROWS, s), BF16),
        ],
        compiler_params=_params(("parallel",)),
        name="mla_prep",
    )(proj, proj, proj, q_a_g, kv_a_g, wq, wk, wv, qg, kg, rope_c, rope_s1, rope_s2)


def _flash_kernel(qt_ref, k_ref, vt_ref, o_ref, s_a, s_b, m_s, acc_s, *, tq, tk):
    i = pl.program_id(1)
    m_s[...] = jnp.full(m_s.shape, NEG, F32)
    acc_s[...] = jnp.zeros(acc_s.shape, F32)
    per_tile = tq // tk
    assert per_tile % 2 == 0

    def compute(c, dst, q_lo=0):
        start = pl.multiple_of(c * tk, tk)
        dst[:, q_lo:] = jnp.dot(k_ref[0, pl.ds(start, tk), :], qt_ref[0, :, q_lo:],
                                preferred_element_type=F32)

    def process(c, src, q_lo=0, masked=False):
        s = src[:, q_lo:]
        if masked:
            key = c * tk + lax.broadcasted_iota(jnp.int32, s.shape, 0)
            qry = i * tq + q_lo + lax.broadcasted_iota(jnp.int32, s.shape, 1)
            s = jnp.where(key <= qry, s, NEG)
        m_prev = m_s[:, q_lo:]
        m_new = jnp.maximum(m_prev, jnp.max(s, axis=0, keepdims=True))
        alpha = jnp.exp2(m_prev - m_new)
        p = jnp.exp2(s - m_new).astype(BF16)
        start = pl.multiple_of(c * tk, tk)
        pv = jnp.dot(vt_ref[0, :, pl.ds(start, tk)], p, preferred_element_type=F32)
        acc_s[:, q_lo:] = alpha * acc_s[:, q_lo:] + pv
        m_s[:, q_lo:] = m_new

    bufs = (s_a, s_b)
    compute(0, s_a)
    n_full = per_tile * i

    def trip(t, carry):
        for u in range(per_tile):
            compute(t * per_tile + u + 1, bufs[(u + 1) % 2])
            process(t * per_tile + u, bufs[u % 2])
        return carry

    lax.fori_loop(0, i, trip, 0)
    for d in range(per_tile):
        if d + 1 < per_tile:
            compute(n_full + d + 1, bufs[(d + 1) % 2], (d + 1) * tk)
        process(n_full + d, bufs[d % 2], d * tk, masked=True)
    acc = acc_s[...]
    o_t = acc[0:V_DIM, :] / acc[V_DIM:V_DIM + 1, :]
    o_ref[...] = o_t.T.astype(o_ref.dtype)


def mla_flash(qt, k, vt, *, tq, tk):
    hb, s, _ = k.shape
    return pl.pallas_call(
        functools.partial(_flash_kernel, tq=tq, tk=tk),
        grid=(hb, s // tq),
        in_specs=[
            pl.BlockSpec((1, QK_PAD_B, tq), lambda h, i: (h, 0, i)),
            pl.BlockSpec((1, s, QK_PAD_B), lambda h, i: (h, 0, 0)),
            pl.BlockSpec((1, VT_ROWS, s), lambda h, i: (h, 0, 0)),
        ],
        out_specs=pl.BlockSpec((tq, V_DIM), lambda h, i: (i, h)),
        out_shape=jax.ShapeDtypeStruct((s, WIDTH_B), BF16),
        scratch_shapes=[
            pltpu.VMEM((tk, tq), F32),
            pltpu.VMEM((tk, tq), F32),
            pltpu.VMEM((1, tq), F32),
            pltpu.VMEM((VT_ROWS, tq), F32),
        ],
        compiler_params=_params(("parallel", "arbitrary")),
        name="mla_flash",
    )(qt, k, vt)


def _out_proj_kernel(x_ref, a_ref, b_ref, wa_ref, wb_ref, o_ref):
    acc = jnp.dot(a_ref[...], wa_ref[...], preferred_element_type=F32)
    acc = acc + jnp.dot(b_ref[...], wb_ref[...], preferred_element_type=F32)
    o_ref[...] = x_ref[...] + acc


def out_proj(x, a, b, wa, wb, *, tm, tn):
    s, n = x.shape
    ka, kb = a.shape[1], b.shape[1]
    return pl.pallas_call(
        _out_proj_kernel,
        grid=(s // tm, n // tn),
        in_specs=[
            pl.BlockSpec((tm, tn), lambda i, j: (i, j)),
            pl.BlockSpec((tm, ka), lambda i, j: (i, 0)),
            pl.BlockSpec((tm, kb), lambda i, j: (i, 0)),
            pl.BlockSpec((ka, tn), lambda i, j: (0, j)),
            pl.BlockSpec((kb, tn), lambda i, j: (0, j)),
        ],
        out_specs=pl.BlockSpec((tm, tn), lambda i, j: (i, j)),
        out_shape=jax.ShapeDtypeStruct((s, n), F32),
        compiler_params=_params(("parallel", "arbitrary")),
        name="out_proj",
    )(x, a, b, wa, wb)


def _swiglu_step(h, wg, wu, wd):
    gate = jnp.dot(h, wg.astype(BF16), preferred_element_type=F32)
    up = jnp.dot(h, wu.astype(BF16), preferred_element_type=F32)
    act = gate * jax.nn.sigmoid(gate) * up
    return jnp.dot(act.astype(BF16), wd.astype(BF16), preferred_element_type=F32)


def _ffn_kernel(x_ref, g_ref, wg_ref, wu_ref, wd_ref, o_ref, hn_ref):
    @pl.when(pl.program_id(1) == 0)
    def _():
        hn_ref[...] = _rms(x_ref[...], g_ref[...]).astype(BF16)
        o_ref[...] = x_ref[...]

    o_ref[...] += _swiglu_step(hn_ref[...], wg_ref[...], wu_ref[...], wd_ref[...])


def ffn(x, g, wg, wu, wd, *, tm, tf):
    s, d = x.shape
    fdim = wg.shape[1]
    return pl.pallas_call(
        _ffn_kernel,
        grid=(s // tm, fdim // tf),
        in_specs=[
            pl.BlockSpec((tm, d), lambda i, f: (i, 0)),
            pl.BlockSpec((1, d), lambda i, f: (0, 0)),
            pl.BlockSpec((d, tf), lambda i, f: (0, f)),
            pl.BlockSpec((d, tf), lambda i, f: (0, f)),
            pl.BlockSpec((tf, d), lambda i, f: (f, 0)),
        ],
        out_specs=pl.BlockSpec((tm, d), lambda i, f: (i, 0)),
        out_shape=jax.ShapeDtypeStruct((s, d), F32),
        scratch_shapes=[pltpu.VMEM((tm, d), BF16)],
        compiler_params=_params(("parallel", "arbitrary")),
        name="ffn",
    )(x, g, wg, wu, wd)


def _pack_rows(y, dst):
    n = y.shape[0]
    for s in range(ROW_SUB):
        lo = y[:, s * 128:(s + 1) * 128].astype(BF16).astype(F32)
        hi = y[:, HALF + s * 128:HALF + (s + 1) * 128].astype(BF16).astype(F32)
        w = (lax.bitcast_convert_type(lo, U32) >> 16) | lax.bitcast_convert_type(hi, U32)
        dst[pl.ds(s, n, stride=ROW_SUB), :] = w


def _unpack_rows(src, n):
    lo, hi = [], []
    for s in range(ROW_SUB):
        w = src[pl.ds(s, n, stride=ROW_SUB), :]
        lo.append(lax.bitcast_convert_type(w << 16, F32))
        hi.append(lax.bitcast_convert_type(w & jnp.uint32(0xFFFF0000), F32))
    return lo + hi


META_I1, META_I2, META_R1, META_R2, META_G1, META_G2 = range(6)


def _router_kernel(x_ref, g_ref, wr_ref, hp_ref, meta_ref, cnt_ref, carry):
    @pl.when(pl.program_id(0) == 0)
    def _():
        carry[...] = jnp.zeros(carry.shape, F32)

    hn = _rms(x_ref[...], g_ref[...])
    _pack_rows(hn, hp_ref)
    h_hi = hn.astype(BF16)
    h_lo = (hn - h_hi.astype(F32)).astype(BF16)
    w_hi, w_lo = wr_ref[0], wr_ref[1]
    logits = (jnp.dot(h_hi, w_hi, preferred_element_type=F32)
              + jnp.dot(h_hi, w_lo, preferred_element_type=F32)
              + jnp.dot(h_lo, w_hi, preferred_element_type=F32))
    lane = lax.broadcasted_iota(jnp.int32, logits.shape, 1)
    logits = jnp.where(lane < N_EXPERTS, logits, NEG)
    v1 = jnp.max(logits, axis=-1, keepdims=True)
    i1 = jnp.min(jnp.where(logits == v1, lane, 128), axis=-1, keepdims=True)
    rest = jnp.where(lane == i1, NEG, logits)
    v2 = jnp.max(rest, axis=-1, keepdims=True)
    i2 = jnp.min(jnp.where(rest == v2, lane, 128), axis=-1, keepdims=True)
    e2 = jnp.exp(v2 - v1)
    g1 = 1.0 / (1.0 + e2)
    g2 = e2 / (1.0 + e2)

    tm = hn.shape[0]
    member = jnp.where(jnp.logical_or(lane == i1, lane == i2), 1.0, 0.0)
    row = lax.broadcasted_iota(jnp.int32, (tm, tm), 0)
    col = lax.broadcasted_iota(jnp.int32, (tm, tm), 1)
    earlier = jnp.where(col < row, 1.0, 0.0).astype(BF16)
    rank = carry[...] + jnp.dot(earlier, member.astype(BF16), preferred_element_type=F32)
    r1 = jnp.sum(jnp.where(lane == i1, rank, 0.0), axis=-1, keepdims=True)
    r2 = jnp.sum(jnp.where(lane == i2, rank, 0.0), axis=-1, keepdims=True)
    carry[...] += jnp.sum(member, axis=0, keepdims=True)
    cnt_ref[...] = carry[...]
    meta = jnp.zeros(logits.shape, F32)
    for pos, val in ((META_I1, i1.astype(F32)), (META_I2, i2.astype(F32)), (META_R1, r1),
                     (META_R2, r2), (META_G1, g1), (META_G2, g2)):
        meta = jnp.where(lane == pos, val, meta)
    meta_ref[...] = meta


def router(x, g, wr, *, tm):
    s, d = x.shape
    return pl.pallas_call(
        _router_kernel,
        grid=(s // tm,),
        in_specs=[
            pl.BlockSpec((tm, d), lambda i: (i, 0)),
            pl.BlockSpec((1, d), lambda i: (0, 0)),
            pl.BlockSpec((2, d, 128), lambda i: (0, 0, 0)),
        ],
        out_specs=[
            pl.BlockSpec((tm * ROW_SUB, 128), lambda i: (i, 0)),
            pl.BlockSpec((tm, 128), lambda i: (i, 0)),
            pl.BlockSpec((1, 128), lambda i: (0, 0)),
        ],
        out_shape=[
            jax.ShapeDtypeStruct((s * ROW_SUB, 128), U32),
            jax.ShapeDtypeStruct((s, 128), F32),
            jax.ShapeDtypeStruct((1, 128), F32),
        ],
        scratch_shapes=[pltpu.VMEM((1, 128), F32)],
        compiler_params=_params(("arbitrary",)),
        name="router",
    )(x, g, wr)


def _row_tile(ref, r):
    return ref.at[pl.ds(pl.multiple_of(r * ROW_SUB, ROW_SUB), ROW_SUB), :]


def _dispatch_kernel(dest_ref, hp_hbm, xs_in, xs_out, stage, in_sem, out_sem, *, tm):
    del xs_in
    i = pl.program_id(0)
    last = pl.num_programs(0) - 1
    block_rows = tm * ROW_SUB

    def fetch(step):
        src = hp_hbm.at[pl.ds(pl.multiple_of(step * block_rows, block_rows), block_rows), :]
        return pltpu.make_async_copy(src, stage.at[step % 3], in_sem.at[step % 3])

    def drain(step):
        for _ in range(TOP_K):
            pltpu.make_async_copy(stage.at[step % 3], xs_out.at[pl.ds(0, block_rows), :],
                                  out_sem.at[step % 2]).wait()

    @pl.when(i == 0)
    def _():
        fetch(0).start()

    @pl.when(i < last)
    def _():
        fetch(i + 1).start()

    fetch(i).wait()
    src = stage.at[i % 3]

    def issue(t, c):
        for k in range(TOP_K):
            pltpu.make_async_copy(_row_tile(src, t), _row_tile(xs_out, dest_ref[2 * (i * tm + t) + k]),
                                  out_sem.at[i % 2]).start()
        return c

    lax.fori_loop(0, tm, issue, 0)

    @pl.when(i > 0)
    def _():
        drain(i - 1)

    @pl.when(i == last)
    def _():
        drain(i)


def dispatch(dest, hp, n_rows, *, tm):
    s = hp.shape[0] // ROW_SUB
    xs0 = jnp.zeros((n_rows * ROW_SUB, 128), U32)
    return pl.pallas_call(
        functools.partial(_dispatch_kernel, tm=tm),
        grid_spec=pltpu.PrefetchScalarGridSpec(
            num_scalar_prefetch=1,
            grid=(s // tm,),
            in_specs=[pl.BlockSpec(memory_space=pl.ANY), pl.BlockSpec(memory_space=pl.ANY)],
            out_specs=pl.BlockSpec(memory_space=pl.ANY),
            scratch_shapes=[pltpu.VMEM((3, tm * ROW_SUB, 128), U32),
                            pltpu.SemaphoreType.DMA((3,)), pltpu.SemaphoreType.DMA((2,))],
        ),
        out_shape=jax.ShapeDtypeStruct(xs0.shape, U32),
        input_output_aliases={2: 0},
        compiler_params=_params(("arbitrary",)),
        name="moe_dispatch",
    )(dest, hp, xs0)


def _moe_ffn_kernel(te_ref, nu_ref, fill_ref, xs_ref, wg_ref, wu_ref, wd_ref, ys_ref, xb, acc, *, tm):
    del te_ref, nu_ref
    f = pl.program_id(1)
    fill = fill_ref[pl.program_id(0)]
    used = fill > 0
    last = f == pl.num_programs(1) - 1

    @pl.when(jnp.logical_and(used, f == 0))
    def _():
        for c, v in enumerate(_unpack_rows(xs_ref, tm)):
            xb[:, c * 128:(c + 1) * 128] = v.astype(BF16)
        acc[...] = jnp.zeros(acc.shape, F32)

    for groups in range(1, tm // MOE_ROW_GROUP + 1):
        @pl.when(fill == groups)
        def _(rows=groups * MOE_ROW_GROUP):
            acc[0:rows, :] += _swiglu_step(xb[0:rows, :], wg_ref[0], wu_ref[0], wd_ref[0])

    @pl.when(jnp.logical_and(used, last))
    def _():
        _pack_rows(acc[...], ys_ref)

    @pl.when(jnp.logical_and(jnp.logical_not(used), last))
    def _():
        ys_ref[...] = jnp.zeros(ys_ref.shape, U32)


def moe_ffn(tile_expert, n_used, tile_fill, xs, wg, wu, wd, *, tm, tf):
    nt = tile_expert.shape[0]
    _, d, fdim = wg.shape
    nf = fdim // tf

    def f_idx(j, f, nu):
        return jnp.where(j < nu[0], f, nf - 1)

    return pl.pallas_call(
        functools.partial(_moe_ffn_kernel, tm=tm),
        grid_spec=pltpu.PrefetchScalarGridSpec(
            num_scalar_prefetch=3,
            grid=(nt, nf),
            in_specs=[
                pl.BlockSpec((tm * ROW_SUB, 128), lambda j, f, te, nu, fl: (jnp.minimum(j, nu[0] - 1), 0)),
                pl.BlockSpec((1, d, tf), lambda j, f, te, nu, fl: (te[j], 0, f_idx(j, f, nu))),
                pl.BlockSpec((1, d, tf), lambda j, f, te, nu, fl: (te[j], 0, f_idx(j, f, nu))),
                pl.BlockSpec((1, tf, d), lambda j, f, te, nu, fl: (te[j], f_idx(j, f, nu), 0)),
            ],
            out_specs=pl.BlockSpec((tm * ROW_SUB, 128), lambda j, f, te, nu, fl: (j, 0)),
            scratch_shapes=[pltpu.VMEM((tm, d), BF16), pltpu.VMEM((tm, d), F32)],
        ),
        out_shape=jax.ShapeDtypeStruct(xs.shape, U32),
        compiler_params=_params(("parallel", "arbitrary")),
        name="moe_ffn",
    )(tile_expert, n_used, tile_fill, xs, wg, wu, wd)


def _combine_kernel(dest_ref, x_ref, meta_ref, ys_hbm, o_ref, buf, sem, *, tm):
    i = pl.program_id(0)

    def issue_step(step, slot):
        def issue(t, c):
            for k in range(TOP_K):
                pltpu.make_async_copy(_row_tile(ys_hbm, dest_ref[2 * (step * tm + t) + k]),
                                      _row_tile(buf.at[slot, k], t), sem.at[slot]).start()
            return c
        lax.fori_loop(0, tm, issue, 0)

    @pl.when(i == 0)
    def _():
        issue_step(0, 0)

    @pl.when(i + 1 < pl.num_programs(0))
    def _():
        issue_step(i + 1, (i + 1) % 2)

    slot = i % 2
    for k in range(TOP_K):
        pltpu.make_async_copy(ys_hbm.at[pl.ds(0, tm * ROW_SUB), :], buf.at[slot, k], sem.at[slot]).wait()
    meta = meta_ref[...]
    g1 = meta[:, META_G1:META_G1 + 1]
    g2 = meta[:, META_G2:META_G2 + 1]
    y1 = _unpack_rows(buf.at[slot, 0], tm)
    y2 = _unpack_rows(buf.at[slot, 1], tm)
    for c in range(len(y1)):
        sl = slice(c * 128, (c + 1) * 128)
        o_ref[:, sl] = x_ref[:, sl] + g1 * y1[c] + g2 * y2[c]


def combine(dest, x, meta, ys, *, tm):
    s, d = x.shape
    return pl.pallas_call(
        functools.partial(_combine_kernel, tm=tm),
        grid_spec=pltpu.PrefetchScalarGridSpec(
            num_scalar_prefetch=1,
            grid=(s // tm,),
            in_specs=[
                pl.BlockSpec((tm, d), lambda i, dest: (i, 0)),
                pl.BlockSpec((tm, 128), lambda i, dest: (i, 0)),
                pl.BlockSpec(memory_space=pl.ANY),
            ],
            out_specs=pl.BlockSpec((tm, d), lambda i, dest: (i, 0)),
            scratch_shapes=[pltpu.VMEM((2, TOP_K, tm * ROW_SUB, 128), U32), pltpu.SemaphoreType.DMA((2,))],
        ),
        out_shape=jax.ShapeDtypeStruct((s, d), F32),
        compiler_params=_params(("arbitrary",)),
        name="moe_combine",
    )(dest, x, meta, ys)


def moe(x, g, w_router, wg, wu, wd, *, tm_rows, tf):
    s = x.shape[0]
    wr = _pad_cols(w_router.astype(F32), 128)
    wr_hi = wr.astype(BF16)
    wr_lo = (wr - wr_hi.astype(F32)).astype(BF16)
    hp, meta, cnt = router(x, g, jnp.stack([wr_hi, wr_lo]), tm=T.route_rows)
    i1, i2 = meta[:, META_I1].astype(jnp.int32), meta[:, META_I2].astype(jnp.int32)
    r1, r2 = meta[:, META_R1].astype(jnp.int32), meta[:, META_R2].astype(jnp.int32)
    counts = cnt[0, :N_EXPERTS].astype(jnp.int32)
    padded = (counts + tm_rows - 1) // tm_rows * tm_rows
    ends = jnp.cumsum(padded)
    offs = ends - padded
    eids = jnp.arange(N_EXPERTS, dtype=jnp.int32)
    off1 = jnp.sum(jnp.where(i1[:, None] == eids, offs, 0), axis=1)
    off2 = jnp.sum(jnp.where(i2[:, None] == eids, offs, 0), axis=1)
    dest = jnp.stack([off1 + r1, off2 + r2], axis=1).reshape(-1)
    n_tiles = (TOP_K * s) // tm_rows + N_EXPERTS
    n_used = (ends[-1] // tm_rows).reshape(1)
    tile_start = jnp.minimum(jnp.arange(n_tiles, dtype=jnp.int32), n_used - 1) * tm_rows
    tile_expert = jnp.sum(tile_start[:, None] >= ends[None, :], axis=1).astype(jnp.int32)
    tile_ids = jnp.arange(n_tiles, dtype=jnp.int32)
    rows_end = jnp.sum(jnp.where(tile_expert[:, None] == eids, offs + counts, 0), axis=1)
    tile_rows = jnp.where(tile_ids < n_used, jnp.clip(rows_end - tile_ids * tm_rows, 0, tm_rows), 0)
    tile_fill = ((tile_rows + MOE_ROW_GROUP - 1) // MOE_ROW_GROUP).astype(jnp.int32)
    xs = dispatch(dest, hp, n_tiles * tm_rows, tm=T.move_rows)
    ys = moe_ffn(tile_expert, n_used, tile_fill, xs, wg, wu, wd, tm=tm_rows, tf=tf)
    return combine(dest, x, meta, ys, tm=T.move_rows)


def _pad_cols(w, n):
    return jnp.pad(w, ((0, 0), (0, n - w.shape[1])))


def _mla_weights(w_q_b, w_kv_b):
    wq = w_q_b.reshape(Q_LORA, N_HEADS_B, QK_DIM_B)
    wq = jnp.pad(wq, ((0, 0), (0, 0), (0, QK_PAD_B - QK_DIM_B))).reshape(Q_LORA, N_HEADS_B * QK_PAD_B)
    wkv = w_kv_b.reshape(KV_LORA, N_HEADS_B, NOPE_DIM + V_DIM)
    wk = wkv[:, :, :NOPE_DIM].reshape(KV_LORA, N_HEADS_B * NOPE_DIM)
    wv = wkv[:, :, NOPE_DIM:].reshape(KV_LORA, N_HEADS_B * V_DIM)
    return wq.astype(BF16), wk.astype(BF16), wv.astype(BF16)


def _rope_tables(positions):
    half = ROPE_DIM // 2
    inv = ROPE_THETA ** (-jnp.arange(half, dtype=F32) / half)
    ang = positions.astype(F32)[:, None] * inv
    cos, sin = jnp.cos(ang), jnp.sin(ang)
    z = jnp.zeros_like(cos)
    c = jnp.concatenate([cos, cos, z, z], axis=-1)
    s1 = jnp.concatenate([-sin, z, z, z], axis=-1)
    s2 = jnp.concatenate([z, sin, z, z], axis=-1)
    return c, s1, s2


def kernel(x, positions, rel_bias_table, norm_mix_g, w_in, q_a_norm_g, kv_a_norm_g, w_q_b, w_kv_b, q_norm_a_g, k_norm_a_g, q_norm_b_g, k_norm_b_g, w_out, norm_ffn_g, w_ff_gate, w_ff_up, w_ff_down, w_router, w_exp_gate, w_exp_up, w_exp_down):
    batch, seq, d = x.shape
    depth = w_in.shape[0]
    outs = []
    bias = _dilated_bias(rel_bias_table)
    for bi in range(batch):
        xs = x.reshape(seq, d) if batch == 1 else x[bi]
        rope_c, rope_s1, rope_s2 = _rope_tables(positions[bi])
        for l in range(depth):
            w_main = w_in[l][:, :3 * WIDTH_A].astype(BF16)
            w_tail = _pad_cols(w_in[l][:, 3 * WIDTH_A:], PROJ_COLS - 3 * WIDTH_A).astype(BF16)
            proj = norm_matmul(xs, norm_mix_g[l][None], w_main, w_tail, tm=T.proj_rows, out_dtype=F32)
            a = dilated_mixer(proj, bias, q_norm_a_g[l][None], k_norm_a_g[l][None])
            wq, wk, wv = _mla_weights(w_q_b[l], w_kv_b[l])
            qg = _pad_cols(q_norm_b_g[l][None] * (QK_DIM_B ** -0.5 * LOG2E), QK_PAD_B)
            kg = _pad_cols(k_norm_b_g[l][None], QK_PAD_B)
            qb, kb, vb = mla_prep(proj, q_a_norm_g[l][None], kv_a_norm_g[l][None], wq, wk, wv,
                                  qg, kg, rope_c, rope_s1, rope_s2, tm=T.prep_rows)
            b = mla_flash(qb, kb, vb, tq=T.flash_q, tk=T.flash_k)
            w_out_l = w_out[l].astype(BF16)
            xs = out_proj(xs, a, b, w_out_l[:WIDTH_A], w_out_l[WIDTH_A:], tm=T.out_rows, tn=T.out_cols)
            gf = norm_ffn_g[l][None]
            if l % 2 == 0:
                i = l // 2
                xs = ffn(xs, gf, w_ff_gate[i], w_ff_up[i], w_ff_down[i], tm=T.ffn_rows, tf=T.ffn_cols)
            else:
                i = l // 2
                xs = moe(xs, gf, w_router[i], w_exp_gate[i], w_exp_up[i], w_exp_down[i],
                         tm_rows=T.ffn_rows, tf=T.ffn_cols)
        outs.append(xs)
    return outs[0].reshape(1, seq, d) if batch == 1 else jnp.stack(outs, axis=0)
```

```python
import functools
import math
from typing import NamedTuple

import jax
import jax.numpy as jnp
from jax import lax
from jax.experimental import pallas as pl
from jax.experimental.pallas import tpu as pltpu

F32 = jnp.float32
BF16 = jnp.bfloat16

D_MODEL = 2048
HEAD_DIM = 128
N_HEADS_A = 8
DILATED_PATTERNS = ((128, 1), (512, 4), (2048, 16))
BLK = 128
N_BUCKETS = 32
MAX_DISTANCE = 2048
N_HEADS_B = 8
Q_LORA = 512
KV_LORA = 256
NOPE_DIM = 128
ROPE_DIM = 64
V_DIM = 128
QK_DIM_B = NOPE_DIM + ROPE_DIM
QK_PAD_B = 256
VT_ROWS = V_DIM + 16
LOG2E = math.log2(math.e)
ROPE_THETA = 10000.0
WIDTH_A = N_HEADS_A * HEAD_DIM
WIDTH_B = N_HEADS_B * V_DIM
IN_COLS = 3 * WIDTH_A + Q_LORA + KV_LORA + ROPE_DIM
PROJ_COLS = 4096
D_FF = 5632
N_EXPERTS = 8
TOP_K = 2
ROW_SUB = 8
MOE_ROW_GROUP = 128
HALF = D_MODEL // 2
U32 = jnp.uint32
EPS = 1e-6
NEG = -1e30

SUPER = 2048
VMEM_LIMIT = 56 * 1024 * 1024


class _Tiles(NamedTuple):
    proj_rows: int = 1024
    prep_rows: int = 512
    flash_q: int = 2048
    flash_k: int = 512
    out_rows: int = 1024
    out_cols: int = 1024
    ffn_rows: int = 512
    ffn_cols: int = 512
    route_rows: int = 512
    move_rows: int = 256


T = _Tiles()


def _params(sem, vmem=VMEM_LIMIT):
    return pltpu.CompilerParams(dimension_semantics=sem, vmem_limit_bytes=vmem)


def _rms(x, g, n=None):
    ss = jnp.sum(x * x, axis=-1, keepdims=True)
    n = x.shape[-1] if n is None else n
    return x * lax.rsqrt(ss * (1.0 / n) + EPS) * g


def _norm_matmul_kernel(x_ref, g_ref, w_ref, wt_ref, hg_ref, o_ref, hn_ref, *, n_main, n_headnorm):
    j = pl.program_id(1)

    @pl.when(j == 0)
    def _():
        hn_ref[...] = _rms(x_ref[...], g_ref[...]).astype(BF16)

    @pl.when(j < n_headnorm)
    def _():
        acc = jnp.dot(hn_ref[...], w_ref[...], preferred_element_type=F32)
        for c in range(acc.shape[1] // HEAD_DIM):
            sl = slice(c * HEAD_DIM, (c + 1) * HEAD_DIM)
            o_ref[:, sl] = _rms(acc[:, sl], hg_ref[:, sl]).astype(o_ref.dtype)

    @pl.when(jnp.logical_and(j >= n_headnorm, j < n_main))
    def _():
        o_ref[...] = jnp.dot(hn_ref[...], w_ref[...], preferred_element_type=F32).astype(o_ref.dtype)

    @pl.when(j == n_main)
    def _():
        o_ref[...] = jnp.dot(hn_ref[...], wt_ref[...], preferred_element_type=F32).astype(o_ref.dtype)


def norm_matmul(x, g, w, w_tail, head_g, *, tm, out_dtype):
    s, k = x.shape
    tn = w_tail.shape[1]
    n_main = w.shape[1] // tn
    n_headnorm = head_g.shape[1] // tn
    return pl.pallas_call(
        functools.partial(_norm_matmul_kernel, n_main=n_main, n_headnorm=n_headnorm),
        grid=(s // tm, n_main + 1),
        in_specs=[
            pl.BlockSpec((tm, k), lambda i, j: (i, 0)),
            pl.BlockSpec((1, k), lambda i, j: (0, 0)),
            pl.BlockSpec((k, tn), lambda i, j: (0, jnp.minimum(j, n_main - 1))),
            pl.BlockSpec((k, tn), lambda i, j: (0, 0)),
            pl.BlockSpec((1, tn), lambda i, j: (0, jnp.minimum(j, n_headnorm - 1))),
        ],
        out_specs=pl.BlockSpec((tm, tn), lambda i, j: (i, j)),
        out_shape=jax.ShapeDtypeStruct((s, (n_main + 1) * tn), out_dtype),
        scratch_shapes=[pltpu.VMEM((tm, k), BF16)],
        compiler_params=_params(("parallel", "arbitrary")),
        name="norm_matmul",
    )(x, g, w, w_tail, head_g)


def _t5_bucket(n):
    max_exact = N_BUCKETS // 2
    nf = jnp.maximum(n, 1).astype(F32)
    large = max_exact + (jnp.log(nf / max_exact) / math.log(MAX_DISTANCE / max_exact)
                         * (N_BUCKETS - max_exact)).astype(jnp.int32)
    large = jnp.minimum(large, N_BUCKETS - 1)
    return jnp.where(n < max_exact, n, large)


def _dilated_bias(rel_bias):
    i = jnp.arange(BLK)[:, None]
    j = jnp.arange(2 * BLK)[None, :]
    delta = i + BLK - j
    out = []
    for window, dilation in DILATED_PATTERNS:
        band = (delta >= 0) & (delta <= window // dilation)
        bucket = _t5_bucket(jnp.maximum(delta, 0) * dilation)
        onehot = (bucket[None] == jnp.arange(N_BUCKETS)[:, None, None]).astype(F32)
        bias = jnp.einsum('nij,nh->hij', onehot, rel_bias.astype(F32), precision=lax.Precision.HIGHEST)
        out.append(jnp.where(band[None], bias * LOG2E, NEG))
    return jnp.stack(out, axis=0)


def _dilated_kernel(q_ref, kp_ref, kc_ref, vp_ref, vc_ref, bias_ref, o_ref, kn_s, v_s, op_s, lse_s):
    sb = pl.program_id(1)
    kn_s[0:SUPER, :] = kp_ref[...]
    kn_s[SUPER:2 * SUPER, :] = kc_ref[...]
    v_s[0:SUPER, :] = vp_ref[...]
    v_s[SUPER:2 * SUPER, :] = vc_ref[...]
    col = lax.broadcasted_iota(jnp.int32, (BLK, 2 * BLK), 1)
    no_prev = jnp.logical_and(sb == 0, col < BLK)

    for pi, (_, d) in enumerate(DILATED_PATTERNS):
        def rows(start, size, d=d):
            return pl.ds(start, size) if d == 1 else pl.ds(start, size, stride=d)

        bias = bias_ref[pi, 0]
        bias_first = jnp.where(no_prev, NEG, bias)
        for r in range(d):
            for b in range(SUPER // (d * BLK)):
                q0 = r + d * BLK * b
                k0 = SUPER + q0 - d * BLK
                qb = q_ref[rows(q0, BLK), :].astype(BF16)
                k2 = kn_s[rows(k0, 2 * BLK), :].astype(BF16)
                v2 = v_s[rows(k0, 2 * BLK), :].astype(BF16)
                s = lax.dot_general(qb, k2, (((1,), (1,)), ((), ())), preferred_element_type=F32)
                s = s + (bias_first if b == 0 else bias)
                m = jnp.max(s, axis=-1, keepdims=True)
                p = jnp.exp2(s - m)
                den = jnp.sum(p, axis=-1, keepdims=True)
                o = jnp.dot(p.astype(BF16), v2, preferred_element_type=F32) / den
                lse = m + jnp.log2(den)
                op_s[pi, rows(q0, BLK), :] = o
                lse_s[pi, rows(q0, BLK), :] = jnp.broadcast_to(lse, (BLK, HEAD_DIM))

    chunk = 256
    for c in range(SUPER // chunk):
        sl = slice(c * chunk, (c + 1) * chunk)
        l0, l1, l2 = lse_s[0, sl, :], lse_s[1, sl, :], lse_s[2, sl, :]
        mx = jnp.maximum(jnp.maximum(l0, l1), l2)
        e0, e1, e2 = jnp.exp2(l0 - mx), jnp.exp2(l1 - mx), jnp.exp2(l2 - mx)
        num = e0 * op_s[0, sl, :] + e1 * op_s[1, sl, :] + e2 * op_s[2, sl, :]
        o_ref[sl, :] = (num / (e0 + e1 + e2)).astype(o_ref.dtype)


def dilated_mixer(proj, bias):
    s = proj.shape[0]
    nsb = s // SUPER
    h8 = N_HEADS_A
    blk = (SUPER, HEAD_DIM)
    return pl.pallas_call(
        _dilated_kernel,
        grid=(h8, nsb),
        in_specs=[
            pl.BlockSpec(blk, lambda h, i: (i, h)),
            pl.BlockSpec(blk, lambda h, i: (jnp.maximum(i - 1, 0), h8 + h)),
            pl.BlockSpec(blk, lambda h, i: (i, h8 + h)),
            pl.BlockSpec(blk, lambda h, i: (jnp.maximum(i - 1, 0), 2 * h8 + h)),
            pl.BlockSpec(blk, lambda h, i: (i, 2 * h8 + h)),
            pl.BlockSpec((len(DILATED_PATTERNS), 1, BLK, 2 * BLK), lambda h, i: (0, h, 0, 0)),
        ],
        out_specs=pl.BlockSpec(blk, lambda h, i: (i, h)),
        out_shape=jax.ShapeDtypeStruct((s, WIDTH_A), BF16),
        scratch_shapes=[
            pltpu.VMEM((2 * SUPER, HEAD_DIM), F32),
            pltpu.VMEM((2 * SUPER, HEAD_DIM), F32),
            pltpu.VMEM((len(DILATED_PATTERNS), SUPER, HEAD_DIM), F32),
            pltpu.VMEM((len(DILATED_PATTERNS), SUPER, HEAD_DIM), F32),
        ],
        compiler_params=_params(("parallel", "arbitrary")),
        name="dilated_mixer",
    )(proj, proj, proj, proj, proj, bias)


def _mla_prep_kernel(cq_ref, ckv_ref, kr_ref, qag_ref, kvag_ref, wq_ref, wk_ref, wv_ref,
                     qg_ref, kg_ref, c_ref, s1_ref, s2_ref, q_out, k_out, v_out):
    c, s1, s2 = c_ref[...], s1_ref[...], s2_ref[...]

    def rope(x):
        return x * c + pltpu.roll(x, 96, 1) * s1 + pltpu.roll(x, 32, 1) * s2

    cqn = _rms(cq_ref[...], qag_ref[...]).astype(BF16)
    ckvn = _rms(ckv_ref[...], kvag_ref[...]).astype(BF16)
    qpre = jnp.dot(cqn, wq_ref[...], preferred_element_type=F32)
    knope = jnp.dot(ckvn, wk_ref[...], preferred_element_type=F32)
    v = jnp.dot(ckvn, wv_ref[...], preferred_element_type=F32)
    kr = kr_ref[...]
    kr_ss = jnp.sum(kr * kr, axis=-1, keepdims=True)
    qg, kg = qg_ref[...], kg_ref[...]
    kr_roped = rope(kr * kg[:, NOPE_DIM:])
    inv_n = 1.0 / QK_DIM_B
    ones = jnp.ones((VT_ROWS - V_DIM, cq_ref.shape[0]), BF16)
    for h in range(N_HEADS_B):
        qh = qpre[:, h * QK_PAD_B:(h + 1) * QK_PAD_B]
        rs = lax.rsqrt(jnp.sum(qh * qh, axis=-1, keepdims=True) * inv_n + EPS)
        q_out[h, 0:NOPE_DIM, :] = (qh[:, :NOPE_DIM] * rs * qg[:, :NOPE_DIM]).T.astype(BF16)
        q_out[h, NOPE_DIM:QK_PAD_B, :] = rope(qh[:, NOPE_DIM:] * rs * qg[:, NOPE_DIM:]).T.astype(BF16)
        kh = knope[:, h * NOPE_DIM:(h + 1) * NOPE_DIM]
        rs = lax.rsqrt((jnp.sum(kh * kh, axis=-1, keepdims=True) + kr_ss) * inv_n + EPS)
        k_out[h, :, 0:NOPE_DIM] = (kh * rs * kg[:, :NOPE_DIM]).astype(BF16)
        k_out[h, :, NOPE_DIM:QK_PAD_B] = (kr_roped * rs).astype(BF16)
        v_out[h, 0:V_DIM, :] = v[:, h * V_DIM:(h + 1) * V_DIM].T.astype(BF16)
        v_out[h, V_DIM:VT_ROWS, :] = ones


def mla_prep(proj, q_a_g, kv_a_g, wq, wk, wv, qg, kg, rope_c, rope_s1, rope_s2, *, tm):
    s = proj.shape[0]
    hb = N_HEADS_B
    full = lambda shape: pl.BlockSpec(shape, lambda i: (0,) * len(shape))
    cq_blk = (3 * WIDTH_A) // Q_LORA
    ckv_blk = (3 * WIDTH_A + Q_LORA) // KV_LORA
    kr_blk = (3 * WIDTH_A + Q_LORA + KV_LORA) // 128
    return pl.pallas_call(
        _mla_prep_kernel,
        grid=(s // tm,),
        in_specs=[
            pl.BlockSpec((tm, Q_LORA), lambda i: (i, cq_blk)),
            pl.BlockSpec((tm, KV_LORA), lambda i: (i, ckv_blk)),
            pl.BlockSpec((tm, 128), lambda i: (i, kr_blk)),
            full((1, Q_LORA)), full((1, KV_LORA)),
            full(wq.shape), full(wk.shape), full(wv.shape),
            full((1, QK_PAD_B)), full((1, QK_PAD_B)),
            pl.BlockSpec((tm, 128), lambda i: (i, 0)),
            pl.BlockSpec((tm, 128), lambda i: (i, 0)),
            pl.BlockSpec((tm, 128), lambda i: (i, 0)),
        ],
        out_specs=[
            pl.BlockSpec((hb, QK_PAD_B, tm), lambda i: (0, 0, i)),
            pl.BlockSpec((hb, tm, QK_PAD_B), lambda i: (0, i, 0)),
            pl.BlockSpec((hb, VT_ROWS, tm), lambda i: (0, 0, i)),
        ],
        out_shape=[
            jax.ShapeDtypeStruct((hb, QK_PAD_B, s), BF16),
            jax.ShapeDtypeStruct((hb, s, QK_PAD_B), BF16),
            jax.ShapeDtypeStruct((hb, VT_ROWS, s), BF16),
        ],
        compiler_params=_params(("parallel",)),
        name="mla_prep",
    )(proj, proj, proj, q_a_g, kv_a_g, wq, wk, wv, qg, kg, rope_c, rope_s1, rope_s2)


def _flash_kernel(qt_ref, k_ref, vt_ref, o_ref, s_a, s_b, m_s, acc_s, *, tq, tk):
    i = pl.program_id(1)
    m_s[...] = jnp.full(m_s.shape, NEG, F32)
    acc_s[...] = jnp.zeros(acc_s.shape, F32)
    per_tile = tq // tk
    assert per_tile % 2 == 0

    def compute(c, dst, q_lo=0):
        start = pl.multiple_of(c * tk, tk)
        dst[:, q_lo:] = jnp.dot(k_ref[0, pl.ds(start, tk), :], qt_ref[0, :, q_lo:],
                                preferred_element_type=F32)

    def process(c, src, q_lo=0, masked=False):
        s = src[:, q_lo:]
        if masked:
            key = c * tk + lax.broadcasted_iota(jnp.int32, s.shape, 0)
            qry = i * tq + q_lo + lax.broadcasted_iota(jnp.int32, s.shape, 1)
            s = jnp.where(key <= qry, s, NEG)
        m_prev = m_s[:, q_lo:]
        m_new = jnp.maximum(m_prev, jnp.max(s, axis=0, keepdims=True))
        alpha = jnp.exp2(m_prev - m_new)
        p = jnp.exp2(s - m_new).astype(BF16)
        start = pl.multiple_of(c * tk, tk)
        pv = jnp.dot(vt_ref[0, :, pl.ds(start, tk)], p, preferred_element_type=F32)
        acc_s[:, q_lo:] = alpha * acc_s[:, q_lo:] + pv
        m_s[:, q_lo:] = m_new

    bufs = (s_a, s_b)
    compute(0, s_a)
    n_full = per_tile * i

    def trip(t, carry):
        for u in range(per_tile):
            compute(t * per_tile + u + 1, bufs[(u + 1) % 2])
            process(t * per_tile + u, bufs[u % 2])
        return carry

    lax.fori_loop(0, i, trip, 0)
    for d in range(per_tile):
        if d + 1 < per_tile:
            compute(n_full + d + 1, bufs[(d + 1) % 2], (d + 1) * tk)
        process(n_full + d, bufs[d % 2], d * tk, masked=True)
    acc = acc_s[...]
    o_t = acc[0:V_DIM, :] / acc[V_DIM:V_DIM + 1, :]
    o_ref[...] = o_t.T.astype(o_ref.dtype)


def mla_flash(qt, k, vt, *, tq, tk):
    hb, s, _ = k.shape
    return pl.pallas_call(
        functools.partial(_flash_kernel, tq=tq, tk=tk),
        grid=(hb, s // tq),
        in_specs=[
            pl.BlockSpec((1, QK_PAD_B, tq), lambda h, i: (h, 0, i)),
            pl.BlockSpec((1, s, QK_PAD_B), lambda h, i: (h, 0, 0)),
            pl.BlockSpec((1, VT_ROWS, s), lambda h, i: (h, 0, 0)),
        ],
        out_specs=pl.BlockSpec((tq, V_DIM), lambda h, i: (i, h)),
        out_shape=jax.ShapeDtypeStruct((s, WIDTH_B), BF16),
        scratch_shapes=[
            pltpu.VMEM((tk, tq), F32),
            pltpu.VMEM((tk, tq), F32),
            pltpu.VMEM((1, tq), F32),
            pltpu.VMEM((VT_ROWS, tq), F32),
        ],
        compiler_params=_params(("parallel", "arbitrary")),
        name="mla_flash",
    )(qt, k, vt)


def _out_proj_kernel(x_ref, a_ref, b_ref, wa_ref, wb_ref, o_ref):
    acc = jnp.dot(a_ref[...], wa_ref[...], preferred_element_type=F32)
    acc = acc + jnp.dot(b_ref[...], wb_ref[...], preferred_element_type=F32)
    o_ref[...] = x_ref[...] + acc


def out_proj(x, a, b, wa, wb, *, tm, tn):
    s, n = x.shape
    ka, kb = a.shape[1], b.shape[1]
    return pl.pallas_call(
        _out_proj_kernel,
        grid=(s // tm, n // tn),
        in_specs=[
            pl.BlockSpec((tm, tn), lambda i, j: (i, j)),
            pl.BlockSpec((tm, ka), lambda i, j: (i, 0)),
            pl.BlockSpec((tm, kb), lambda i, j: (i, 0)),
            pl.BlockSpec((ka, tn), lambda i, j: (0, j)),
            pl.BlockSpec((kb, tn), lambda i, j: (0, j)),
        ],
        out_specs=pl.BlockSpec((tm, tn), lambda i, j: (i, j)),
        out_shape=jax.ShapeDtypeStruct((s, n), F32),
        compiler_params=_params(("parallel", "arbitrary")),
        name="out_proj",
    )(x, a, b, wa, wb)


def _swiglu_step(h, wg, wu, wd):
    gate = jnp.dot(h, wg, preferred_element_type=F32)
    up = jnp.dot(h, wu, preferred_element_type=F32)
    act = gate * jax.nn.sigmoid(gate) * up
    return jnp.dot(act.astype(BF16), wd, preferred_element_type=F32)


def _ffn_kernel(x_ref, g_ref, wg_ref, wu_ref, wd_ref, o_ref, hn_ref):
    @pl.when(pl.program_id(1) == 0)
    def _():
        hn_ref[...] = _rms(x_ref[...], g_ref[...]).astype(BF16)
        o_ref[...] = x_ref[...]

    o_ref[...] += _swiglu_step(hn_ref[...], wg_ref[...], wu_ref[...], wd_ref[...])


def ffn(x, g, wg, wu, wd, *, tm, tf):
    s, d = x.shape
    fdim = wg.shape[1]
    return pl.pallas_call(
        _ffn_kernel,
        grid=(s // tm, fdim // tf),
        in_specs=[
            pl.BlockSpec((tm, d), lambda i, f: (i, 0)),
            pl.BlockSpec((1, d), lambda i, f: (0, 0)),
            pl.BlockSpec((d, tf), lambda i, f: (0, f)),
            pl.BlockSpec((d, tf), lambda i, f: (0, f)),
            pl.BlockSpec((tf, d), lambda i, f: (f, 0)),
        ],
        out_specs=pl.BlockSpec((tm, d), lambda i, f: (i, 0)),
        out_shape=jax.ShapeDtypeStruct((s, d), F32),
        scratch_shapes=[pltpu.VMEM((tm, d), BF16)],
        compiler_params=_params(("parallel", "arbitrary")),
        name="ffn",
    )(x, g, wg, wu, wd)


def _pack_rows(y, dst):
    n = y.shape[0]
    for s in range(ROW_SUB):
        lo = y[:, s * 128:(s + 1) * 128].astype(BF16).astype(F32)
        hi = y[:, HALF + s * 128:HALF + (s + 1) * 128].astype(BF16).astype(F32)
        w = (lax.bitcast_convert_type(lo, U32) >> 16) | lax.bitcast_convert_type(hi, U32)
        dst[pl.ds(s, n, stride=ROW_SUB), :] = w


def _unpack_rows(src, n):
    lo, hi = [], []
    for s in range(ROW_SUB):
        w = src[pl.ds(s, n, stride=ROW_SUB), :]
        lo.append(lax.bitcast_convert_type(w << 16, F32))
        hi.append(lax.bitcast_convert_type(w & jnp.uint32(0xFFFF0000), F32))
    return lo + hi


META_I1, META_I2, META_R1, META_R2, META_G1, META_G2 = range(6)


def _router_kernel(x_ref, g_ref, wr_ref, hp_ref, meta_ref, cnt_ref, carry):
    @pl.when(pl.program_id(0) == 0)
    def _():
        carry[...] = jnp.zeros(carry.shape, F32)

    hn = _rms(x_ref[...], g_ref[...])
    _pack_rows(hn, hp_ref)
    h_hi = hn.astype(BF16)
    h_lo = (hn - h_hi.astype(F32)).astype(BF16)
    w_hi, w_lo = wr_ref[0], wr_ref[1]
    logits = (jnp.dot(h_hi, w_hi, preferred_element_type=F32)
              + jnp.dot(h_hi, w_lo, preferred_element_type=F32)
              + jnp.dot(h_lo, w_hi, preferred_element_type=F32))
    lane = lax.broadcasted_iota(jnp.int32, logits.shape, 1)
    logits = jnp.where(lane < N_EXPERTS, logits, NEG)
    v1 = jnp.max(logits, axis=-1, keepdims=True)
    i1 = jnp.min(jnp.where(logits == v1, lane, 128), axis=-1, keepdims=True)
    rest = jnp.where(lane == i1, NEG, logits)
    v2 = jnp.max(rest, axis=-1, keepdims=True)
    i2 = jnp.min(jnp.where(rest == v2, lane, 128), axis=-1, keepdims=True)
    e2 = jnp.exp(v2 - v1)
    g1 = 1.0 / (1.0 + e2)
    g2 = e2 / (1.0 + e2)

    tm = hn.shape[0]
    member = jnp.where(jnp.logical_or(lane == i1, lane == i2), 1.0, 0.0)
    row = lax.broadcasted_iota(jnp.int32, (tm, tm), 0)
    col = lax.broadcasted_iota(jnp.int32, (tm, tm), 1)
    earlier = jnp.where(col < row, 1.0, 0.0).astype(BF16)
    rank = carry[...] + jnp.dot(earlier, member.astype(BF16), preferred_element_type=F32)
    r1 = jnp.sum(jnp.where(lane == i1, rank, 0.0), axis=-1, keepdims=True)
    r2 = jnp.sum(jnp.where(lane == i2, rank, 0.0), axis=-1, keepdims=True)
    carry[...] += jnp.sum(member, axis=0, keepdims=True)
    cnt_ref[...] = carry[...]
    meta = jnp.zeros(logits.shape, F32)
    for pos, val in ((META_I1, i1.astype(F32)), (META_I2, i2.astype(F32)), (META_R1, r1),
                     (META_R2, r2), (META_G1, g1), (META_G2, g2)):
        meta = jnp.where(lane == pos, val, meta)
    meta_ref[...] = meta


def router(x, g, wr, *, tm):
    s, d = x.shape
    return pl.pallas_call(
        _router_kernel,
        grid=(s // tm,),
        in_specs=[
            pl.BlockSpec((tm, d), lambda i: (i, 0)),
            pl.BlockSpec((1, d), lambda i: (0, 0)),
            pl.BlockSpec((2, d, 128), lambda i: (0, 0, 0)),
        ],
        out_specs=[
            pl.BlockSpec((tm * ROW_SUB, 128), lambda i: (i, 0)),
            pl.BlockSpec((tm, 128), lambda i: (i, 0)),
            pl.BlockSpec((1, 128), lambda i: (0, 0)),
        ],
        out_shape=[
            jax.ShapeDtypeStruct((s * ROW_SUB, 128), U32),
            jax.ShapeDtypeStruct((s, 128), F32),
            jax.ShapeDtypeStruct((1, 128), F32),
        ],
        scratch_shapes=[pltpu.VMEM((1, 128), F32)],
        compiler_params=_params(("arbitrary",)),
        name="router",
    )(x, g, wr)


def _row_tile(ref, r):
    return ref.at[pl.ds(pl.multiple_of(r * ROW_SUB, ROW_SUB), ROW_SUB), :]


def _dispatch_kernel(dest_ref, hp_hbm, xs_in, xs_out, stage, in_sem, out_sem, *, tm):
    del xs_in
    i = pl.program_id(0)
    last = pl.num_programs(0) - 1
    block_rows = tm * ROW_SUB

    def fetch(step):
        src = hp_hbm.at[pl.ds(pl.multiple_of(step * block_rows, block_rows), block_rows), :]
        return pltpu.make_async_copy(src, stage.at[step % 3], in_sem.at[step % 3])

    def drain(step):
        for _ in range(TOP_K):
            pltpu.make_async_copy(stage.at[step % 3], xs_out.at[pl.ds(0, block_rows), :],
                                  out_sem.at[step % 2]).wait()

    @pl.when(i == 0)
    def _():
        fetch(0).start()

    @pl.when(i < last)
    def _():
        fetch(i + 1).start()

    fetch(i).wait()
    src = stage.at[i % 3]

    def issue(t, c):
        for k in range(TOP_K):
            pltpu.make_async_copy(_row_tile(src, t), _row_tile(xs_out, dest_ref[2 * (i * tm + t) + k]),
                                  out_sem.at[i % 2]).start()
        return c

    lax.fori_loop(0, tm, issue, 0)

    @pl.when(i > 0)
    def _():
        drain(i - 1)

    @pl.when(i == last)
    def _():
        drain(i)


def dispatch(dest, hp, n_rows, *, tm):
    s = hp.shape[0] // ROW_SUB
    xs0 = jnp.zeros((n_rows * ROW_SUB, 128), U32)
    return pl.pallas_call(
        functools.partial(_dispatch_kernel, tm=tm),
        grid_spec=pltpu.PrefetchScalarGridSpec(
            num_scalar_prefetch=1,
            grid=(s // tm,),
            in_specs=[pl.BlockSpec(memory_space=pl.ANY), pl.BlockSpec(memory_space=pl.ANY)],
            out_specs=pl.BlockSpec(memory_space=pl.ANY),
            scratch_shapes=[pltpu.VMEM((3, tm * ROW_SUB, 128), U32),
                            pltpu.SemaphoreType.DMA((3,)), pltpu.SemaphoreType.DMA((2,))],
        ),
        out_shape=jax.ShapeDtypeStruct(xs0.shape, U32),
        input_output_aliases={2: 0},
        compiler_params=_params(("arbitrary",)),
        name="moe_dispatch",
    )(dest, hp, xs0)


def _moe_ffn_kernel(te_ref, nu_ref, fill_ref, xs_ref, wg_ref, wu_ref, wd_ref, ys_ref, xb, acc, *, tm):
    del te_ref, nu_ref
    f = pl.program_id(1)
    fill = fill_ref[pl.program_id(0)]
    used = fill > 0
    last = f == pl.num_programs(1) - 1

    @pl.when(jnp.logical_and(used, f == 0))
    def _():
        for c, v in enumerate(_unpack_rows(xs_ref, tm)):
            xb[:, c * 128:(c + 1) * 128] = v.astype(BF16)
        acc[...] = jnp.zeros(acc.shape, F32)

    for groups in range(1, tm // MOE_ROW_GROUP + 1):
        @pl.when(fill == groups)
        def _(rows=groups * MOE_ROW_GROUP):
            acc[0:rows, :] += _swiglu_step(xb[0:rows, :], wg_ref[0], wu_ref[0], wd_ref[0])

    @pl.when(jnp.logical_and(used, last))
    def _():
        _pack_rows(acc[...], ys_ref)

    @pl.when(jnp.logical_and(jnp.logical_not(used), last))
    def _():
        ys_ref[...] = jnp.zeros(ys_ref.shape, U32)


def moe_ffn(tile_expert, n_used, tile_fill, xs, wg, wu, wd, *, tm, tf):
    nt = tile_expert.shape[0]
    _, d, fdim = wg.shape
    nf = fdim // tf

    def f_idx(j, f, nu):
        return jnp.where(j < nu[0], f, nf - 1)

    return pl.pallas_call(
        functools.partial(_moe_ffn_kernel, tm=tm),
        grid_spec=pltpu.PrefetchScalarGridSpec(
            num_scalar_prefetch=3,
            grid=(nt, nf),
            in_specs=[
                pl.BlockSpec((tm * ROW_SUB, 128), lambda j, f, te, nu, fl: (jnp.minimum(j, nu[0] - 1), 0)),
                pl.BlockSpec((1, d, tf), lambda j, f, te, nu, fl: (te[j], 0, f_idx(j, f, nu))),
                pl.BlockSpec((1, d, tf), lambda j, f, te, nu, fl: (te[j], 0, f_idx(j, f, nu))),
                pl.BlockSpec((1, tf, d), lambda j, f, te, nu, fl: (te[j], f_idx(j, f, nu), 0)),
            ],
            out_specs=pl.BlockSpec((tm * ROW_SUB, 128), lambda j, f, te, nu, fl: (j, 0)),
            scratch_shapes=[pltpu.VMEM((tm, d), BF16), pltpu.VMEM((tm, d), F32)],
        ),
        out_shape=jax.ShapeDtypeStruct(xs.shape, U32),
        compiler_params=_params(("parallel", "arbitrary")),
        name="moe_ffn",
    )(tile_expert, n_used, tile_fill, xs, wg, wu, wd)


def _combine_kernel(dest_ref, x_ref, meta_ref, ys_hbm, o_ref, buf, sem, *, tm):
    i = pl.program_id(0)

    def issue_step(step, slot):
        def issue(t, c):
            for k in range(TOP_K):
                pltpu.make_async_copy(_row_tile(ys_hbm, dest_ref[2 * (step * tm + t) + k]),
                                      _row_tile(buf.at[slot, k], t), sem.at[slot]).start()
            return c
        lax.fori_loop(0, tm, issue, 0)

    @pl.when(i == 0)
    def _():
        issue_step(0, 0)

    @pl.when(i + 1 < pl.num_programs(0))
    def _():
        issue_step(i + 1, (i + 1) % 2)

    slot = i % 2
    for k in range(TOP_K):
        pltpu.make_async_copy(ys_hbm.at[pl.ds(0, tm * ROW_SUB), :], buf.at[slot, k], sem.at[slot]).wait()
    meta = meta_ref[...]
    g1 = meta[:, META_G1:META_G1 + 1]
    g2 = meta[:, META_G2:META_G2 + 1]
    y1 = _unpack_rows(buf.at[slot, 0], tm)
    y2 = _unpack_rows(buf.at[slot, 1], tm)
    for c in range(len(y1)):
        sl = slice(c * 128, (c + 1) * 128)
        o_ref[:, sl] = x_ref[:, sl] + g1 * y1[c] + g2 * y2[c]


def combine(dest, x, meta, ys, *, tm):
    s, d = x.shape
    return pl.pallas_call(
        functools.partial(_combine_kernel, tm=tm),
        grid_spec=pltpu.PrefetchScalarGridSpec(
            num_scalar_prefetch=1,
            grid=(s // tm,),
            in_specs=[
                pl.BlockSpec((tm, d), lambda i, dest: (i, 0)),
                pl.BlockSpec((tm, 128), lambda i, dest: (i, 0)),
                pl.BlockSpec(memory_space=pl.ANY),
            ],
            out_specs=pl.BlockSpec((tm, d), lambda i, dest: (i, 0)),
            scratch_shapes=[pltpu.VMEM((2, TOP_K, tm * ROW_SUB, 128), U32), pltpu.SemaphoreType.DMA((2,))],
        ),
        out_shape=jax.ShapeDtypeStruct((s, d), F32),
        compiler_params=_params(("arbitrary",)),
        name="moe_combine",
    )(dest, x, meta, ys)


def moe(x, g, w_router, wg, wu, wd, *, tm_rows, tf):
    s = x.shape[0]
    wr = _pad_cols(w_router.astype(F32), 128)
    wr_hi = wr.astype(BF16)
    wr_lo = (wr - wr_hi.astype(F32)).astype(BF16)
    hp, meta, cnt = router(x, g, jnp.stack([wr_hi, wr_lo]), tm=T.route_rows)
    i1, i2 = meta[:, META_I1].astype(jnp.int32), meta[:, META_I2].astype(jnp.int32)
    r1, r2 = meta[:, META_R1].astype(jnp.int32), meta[:, META_R2].astype(jnp.int32)
    counts = cnt[0, :N_EXPERTS].astype(jnp.int32)
    padded = (counts + tm_rows - 1) // tm_rows * tm_rows
    ends = jnp.cumsum(padded)
    offs = ends - padded
    eids = jnp.arange(N_EXPERTS, dtype=jnp.int32)
    off1 = jnp.sum(jnp.where(i1[:, None] == eids, offs, 0), axis=1)
    off2 = jnp.sum(jnp.where(i2[:, None] == eids, offs, 0), axis=1)
    dest = jnp.stack([off1 + r1, off2 + r2], axis=1).reshape(-1)
    n_tiles = (TOP_K * s) // tm_rows + N_EXPERTS
    n_used = (ends[-1] // tm_rows).reshape(1)
    tile_start = jnp.minimum(jnp.arange(n_tiles, dtype=jnp.int32), n_used - 1) * tm_rows
    tile_expert = jnp.sum(tile_start[:, None] >= ends[None, :], axis=1).astype(jnp.int32)
    tile_ids = jnp.arange(n_tiles, dtype=jnp.int32)
    rows_end = jnp.sum(jnp.where(tile_expert[:, None] == eids, offs + counts, 0), axis=1)
    tile_rows = jnp.where(tile_ids < n_used, jnp.clip(rows_end - tile_ids * tm_rows, 0, tm_rows), 0)
    tile_fill = ((tile_rows + MOE_ROW_GROUP - 1) // MOE_ROW_GROUP).astype(jnp.int32)
    xs = dispatch(dest, hp, n_tiles * tm_rows, tm=T.move_rows)
    ys = moe_ffn(tile_expert, n_used, tile_fill, xs, wg, wu, wd, tm=tm_rows, tf=tf)
    return combine(dest, x, meta, ys, tm=T.move_rows)


def _pad_cols(w, n):
    return jnp.pad(w, ((0, 0), (0, n - w.shape[1])))


def _mla_weights(w_q_b, w_kv_b):
    wq = w_q_b.reshape(Q_LORA, N_HEADS_B, QK_DIM_B)
    wq = jnp.pad(wq, ((0, 0), (0, 0), (0, QK_PAD_B - QK_DIM_B))).reshape(Q_LORA, N_HEADS_B * QK_PAD_B)
    wkv = w_kv_b.reshape(KV_LORA, N_HEADS_B, NOPE_DIM + V_DIM)
    wk = wkv[:, :, :NOPE_DIM].reshape(KV_LORA, N_HEADS_B * NOPE_DIM)
    wv = wkv[:, :, NOPE_DIM:].reshape(KV_LORA, N_HEADS_B * V_DIM)
    return wq.astype(BF16), wk.astype(BF16), wv.astype(BF16)


def _rope_tables(positions):
    half = ROPE_DIM // 2
    inv = ROPE_THETA ** (-jnp.arange(half, dtype=F32) / half)
    ang = positions.astype(F32)[:, None] * inv
    cos, sin = jnp.cos(ang), jnp.sin(ang)
    z = jnp.zeros_like(cos)
    c = jnp.concatenate([cos, cos, z, z], axis=-1)
    s1 = jnp.concatenate([-sin, z, z, z], axis=-1)
    s2 = jnp.concatenate([z, sin, z, z], axis=-1)
    return c, s1, s2


def kernel(x, positions, rel_bias_table, norm_mix_g, w_in, q_a_norm_g, kv_a_norm_g, w_q_b, w_kv_b, q_norm_a_g, k_norm_a_g, q_norm_b_g, k_norm_b_g, w_out, norm_ffn_g, w_ff_gate, w_ff_up, w_ff_down, w_router, w_exp_gate, w_exp_up, w_exp_down):
    batch, seq, d = x.shape
    depth = w_in.shape[0]
    outs = []
    bias = _dilated_bias(rel_bias_table)
    for bi in range(batch):
        xs = x.reshape(seq, d) if batch == 1 else x[bi]
        rope_c, rope_s1, rope_s2 = _rope_tables(positions[bi])
        for l in range(depth):
            w_main = w_in[l][:, :3 * WIDTH_A].astype(BF16)
            w_tail = _pad_cols(w_in[l][:, 3 * WIDTH_A:], PROJ_COLS - 3 * WIDTH_A).astype(BF16)
            head_g = jnp.concatenate([jnp.tile(q_norm_a_g[l] * (HEAD_DIM ** -0.5 * LOG2E), N_HEADS_A),
                                      jnp.tile(k_norm_a_g[l], N_HEADS_A)])[None]
            proj = norm_matmul(xs, norm_mix_g[l][None], w_main, w_tail, head_g, tm=T.proj_rows,
                               out_dtype=F32)
            a = dilated_mixer(proj, bias)
            wq, wk, wv = _mla_weights(w_q_b[l], w_kv_b[l])
            qg = _pad_cols(q_norm_b_g[l][None] * (QK_DIM_B ** -0.5 * LOG2E), QK_PAD_B)
            kg = _pad_cols(k_norm_b_g[l][None], QK_PAD_B)
            qb, kb, vb = mla_prep(proj, q_a_norm_g[l][None], kv_a_norm_g[l][None], wq, wk, wv,
                                  qg, kg, rope_c, rope_s1, rope_s2, tm=T.prep_rows)
            b = mla_flash(qb, kb, vb, tq=T.flash_q, tk=T.flash_k)
            w_out_l = w_out[l].astype(BF16)
            xs = out_proj(xs, a, b, w_out_l[:WIDTH_A], w_out_l[WIDTH_A:], tm=T.out_rows, tn=T.out_cols)
            gf = norm_ffn_g[l][None]
            if l % 2 == 0:
                i = l // 2
                xs = ffn(xs, gf, w_ff_gate[i].astype(BF16), w_ff_up[i].astype(BF16),
                         w_ff_down[i].astype(BF16), tm=T.ffn_rows, tf=T.ffn_cols)
            else:
                i = l // 2
                xs = moe(xs, gf, w_router[i], w_exp_gate[i].astype(BF16),
                         w_exp_up[i].astype(BF16), w_exp_down[i].astype(BF16),
                         tm_rows=T.ffn_rows, tf=T.ffn_cols)
        outs.append(xs)
    return outs[0].reshape(1, seq, d) if batch == 1 else jnp.stack(outs, axis=0)
```

```python
import functools
import math
from typing import NamedTuple

import jax
import jax.numpy as jnp
from jax import lax
from jax.experimental import pallas as pl
from jax.experimental.pallas import tpu as pltpu

F32 = jnp.float32
BF16 = jnp.bfloat16

D_MODEL = 2048
HEAD_DIM = 128
N_HEADS_A = 8
DILATED_PATTERNS = ((128, 1), (512, 4), (2048, 16))
BLK = 128
N_BUCKETS = 32
MAX_DISTANCE = 2048
N_HEADS_B = 8
Q_LORA = 512
KV_LORA = 256
NOPE_DIM = 128
ROPE_DIM = 64
V_DIM = 128
QK_DIM_B = NOPE_DIM + ROPE_DIM
QK_PAD_B = 256
VT_ROWS = V_DIM + 16
LOG2E = math.log2(math.e)
ROPE_THETA = 10000.0
WIDTH_A = N_HEADS_A * HEAD_DIM
WIDTH_B = N_HEADS_B * V_DIM
IN_COLS = 3 * WIDTH_A + Q_LORA + KV_LORA + ROPE_DIM
PROJ_COLS = 4096
D_FF = 5632
N_EXPERTS = 8
TOP_K = 2
ROW_SUB = 8
MOE_ROW_GROUP = 128
HALF = D_MODEL // 2
U32 = jnp.uint32
EPS = 1e-6
NEG = -1e30

SUPER = 2048
DEINT = 4
VMEM_LIMIT = 56 * 1024 * 1024


class _Tiles(NamedTuple):
    proj_rows: int = 1024
    prep_rows: int = 512
    flash_q: int = 2048
    flash_k: int = 512
    out_rows: int = 1024
    out_cols: int = 1024
    ffn_rows: int = 512
    ffn_cols: int = 512
    route_rows: int = 512
    move_rows: int = 256


T = _Tiles()


def _params(sem, vmem=VMEM_LIMIT):
    return pltpu.CompilerParams(dimension_semantics=sem, vmem_limit_bytes=vmem)


def _rms(x, g, n=None):
    ss = jnp.sum(x * x, axis=-1, keepdims=True)
    n = x.shape[-1] if n is None else n
    return x * lax.rsqrt(ss * (1.0 / n) + EPS) * g


def _norm_matmul_kernel(x_ref, g_ref, w_ref, wt_ref, hg_ref, o_ref, hn_ref, *, n_main, n_headnorm):
    j = pl.program_id(1)

    @pl.when(j == 0)
    def _():
        hn_ref[...] = _rms(x_ref[...], g_ref[...]).astype(BF16)

    @pl.when(j < n_headnorm)
    def _():
        acc = jnp.dot(hn_ref[...], w_ref[...], preferred_element_type=F32)
        for c in range(acc.shape[1] // HEAD_DIM):
            sl = slice(c * HEAD_DIM, (c + 1) * HEAD_DIM)
            o_ref[:, sl] = _rms(acc[:, sl], hg_ref[:, sl]).astype(o_ref.dtype)

    @pl.when(jnp.logical_and(j >= n_headnorm, j < n_main))
    def _():
        o_ref[...] = jnp.dot(hn_ref[...], w_ref[...], preferred_element_type=F32).astype(o_ref.dtype)

    @pl.when(j == n_main)
    def _():
        o_ref[...] = jnp.dot(hn_ref[...], wt_ref[...], preferred_element_type=F32).astype(o_ref.dtype)


def norm_matmul(x, g, w, w_tail, head_g, *, tm, out_dtype):
    s, k = x.shape
    tn = w_tail.shape[1]
    n_main = w.shape[1] // tn
    n_headnorm = head_g.shape[1] // tn
    return pl.pallas_call(
        functools.partial(_norm_matmul_kernel, n_main=n_main, n_headnorm=n_headnorm),
        grid=(s // tm, n_main + 1),
        in_specs=[
            pl.BlockSpec((tm, k), lambda i, j: (i, 0)),
            pl.BlockSpec((1, k), lambda i, j: (0, 0)),
            pl.BlockSpec((k, tn), lambda i, j: (0, jnp.minimum(j, n_main - 1))),
            pl.BlockSpec((k, tn), lambda i, j: (0, 0)),
            pl.BlockSpec((1, tn), lambda i, j: (0, jnp.minimum(j, n_headnorm - 1))),
        ],
        out_specs=pl.BlockSpec((tm, tn), lambda i, j: (i, j)),
        out_shape=jax.ShapeDtypeStruct((s, (n_main + 1) * tn), out_dtype),
        scratch_shapes=[pltpu.VMEM((tm, k), BF16)],
        compiler_params=_params(("parallel", "arbitrary")),
        name="norm_matmul",
    )(x, g, w, w_tail, head_g)


def _t5_bucket(n):
    max_exact = N_BUCKETS // 2
    nf = jnp.maximum(n, 1).astype(F32)
    large = max_exact + (jnp.log(nf / max_exact) / math.log(MAX_DISTANCE / max_exact)
                         * (N_BUCKETS - max_exact)).astype(jnp.int32)
    large = jnp.minimum(large, N_BUCKETS - 1)
    return jnp.where(n < max_exact, n, large)


def _dilated_bias(rel_bias):
    i = jnp.arange(BLK)[:, None]
    j = jnp.arange(2 * BLK)[None, :]
    delta = i + BLK - j
    out = []
    for window, dilation in DILATED_PATTERNS:
        band = (delta >= 0) & (delta <= window // dilation)
        bucket = _t5_bucket(jnp.maximum(delta, 0) * dilation)
        onehot = (bucket[None] == jnp.arange(N_BUCKETS)[:, None, None]).astype(F32)
        bias = jnp.einsum('nij,nh->hij', onehot, rel_bias.astype(F32), precision=lax.Precision.HIGHEST)
        out.append(jnp.where(band[None], bias * LOG2E, NEG))
    return jnp.stack(out, axis=0)


def _dilated_kernel(q_ref, kp_ref, kc_ref, vp_ref, vc_ref, bias_ref, o_ref, q4_s, k4_s, v4_s, op_s, lse_s):
    sb = pl.program_id(1)
    qseg, kseg = SUPER // DEINT, 2 * SUPER // DEINT
    for r in range(DEINT):
        grp = pl.ds(r, qseg, stride=DEINT)
        q4_s[r * qseg:(r + 1) * qseg, :] = q_ref[grp, :]
        for src_p, src_c, dst in ((kp_ref, kc_ref, k4_s), (vp_ref, vc_ref, v4_s)):
            dst[r * kseg:r * kseg + qseg, :] = src_p[grp, :]
            dst[r * kseg + qseg:(r + 1) * kseg, :] = src_c[grp, :]

    def load_block(d, r, b):
        if d == 1:
            q = q_ref[b * BLK:(b + 1) * BLK, :]
            if b == 0:
                k = jnp.concatenate([kp_ref[SUPER - BLK:SUPER, :], kc_ref[0:BLK, :]], axis=0)
                v = jnp.concatenate([vp_ref[SUPER - BLK:SUPER, :], vc_ref[0:BLK, :]], axis=0)
            else:
                k = kc_ref[(b - 1) * BLK:(b + 1) * BLK, :]
                v = vc_ref[(b - 1) * BLK:(b + 1) * BLK, :]
        elif d == DEINT:
            q = q4_s[r * qseg + b * BLK:r * qseg + (b + 1) * BLK, :]
            k0 = r * kseg + qseg + (b - 1) * BLK
            k, v = k4_s[k0:k0 + 2 * BLK, :], v4_s[k0:k0 + 2 * BLK, :]
        else:
            assert d == DEINT * DEINT and b == 0
            g, c = r % DEINT, r // DEINT
            q = q4_s[pl.ds(g * qseg + c, BLK, stride=DEINT), :]
            k = k4_s[pl.ds(g * kseg + c, 2 * BLK, stride=DEINT), :]
            v = v4_s[pl.ds(g * kseg + c, 2 * BLK, stride=DEINT), :]
        return q.astype(BF16), k.astype(BF16), v.astype(BF16)

    col = lax.broadcasted_iota(jnp.int32, (BLK, 2 * BLK), 1)
    no_prev = jnp.logical_and(sb == 0, col < BLK)

    for pi, (_, d) in enumerate(DILATED_PATTERNS):
        def rows(start, size, d=d):
            return pl.ds(start, size) if d == 1 else pl.ds(start, size, stride=d)

        bias = bias_ref[pi, 0]
        bias_first = jnp.where(no_prev, NEG, bias)
        for r in range(d):
            for b in range(SUPER // (d * BLK)):
                q0 = r + d * BLK * b
                qb, k2, v2 = load_block(d, r, b)
                s = lax.dot_general(qb, k2, (((1,), (1,)), ((), ())), preferred_element_type=F32)
                s = s + (bias_first if b == 0 else bias)
                m = jnp.max(s, axis=-1, keepdims=True)
                p = jnp.exp2(s - m)
                den = jnp.sum(p, axis=-1, keepdims=True)
                o = jnp.dot(p.astype(BF16), v2, preferred_element_type=F32) / den
                lse = m + jnp.log2(den)
                op_s[pi, rows(q0, BLK), :] = o
                lse_s[pi, rows(q0, BLK), :] = jnp.broadcast_to(lse, (BLK, HEAD_DIM))

    chunk = 256
    for c in range(SUPER // chunk):
        sl = slice(c * chunk, (c + 1) * chunk)
        l0, l1, l2 = lse_s[0, sl, :], lse_s[1, sl, :], lse_s[2, sl, :]
        mx = jnp.maximum(jnp.maximum(l0, l1), l2)
        e0, e1, e2 = jnp.exp2(l0 - mx), jnp.exp2(l1 - mx), jnp.exp2(l2 - mx)
        num = e0 * op_s[0, sl, :] + e1 * op_s[1, sl, :] + e2 * op_s[2, sl, :]
        o_ref[sl, :] = (num / (e0 + e1 + e2)).astype(o_ref.dtype)


def dilated_mixer(proj, bias):
    s = proj.shape[0]
    nsb = s // SUPER
    h8 = N_HEADS_A
    blk = (SUPER, HEAD_DIM)
    return pl.pallas_call(
        _dilated_kernel,
        grid=(h8, nsb),
        in_specs=[
            pl.BlockSpec(blk, lambda h, i: (i, h)),
            pl.BlockSpec(blk, lambda h, i: (jnp.maximum(i - 1, 0), h8 + h)),
            pl.BlockSpec(blk, lambda h, i: (i, h8 + h)),
            pl.BlockSpec(blk, lambda h, i: (jnp.maximum(i - 1, 0), 2 * h8 + h)),
            pl.BlockSpec(blk, lambda h, i: (i, 2 * h8 + h)),
            pl.BlockSpec((len(DILATED_PATTERNS), 1, BLK, 2 * BLK), lambda h, i: (0, h, 0, 0)),
        ],
        out_specs=pl.BlockSpec(blk, lambda h, i: (i, h)),
        out_shape=jax.ShapeDtypeStruct((s, WIDTH_A), BF16),
        scratch_shapes=[
            pltpu.VMEM((SUPER, HEAD_DIM), F32),
            pltpu.VMEM((2 * SUPER, HEAD_DIM), F32),
            pltpu.VMEM((2 * SUPER, HEAD_DIM), F32),
            pltpu.VMEM((len(DILATED_PATTERNS), SUPER, HEAD_DIM), F32),
            pltpu.VMEM((len(DILATED_PATTERNS), SUPER, HEAD_DIM), F32),
        ],
        compiler_params=_params(("parallel", "arbitrary")),
        name="dilated_mixer",
    )(proj, proj, proj, proj, proj, bias)


def _mla_prep_kernel(cq_ref, ckv_ref, kr_ref, qag_ref, kvag_ref, wq_ref, wk_ref, wv_ref,
                     qg_ref, kg_ref, c_ref, s1_ref, s2_ref, q_out, k_out, v_out):
    c, s1, s2 = c_ref[...], s1_ref[...], s2_ref[...]

    def rope(x):
        return x * c + pltpu.roll(x, 96, 1) * s1 + pltpu.roll(x, 32, 1) * s2

    cqn = _rms(cq_ref[...], qag_ref[...]).astype(BF16)
    ckvn = _rms(ckv_ref[...], kvag_ref[...]).astype(BF16)
    qpre = jnp.dot(cqn, wq_ref[...], preferred_element_type=F32)
    knope = jnp.dot(ckvn, wk_ref[...], preferred_element_type=F32)
    v = jnp.dot(ckvn, wv_ref[...], preferred_element_type=F32)
    kr = kr_ref[...]
    kr_ss = jnp.sum(kr * kr, axis=-1, keepdims=True)
    qg, kg = qg_ref[...], kg_ref[...]
    kr_roped = rope(kr * kg[:, NOPE_DIM:])
    inv_n = 1.0 / QK_DIM_B
    ones = jnp.ones((VT_ROWS - V_DIM, cq_ref.shape[0]), BF16)
    for h in range(N_HEADS_B):
        qh = qpre[:, h * QK_PAD_B:(h + 1) * QK_PAD_B]
        rs = lax.rsqrt(jnp.sum(qh * qh, axis=-1, keepdims=True) * inv_n + EPS)
        q_out[h, 0:NOPE_DIM, :] = (qh[:, :NOPE_DIM] * rs * qg[:, :NOPE_DIM]).T.astype(BF16)
        q_out[h, NOPE_DIM:QK_PAD_B, :] = rope(qh[:, NOPE_DIM:] * rs * qg[:, NOPE_DIM:]).T.astype(BF16)
        kh = knope[:, h * NOPE_DIM:(h + 1) * NOPE_DIM]
        rs = lax.rsqrt((jnp.sum(kh * kh, axis=-1, keepdims=True) + kr_ss) * inv_n + EPS)
        k_out[h, :, 0:NOPE_DIM] = (kh * rs * kg[:, :NOPE_DIM]).astype(BF16)
        k_out[h, :, NOPE_DIM:QK_PAD_B] = (kr_roped * rs).astype(BF16)
        v_out[h, 0:V_DIM, :] = v[:, h * V_DIM:(h + 1) * V_DIM].T.astype(BF16)
        v_out[h, V_DIM:VT_ROWS, :] = ones


def mla_prep(proj, q_a_g, kv_a_g, wq, wk, wv, qg, kg, rope_c, rope_s1, rope_s2, *, tm):
    s = proj.shape[0]
    hb = N_HEADS_B
    full = lambda shape: pl.BlockSpec(shape, lambda i: (0,) * len(shape))
    cq_blk = (3 * WIDTH_A) // Q_LORA
    ckv_blk = (3 * WIDTH_A + Q_LORA) // KV_LORA
    kr_blk = (3 * WIDTH_A + Q_LORA + KV_LORA) // 128
    return pl.pallas_call(
        _mla_prep_kernel,
        grid=(s // tm,),
        in_specs=[
            pl.BlockSpec((tm, Q_LORA), lambda i: (i, cq_blk)),
            pl.BlockSpec((tm, KV_LORA), lambda i: (i, ckv_blk)),
            pl.BlockSpec((tm, 128), lambda i: (i, kr_blk)),
            full((1, Q_LORA)), full((1, KV_LORA)),
            full(wq.shape), full(wk.shape), full(wv.shape),
            full((1, QK_PAD_B)), full((1, QK_PAD_B)),
            pl.BlockSpec((tm, 128), lambda i: (i, 0)),
            pl.BlockSpec((tm, 128), lambda i: (i, 0)),
            pl.BlockSpec((tm, 128), lambda i: (i, 0)),
        ],
        out_specs=[
            pl.BlockSpec((hb, QK_PAD_B, tm), lambda i: (0, 0, i)),
            pl.BlockSpec((hb, tm, QK_PAD_B), lambda i: (0, i, 0)),
            pl.BlockSpec((hb, VT_ROWS, tm), lambda i: (0, 0, i)),
        ],
        out_shape=[
            jax.ShapeDtypeStruct((hb, QK_PAD_B, s), BF16),
            jax.ShapeDtypeStruct((hb, s, QK_PAD_B), BF16),
            jax.ShapeDtypeStruct((hb, VT_ROWS, s), BF16),
        ],
        compiler_params=_params(("parallel",)),
        name="mla_prep",
    )(proj, proj, proj, q_a_g, kv_a_g, wq, wk, wv, qg, kg, rope_c, rope_s1, rope_s2)


def _flash_kernel(qt_ref, k_ref, vt_ref, o_ref, s_a, s_b, m_s, acc_s, *, tq, tk):
    i = pl.program_id(1)
    m_s[...] = jnp.full(m_s.shape, NEG, F32)
    acc_s[...] = jnp.zeros(acc_s.shape, F32)
    per_tile = tq // tk
    assert per_tile % 2 == 0

    def compute(c, dst, q_lo=0):
        start = pl.multiple_of(c * tk, tk)
        dst[:, q_lo:] = jnp.dot(k_ref[0, pl.ds(start, tk), :], qt_ref[0, :, q_lo:],
                                preferred_element_type=F32)

    def process(c, src, q_lo=0, masked=False):
        s = src[:, q_lo:]
        if masked:
            key = c * tk + lax.broadcasted_iota(jnp.int32, s.shape, 0)
            qry = i * tq + q_lo + lax.broadcasted_iota(jnp.int32, s.shape, 1)
            s = jnp.where(key <= qry, s, NEG)
        m_prev = m_s[:, q_lo:]
        m_new = jnp.maximum(m_prev, jnp.max(s, axis=0, keepdims=True))
        alpha = jnp.exp2(m_prev - m_new)
        p = jnp.exp2(s - m_new).astype(BF16)
        start = pl.multiple_of(c * tk, tk)
        pv = jnp.dot(vt_ref[0, :, pl.ds(start, tk)], p, preferred_element_type=F32)
        acc_s[:, q_lo:] = alpha * acc_s[:, q_lo:] + pv
        m_s[:, q_lo:] = m_new

    bufs = (s_a, s_b)
    compute(0, s_a)
    n_full = per_tile * i

    def trip(t, carry):
        for u in range(per_tile):
            compute(t * per_tile + u + 1, bufs[(u + 1) % 2])
            process(t * per_tile + u, bufs[u % 2])
        return carry

    lax.fori_loop(0, i, trip, 0)
    for d in range(per_tile):
        if d + 1 < per_tile:
            compute(n_full + d + 1, bufs[(d + 1) % 2], (d + 1) * tk)
        process(n_full + d, bufs[d % 2], d * tk, masked=True)
    acc = acc_s[...]
    o_t = acc[0:V_DIM, :] / acc[V_DIM:V_DIM + 1, :]
    o_ref[...] = o_t.T.astype(o_ref.dtype)


def mla_flash(qt, k, vt, *, tq, tk):
    hb, s, _ = k.shape
    return pl.pallas_call(
        functools.partial(_flash_kernel, tq=tq, tk=tk),
        grid=(hb, s // tq),
        in_specs=[
            pl.BlockSpec((1, QK_PAD_B, tq), lambda h, i: (h, 0, i)),
            pl.BlockSpec((1, s, QK_PAD_B), lambda h, i: (h, 0, 0)),
            pl.BlockSpec((1, VT_ROWS, s), lambda h, i: (h, 0, 0)),
        ],
        out_specs=pl.BlockSpec((tq, V_DIM), lambda h, i: (i, h)),
        out_shape=jax.ShapeDtypeStruct((s, WIDTH_B), BF16),
        scratch_shapes=[
            pltpu.VMEM((tk, tq), F32),
            pltpu.VMEM((tk, tq), F32),
            pltpu.VMEM((1, tq), F32),
            pltpu.VMEM((VT_ROWS, tq), F32),
        ],
        compiler_params=_params(("parallel", "arbitrary")),
        name="mla_flash",
    )(qt, k, vt)


def _out_proj_kernel(x_ref, a_ref, b_ref, wa_ref, wb_ref, o_ref):
    acc = jnp.dot(a_ref[...], wa_ref[...], preferred_element_type=F32)
    acc = acc + jnp.dot(b_ref[...], wb_ref[...], preferred_element_type=F32)
    o_ref[...] = x_ref[...] + acc


def out_proj(x, a, b, wa, wb, *, tm, tn):
    s, n = x.shape
    ka, kb = a.shape[1], b.shape[1]
    return pl.pallas_call(
        _out_proj_kernel,
        grid=(s // tm, n // tn),
        in_specs=[
            pl.BlockSpec((tm, tn), lambda i, j: (i, j)),
            pl.BlockSpec((tm, ka), lambda i, j: (i, 0)),
            pl.BlockSpec((tm, kb), lambda i, j: (i, 0)),
            pl.BlockSpec((ka, tn), lambda i, j: (0, j)),
            pl.BlockSpec((kb, tn), lambda i, j: (0, j)),
        ],
        out_specs=pl.BlockSpec((tm, tn), lambda i, j: (i, j)),
        out_shape=jax.ShapeDtypeStruct((s, n), F32),
        compiler_params=_params(("parallel", "arbitrary")),
        name="out_proj",
    )(x, a, b, wa, wb)


def _swiglu_step(h, wg, wu, wd):
    gate = jnp.dot(h, wg, preferred_element_type=F32)
    up = jnp.dot(h, wu, preferred_element_type=F32)
    act = gate * jax.nn.sigmoid(gate) * up
    return jnp.dot(act.astype(BF16), wd, preferred_element_type=F32)


def _ffn_kernel(x_ref, g_ref, wg_ref, wu_ref, wd_ref, o_ref, hn_ref):
    @pl.when(pl.program_id(1) == 0)
    def _():
        hn_ref[...] = _rms(x_ref[...], g_ref[...]).astype(BF16)
        o_ref[...] = x_ref[...]

    o_ref[...] += _swiglu_step(hn_ref[...], wg_ref[...], wu_ref[...], wd_ref[...])


def ffn(x, g, wg, wu, wd, *, tm, tf):
    s, d = x.shape
    fdim = wg.shape[1]
    return pl.pallas_call(
        _ffn_kernel,
        grid=(s // tm, fdim // tf),
        in_specs=[
            pl.BlockSpec((tm, d), lambda i, f: (i, 0)),
            pl.BlockSpec((1, d), lambda i, f: (0, 0)),
            pl.BlockSpec((d, tf), lambda i, f: (0, f)),
            pl.BlockSpec((d, tf), lambda i, f: (0, f)),
            pl.BlockSpec((tf, d), lambda i, f: (f, 0)),
        ],
        out_specs=pl.BlockSpec((tm, d), lambda i, f: (i, 0)),
        out_shape=jax.ShapeDtypeStruct((s, d), F32),
        scratch_shapes=[pltpu.VMEM((tm, d), BF16)],
        compiler_params=_params(("parallel", "arbitrary")),
        name="ffn",
    )(x, g, wg, wu, wd)


def _pack_rows(y, dst):
    n = y.shape[0]
    for s in range(ROW_SUB):
        lo = y[:, s * 128:(s + 1) * 128].astype(BF16).astype(F32)
        hi = y[:, HALF + s * 128:HALF + (s + 1) * 128].astype(BF16).astype(F32)
        w = (lax.bitcast_convert_type(lo, U32) >> 16) | lax.bitcast_convert_type(hi, U32)
        dst[pl.ds(s, n, stride=ROW_SUB), :] = w


def _unpack_rows(src, n):
    lo, hi = [], []
    for s in range(ROW_SUB):
        w = src[pl.ds(s, n, stride=ROW_SUB), :]
        lo.append(lax.bitcast_convert_type(w << 16, F32))
        hi.append(lax.bitcast_convert_type(w & jnp.uint32(0xFFFF0000), F32))
    return lo + hi


META_I1, META_I2, META_R1, META_R2, META_G1, META_G2 = range(6)


def _router_kernel(x_ref, g_ref, wr_ref, hp_ref, meta_ref, cnt_ref, carry):
    @pl.when(pl.program_id(0) == 0)
    def _():
        carry[...] = jnp.zeros(carry.shape, F32)

    hn = _rms(x_ref[...], g_ref[...])
    _pack_rows(hn, hp_ref)
    h_hi = hn.astype(BF16)
    h_lo = (hn - h_hi.astype(F32)).astype(BF16)
    w_hi, w_lo = wr_ref[0], wr_ref[1]
    logits = (jnp.dot(h_hi, w_hi, preferred_element_type=F32)
              + jnp.dot(h_hi, w_lo, preferred_element_type=F32)
              + jnp.dot(h_lo, w_hi, preferred_element_type=F32))
    lane = lax.broadcasted_iota(jnp.int32, logits.shape, 1)
    logits = jnp.where(lane < N_EXPERTS, logits, NEG)
    v1 = jnp.max(logits, axis=-1, keepdims=True)
    i1 = jnp.min(jnp.where(logits == v1, lane, 128), axis=-1, keepdims=True)
    rest = jnp.where(lane == i1, NEG, logits)
    v2 = jnp.max(rest, axis=-1, keepdims=True)
    i2 = jnp.min(jnp.where(rest == v2, lane, 128), axis=-1, keepdims=True)
    e2 = jnp.exp(v2 - v1)
    g1 = 1.0 / (1.0 + e2)
    g2 = e2 / (1.0 + e2)

    tm = hn.shape[0]
    member = jnp.where(jnp.logical_or(lane == i1, lane == i2), 1.0, 0.0)
    row = lax.broadcasted_iota(jnp.int32, (tm, tm), 0)
    col = lax.broadcasted_iota(jnp.int32, (tm, tm), 1)
    earlier = jnp.where(col < row, 1.0, 0.0).astype(BF16)
    rank = carry[...] + jnp.dot(earlier, member.astype(BF16), preferred_element_type=F32)
    r1 = jnp.sum(jnp.where(lane == i1, rank, 0.0), axis=-1, keepdims=True)
    r2 = jnp.sum(jnp.where(lane == i2, rank, 0.0), axis=-1, keepdims=True)
    carry[...] += jnp.sum(member, axis=0, keepdims=True)
    cnt_ref[...] = carry[...]
    meta = jnp.zeros(logits.shape, F32)
    for pos, val in ((META_I1, i1.astype(F32)), (META_I2, i2.astype(F32)), (META_R1, r1),
                     (META_R2, r2), (META_G1, g1), (META_G2, g2)):
        meta = jnp.where(lane == pos, val, meta)
    meta_ref[...] = meta


def router(x, g, wr, *, tm):
    s, d = x.shape
    return pl.pallas_call(
        _router_kernel,
        grid=(s // tm,),
        in_specs=[
            pl.BlockSpec((tm, d), lambda i: (i, 0)),
            pl.BlockSpec((1, d), lambda i: (0, 0)),
            pl.BlockSpec((2, d, 128), lambda i: (0, 0, 0)),
        ],
        out_specs=[
            pl.BlockSpec((tm * ROW_SUB, 128), lambda i: (i, 0)),
            pl.BlockSpec((tm, 128), lambda i: (i, 0)),
            pl.BlockSpec((1, 128), lambda i: (0, 0)),
        ],
        out_shape=[
            jax.ShapeDtypeStruct((s * ROW_SUB, 128), U32),
            jax.ShapeDtypeStruct((s, 128), F32),
            jax.ShapeDtypeStruct((1, 128), F32),
        ],
        scratch_shapes=[pltpu.VMEM((1, 128), F32)],
        compiler_params=_params(("arbitrary",)),
        name="router",
    )(x, g, wr)


def _row_tile(ref, r):
    return ref.at[pl.ds(pl.multiple_of(r * ROW_SUB, ROW_SUB), ROW_SUB), :]


def _dispatch_kernel(dest_ref, hp_hbm, xs_in, xs_out, stage, in_sem, out_sem, *, tm):
    del xs_in
    i = pl.program_id(0)
    last = pl.num_programs(0) - 1
    block_rows = tm * ROW_SUB

    def fetch(step):
        src = hp_hbm.at[pl.ds(pl.multiple_of(step * block_rows, block_rows), block_rows), :]
        return pltpu.make_async_copy(src, stage.at[step % 3], in_sem.at[step % 3])

    def drain(step):
        for _ in range(TOP_K):
            pltpu.make_async_copy(stage.at[step % 3], xs_out.at[pl.ds(0, block_rows), :],
                                  out_sem.at[step % 2]).wait()

    @pl.when(i == 0)
    def _():
        fetch(0).start()

    @pl.when(i < last)
    def _():
        fetch(i + 1).start()

    fetch(i).wait()
    src = stage.at[i % 3]

    def issue(t, c):
        for k in range(TOP_K):
            pltpu.make_async_copy(_row_tile(src, t), _row_tile(xs_out, dest_ref[2 * (i * tm + t) + k]),
                                  out_sem.at[i % 2]).start()
        return c

    lax.fori_loop(0, tm, issue, 0)

    @pl.when(i > 0)
    def _():
        drain(i - 1)

    @pl.when(i == last)
    def _():
        drain(i)


def dispatch(dest, hp, n_rows, *, tm):
    s = hp.shape[0] // ROW_SUB
    xs0 = jnp.zeros((n_rows * ROW_SUB, 128), U32)
    return pl.pallas_call(
        functools.partial(_dispatch_kernel, tm=tm),
        grid_spec=pltpu.PrefetchScalarGridSpec(
            num_scalar_prefetch=1,
            grid=(s // tm,),
            in_specs=[pl.BlockSpec(memory_space=pl.ANY), pl.BlockSpec(memory_space=pl.ANY)],
            out_specs=pl.BlockSpec(memory_space=pl.ANY),
            scratch_shapes=[pltpu.VMEM((3, tm * ROW_SUB, 128), U32),
                            pltpu.SemaphoreType.DMA((3,)), pltpu.SemaphoreType.DMA((2,))],
        ),
        out_shape=jax.ShapeDtypeStruct(xs0.shape, U32),
        input_output_aliases={2: 0},
        compiler_params=_params(("arbitrary",)),
        name="moe_dispatch",
    )(dest, hp, xs0)


def _moe_ffn_kernel(te_ref, nu_ref, fill_ref, xs_ref, wg_ref, wu_ref, wd_ref, ys_ref, xb, acc, *, tm):
    del te_ref, nu_ref
    f = pl.program_id(1)
    fill = fill_ref[pl.program_id(0)]
    used = fill > 0
    last = f == pl.num_programs(1) - 1

    @pl.when(jnp.logical_and(used, f == 0))
    def _():
        for c, v in enumerate(_unpack_rows(xs_ref, tm)):
            xb[:, c * 128:(c + 1) * 128] = v.astype(BF16)
        acc[...] = jnp.zeros(acc.shape, F32)

    for groups in range(1, tm // MOE_ROW_GROUP + 1):
        @pl.when(fill == groups)
        def _(rows=groups * MOE_ROW_GROUP):
            acc[0:rows, :] += _swiglu_step(xb[0:rows, :], wg_ref[0], wu_ref[0], wd_ref[0])

    @pl.when(jnp.logical_and(used, last))
    def _():
        _pack_rows(acc[...], ys_ref)

    @pl.when(jnp.logical_and(jnp.logical_not(used), last))
    def _():
        ys_ref[...] = jnp.zeros(ys_ref.shape, U32)


def moe_ffn(tile_expert, n_used, tile_fill, xs, wg, wu, wd, *, tm, tf):
    nt = tile_expert.shape[0]
    _, d, fdim = wg.shape
    nf = fdim // tf

    def f_idx(j, f, nu):
        return jnp.where(j < nu[0], f, nf - 1)

    return pl.pallas_call(
        functools.partial(_moe_ffn_kernel, tm=tm),
        grid_spec=pltpu.PrefetchScalarGridSpec(
            num_scalar_prefetch=3,
            grid=(nt, nf),
            in_specs=[
                pl.BlockSpec((tm * ROW_SUB, 128), lambda j, f, te, nu, fl: (jnp.minimum(j, nu[0] - 1), 0)),
                pl.BlockSpec((1, d, tf), lambda j, f, te, nu, fl: (te[j], 0, f_idx(j, f, nu))),
                pl.BlockSpec((1, d, tf), lambda j, f, te, nu, fl: (te[j], 0, f_idx(j, f, nu))),
                pl.BlockSpec((1, tf, d), lambda j, f, te, nu, fl: (te[j], f_idx(j, f, nu), 0)),
            ],
            out_specs=pl.BlockSpec((tm * ROW_SUB, 128), lambda j, f, te, nu, fl: (j, 0)),
            scratch_shapes=[pltpu.VMEM((tm, d), BF16), pltpu.VMEM((tm, d), F32)],
        ),
        out_shape=jax.ShapeDtypeStruct(xs.shape, U32),
        compiler_params=_params(("parallel", "arbitrary")),
        name="moe_ffn",
    )(tile_expert, n_used, tile_fill, xs, wg, wu, wd)


def _combine_kernel(dest_ref, x_ref, meta_ref, ys_hbm, o_ref, buf, sem, *, tm):
    i = pl.program_id(0)

    def issue_step(step, slot):
        def issue(t, c):
            for k in range(TOP_K):
                pltpu.make_async_copy(_row_tile(ys_hbm, dest_ref[2 * (step * tm + t) + k]),
                                      _row_tile(buf.at[slot, k], t), sem.at[slot]).start()
            return c
        lax.fori_loop(0, tm, issue, 0)

    @pl.when(i == 0)
    def _():
        issue_step(0, 0)

    @pl.when(i + 1 < pl.num_programs(0))
    def _():
        issue_step(i + 1, (i + 1) % 2)

    slot = i % 2
    for k in range(TOP_K):
        pltpu.make_async_copy(ys_hbm.at[pl.ds(0, tm * ROW_SUB), :], buf.at[slot, k], sem.at[slot]).wait()
    meta = meta_ref[...]
    g1 = meta[:, META_G1:META_G1 + 1]
    g2 = meta[:, META_G2:META_G2 + 1]
    y1 = _unpack_rows(buf.at[slot, 0], tm)
    y2 = _unpack_rows(buf.at[slot, 1], tm)
    for c in range(len(y1)):
        sl = slice(c * 128, (c + 1) * 128)
        o_ref[:, sl] = x_ref[:, sl] + g1 * y1[c] + g2 * y2[c]


def combine(dest, x, meta, ys, *, tm):
    s, d = x.shape
    return pl.pallas_call(
        functools.partial(_combine_kernel, tm=tm),
        grid_spec=pltpu.PrefetchScalarGridSpec(
            num_scalar_prefetch=1,
            grid=(s // tm,),
            in_specs=[
                pl.BlockSpec((tm, d), lambda i, dest: (i, 0)),
                pl.BlockSpec((tm, 128), lambda i, dest: (i, 0)),
                pl.BlockSpec(memory_space=pl.ANY),
            ],
            out_specs=pl.BlockSpec((tm, d), lambda i, dest: (i, 0)),
            scratch_shapes=[pltpu.VMEM((2, TOP_K, tm * ROW_SUB, 128), U32), pltpu.SemaphoreType.DMA((2,))],
        ),
        out_shape=jax.ShapeDtypeStruct((s, d), F32),
        compiler_params=_params(("arbitrary",)),
        name="moe_combine",
    )(dest, x, meta, ys)


def moe(x, g, w_router, wg, wu, wd, *, tm_rows, tf):
    s = x.shape[0]
    wr = _pad_cols(w_router.astype(F32), 128)
    wr_hi = wr.astype(BF16)
    wr_lo = (wr - wr_hi.astype(F32)).astype(BF16)
    hp, meta, cnt = router(x, g, jnp.stack([wr_hi, wr_lo]), tm=T.route_rows)
    i1, i2 = meta[:, META_I1].astype(jnp.int32), meta[:, META_I2].astype(jnp.int32)
    r1, r2 = meta[:, META_R1].astype(jnp.int32), meta[:, META_R2].astype(jnp.int32)
    counts = cnt[0, :N_EXPERTS].astype(jnp.int32)
    padded = (counts + tm_rows - 1) // tm_rows * tm_rows
    ends = jnp.cumsum(padded)
    offs = ends - padded
    eids = jnp.arange(N_EXPERTS, dtype=jnp.int32)
    off1 = jnp.sum(jnp.where(i1[:, None] == eids, offs, 0), axis=1)
    off2 = jnp.sum(jnp.where(i2[:, None] == eids, offs, 0), axis=1)
    dest = jnp.stack([off1 + r1, off2 + r2], axis=1).reshape(-1)
    n_tiles = (TOP_K * s) // tm_rows + N_EXPERTS
    n_used = (ends[-1] // tm_rows).reshape(1)
    tile_start = jnp.minimum(jnp.arange(n_tiles, dtype=jnp.int32), n_used - 1) * tm_rows
    tile_expert = jnp.sum(tile_start[:, None] >= ends[None, :], axis=1).astype(jnp.int32)
    tile_ids = jnp.arange(n_tiles, dtype=jnp.int32)
    rows_end = jnp.sum(jnp.where(tile_expert[:, None] == eids, offs + counts, 0), axis=1)
    tile_rows = jnp.where(tile_ids < n_used, jnp.clip(rows_end - tile_ids * tm_rows, 0, tm_rows), 0)
    tile_fill = ((tile_rows + MOE_ROW_GROUP - 1) // MOE_ROW_GROUP).astype(jnp.int32)
    xs = dispatch(dest, hp, n_tiles * tm_rows, tm=T.move_rows)
    ys = moe_ffn(tile_expert, n_used, tile_fill, xs, wg, wu, wd, tm=tm_rows, tf=tf)
    return combine(dest, x, meta, ys, tm=T.move_rows)


def _pad_cols(w, n):
    return jnp.pad(w, ((0, 0), (0, n - w.shape[1])))


def _mla_weights(w_q_b, w_kv_b):
    wq = w_q_b.reshape(Q_LORA, N_HEADS_B, QK_DIM_B)
    wq = jnp.pad(wq, ((0, 0), (0, 0), (0, QK_PAD_B - QK_DIM_B))).reshape(Q_LORA, N_HEADS_B * QK_PAD_B)
    wkv = w_kv_b.reshape(KV_LORA, N_HEADS_B, NOPE_DIM + V_DIM)
    wk = wkv[:, :, :NOPE_DIM].reshape(KV_LORA, N_HEADS_B * NOPE_DIM)
    wv = wkv[:, :, NOPE_DIM:].reshape(KV_LORA, N_HEADS_B * V_DIM)
    return wq.astype(BF16), wk.astype(BF16), wv.astype(BF16)


def _rope_tables(positions):
    half = ROPE_DIM // 2
    inv = ROPE_THETA ** (-jnp.arange(half, dtype=F32) / half)
    ang = positions.astype(F32)[:, None] * inv
    cos, sin = jnp.cos(ang), jnp.sin(ang)
    z = jnp.zeros_like(cos)
    c = jnp.concatenate([cos, cos, z, z], axis=-1)
    s1 = jnp.concatenate([-sin, z, z, z], axis=-1)
    s2 = jnp.concatenate([z, sin, z, z], axis=-1)
    return c, s1, s2


def kernel(x, positions, rel_bias_table, norm_mix_g, w_in, q_a_norm_g, kv_a_norm_g, w_q_b, w_kv_b, q_norm_a_g, k_norm_a_g, q_norm_b_g, k_norm_b_g, w_out, norm_ffn_g, w_ff_gate, w_ff_up, w_ff_down, w_router, w_exp_gate, w_exp_up, w_exp_down):
    batch, seq, d = x.shape
    depth = w_in.shape[0]
    outs = []
    bias = _dilated_bias(rel_bias_table)
    for bi in range(batch):
        xs = x.reshape(seq, d) if batch == 1 else x[bi]
        rope_c, rope_s1, rope_s2 = _rope_tables(positions[bi])
        for l in range(depth):
            w_main = w_in[l][:, :3 * WIDTH_A].astype(BF16)
            w_tail = _pad_cols(w_in[l][:, 3 * WIDTH_A:], PROJ_COLS - 3 * WIDTH_A).astype(BF16)
            head_g = jnp.concatenate([jnp.tile(q_norm_a_g[l] * (HEAD_DIM ** -0.5 * LOG2E), N_HEADS_A),
                                      jnp.tile(k_norm_a_g[l], N_HEADS_A)])[None]
            proj = norm_matmul(xs, norm_mix_g[l][None], w_main, w_tail, head_g, tm=T.proj_rows,
                               out_dtype=F32)
            a = dilated_mixer(proj, bias)
            wq, wk, wv = _mla_weights(w_q_b[l], w_kv_b[l])
            qg = _pad_cols(q_norm_b_g[l][None] * (QK_DIM_B ** -0.5 * LOG2E), QK_PAD_B)
            kg = _pad_cols(k_norm_b_g[l][None], QK_PAD_B)
            qb, kb, vb = mla_prep(proj, q_a_norm_g[l][None], kv_a_norm_g[l][None], wq, wk, wv,
                                  qg, kg, rope_c, rope_s1, rope_s2, tm=T.prep_rows)
            b = mla_flash(qb, kb, vb, tq=T.flash_q, tk=T.flash_k)
            w_out_l = w_out[l].astype(BF16)
            xs = out_proj(xs, a, b, w_out_l[:WIDTH_A], w_out_l[WIDTH_A:], tm=T.out_rows, tn=T.out_cols)
            gf = norm_ffn_g[l][None]
            if l % 2 == 0:
                i = l // 2
                xs = ffn(xs, gf, w_ff_gate[i].astype(BF16), w_ff_up[i].astype(BF16),
                         w_ff_down[i].astype(BF16), tm=T.ffn_rows, tf=T.ffn_cols)
            else:
                i = l // 2
                xs = moe(xs, gf, w_router[i], w_exp_gate[i].astype(BF16),
                         w_exp_up[i].astype(BF16), w_exp_down[i].astype(BF16),
                         tm_rows=T.ffn_rows, tf=T.ffn_cols)
        outs.append(xs)
    return outs[0].reshape(1, seq, d) if batch == 1 else jnp.stack(outs, axis=0)
```

```python
import functools
import math
from typing import NamedTuple

import jax
import jax.numpy as jnp
from jax import lax
from jax.experimental import pallas as pl
from jax.experimental.pallas import tpu as pltpu

F32 = jnp.float32
BF16 = jnp.bfloat16

D_MODEL = 2048
HEAD_DIM = 128
N_HEADS_A = 8
DILATED_PATTERNS = ((128, 1), (512, 4), (2048, 16))
BLK = 128
N_BUCKETS = 32
MAX_DISTANCE = 2048
N_HEADS_B = 8
Q_LORA = 512
KV_LORA = 256
NOPE_DIM = 128
ROPE_DIM = 64
V_DIM = 128
QK_DIM_B = NOPE_DIM + ROPE_DIM
QK_PAD_B = 256
VT_ROWS = V_DIM + 16
LOG2E = math.log2(math.e)
ROPE_THETA = 10000.0
WIDTH_A = N_HEADS_A * HEAD_DIM
WIDTH_B = N_HEADS_B * V_DIM
IN_COLS = 3 * WIDTH_A + Q_LORA + KV_LORA + ROPE_DIM
PROJ_COLS = 4096
D_FF = 5632
N_EXPERTS = 8
TOP_K = 2
ROW_SUB = 8
MOE_ROW_GROUP = 128
HALF = D_MODEL // 2
U32 = jnp.uint32
EPS = 1e-6
NEG = -1e30

SUPER = 2048
DEINT = 4
VMEM_LIMIT = 56 * 1024 * 1024


class _Tiles(NamedTuple):
    proj_rows: int = 1024
    prep_rows: int = 512
    flash_q: int = 2048
    flash_k: int = 512
    out_rows: int = 1024
    out_cols: int = 1024
    ffn_rows: int = 512
    ffn_cols: int = 512
    route_rows: int = 512
    move_rows: int = 256


T = _Tiles()


def _params(sem, vmem=VMEM_LIMIT):
    return pltpu.CompilerParams(dimension_semantics=sem, vmem_limit_bytes=vmem)


def _rms(x, g, n=None):
    ss = jnp.sum(x * x, axis=-1, keepdims=True)
    n = x.shape[-1] if n is None else n
    return x * lax.rsqrt(ss * (1.0 / n) + EPS) * g


def _norm_matmul_kernel(x_ref, g_ref, w_ref, wt_ref, hg_ref, o_ref, hn_ref, *, n_main, n_headnorm):
    j = pl.program_id(1)

    @pl.when(j == 0)
    def _():
        hn_ref[...] = _rms(x_ref[...], g_ref[...]).astype(BF16)

    @pl.when(j < n_headnorm)
    def _():
        acc = jnp.dot(hn_ref[...], w_ref[...], preferred_element_type=F32)
        for c in range(acc.shape[1] // HEAD_DIM):
            sl = slice(c * HEAD_DIM, (c + 1) * HEAD_DIM)
            o_ref[:, sl] = _rms(acc[:, sl], hg_ref[:, sl]).astype(o_ref.dtype)

    @pl.when(jnp.logical_and(j >= n_headnorm, j < n_main))
    def _():
        o_ref[...] = jnp.dot(hn_ref[...], w_ref[...], preferred_element_type=F32).astype(o_ref.dtype)

    @pl.when(j == n_main)
    def _():
        o_ref[...] = jnp.dot(hn_ref[...], wt_ref[...], preferred_element_type=F32).astype(o_ref.dtype)


def norm_matmul(x, g, w, w_tail, head_g, *, tm, out_dtype):
    s, k = x.shape
    tn = w_tail.shape[1]
    n_main = w.shape[1] // tn
    n_headnorm = head_g.shape[1] // tn
    return pl.pallas_call(
        functools.partial(_norm_matmul_kernel, n_main=n_main, n_headnorm=n_headnorm),
        grid=(s // tm, n_main + 1),
        in_specs=[
            pl.BlockSpec((tm, k), lambda i, j: (i, 0)),
            pl.BlockSpec((1, k), lambda i, j: (0, 0)),
            pl.BlockSpec((k, tn), lambda i, j: (0, jnp.minimum(j, n_main - 1))),
            pl.BlockSpec((k, tn), lambda i, j: (0, 0)),
            pl.BlockSpec((1, tn), lambda i, j: (0, jnp.minimum(j, n_headnorm - 1))),
        ],
        out_specs=pl.BlockSpec((tm, tn), lambda i, j: (i, j)),
        out_shape=jax.ShapeDtypeStruct((s, (n_main + 1) * tn), out_dtype),
        scratch_shapes=[pltpu.VMEM((tm, k), BF16)],
        compiler_params=_params(("parallel", "arbitrary")),
        name="norm_matmul",
    )(x, g, w, w_tail, head_g)


def _t5_bucket(n):
    max_exact = N_BUCKETS // 2
    nf = jnp.maximum(n, 1).astype(F32)
    large = max_exact + (jnp.log(nf / max_exact) / math.log(MAX_DISTANCE / max_exact)
                         * (N_BUCKETS - max_exact)).astype(jnp.int32)
    large = jnp.minimum(large, N_BUCKETS - 1)
    return jnp.where(n < max_exact, n, large)


def _dilated_bias(rel_bias):
    i = jnp.arange(BLK)[:, None]
    j = jnp.arange(2 * BLK)[None, :]
    delta = i + BLK - j
    out = []
    for window, dilation in DILATED_PATTERNS:
        band = (delta >= 0) & (delta <= window // dilation)
        bucket = _t5_bucket(jnp.maximum(delta, 0) * dilation)
        onehot = (bucket[None] == jnp.arange(N_BUCKETS)[:, None, None]).astype(F32)
        bias = jnp.einsum('nij,nh->hij', onehot, rel_bias.astype(F32), precision=lax.Precision.HIGHEST)
        out.append(jnp.where(band[None], bias * LOG2E, NEG))
    return jnp.stack(out, axis=0)


def _dilated_kernel(q_ref, kp_ref, kc_ref, vp_ref, vc_ref, bias_ref, o_ref,
                    q4_s, k4_s, v4_s, op_s, lse_s, out_s):
    sb = pl.program_id(1)
    qseg, kseg = SUPER // DEINT, 2 * SUPER // DEINT
    for r in range(DEINT):
        grp = pl.ds(r, qseg, stride=DEINT)
        q4_s[r * qseg:(r + 1) * qseg, :] = q_ref[grp, :]
        for src_p, src_c, dst in ((kp_ref, kc_ref, k4_s), (vp_ref, vc_ref, v4_s)):
            dst[r * kseg:r * kseg + qseg, :] = src_p[grp, :]
            dst[r * kseg + qseg:(r + 1) * kseg, :] = src_c[grp, :]

    def load_block(d, r, b):
        if d == 1:
            q = q_ref[b * BLK:(b + 1) * BLK, :]
            if b == 0:
                k = jnp.concatenate([kp_ref[SUPER - BLK:SUPER, :], kc_ref[0:BLK, :]], axis=0)
                v = jnp.concatenate([vp_ref[SUPER - BLK:SUPER, :], vc_ref[0:BLK, :]], axis=0)
            else:
                k = kc_ref[(b - 1) * BLK:(b + 1) * BLK, :]
                v = vc_ref[(b - 1) * BLK:(b + 1) * BLK, :]
        elif d == DEINT:
            q = q4_s[r * qseg + b * BLK:r * qseg + (b + 1) * BLK, :]
            k0 = r * kseg + qseg + (b - 1) * BLK
            k, v = k4_s[k0:k0 + 2 * BLK, :], v4_s[k0:k0 + 2 * BLK, :]
        else:
            assert d == DEINT * DEINT and b == 0
            g, c = r % DEINT, r // DEINT
            q = q4_s[pl.ds(g * qseg + c, BLK, stride=DEINT), :]
            k = k4_s[pl.ds(g * kseg + c, 2 * BLK, stride=DEINT), :]
            v = v4_s[pl.ds(g * kseg + c, 2 * BLK, stride=DEINT), :]
        return q.astype(BF16), k.astype(BF16), v.astype(BF16)

    col = lax.broadcasted_iota(jnp.int32, (BLK, 2 * BLK), 1)
    no_prev = jnp.logical_and(sb == 0, col < BLK)

    def out_rows(d, r, b):
        if d == 1:
            return pl.ds(b * BLK, BLK)
        if d == DEINT:
            return pl.ds(r * qseg + b * BLK, BLK)
        return pl.ds((r % DEINT) * qseg + r // DEINT, BLK, stride=DEINT)

    for pi, (_, d) in enumerate(DILATED_PATTERNS):
        bias = bias_ref[pi, 0]
        bias_first = jnp.where(no_prev, NEG, bias)
        for r in range(d):
            for b in range(SUPER // (d * BLK)):
                qb, k2, v2 = load_block(d, r, b)
                s = lax.dot_general(qb, k2, (((1,), (1,)), ((), ())), preferred_element_type=F32)
                s = s + (bias_first if b == 0 else bias)
                m = jnp.max(s, axis=-1, keepdims=True)
                p = jnp.exp2(s - m)
                den = jnp.sum(p, axis=-1, keepdims=True)
                o = jnp.dot(p.astype(BF16), v2, preferred_element_type=F32) / den
                lse = m + jnp.log2(den)
                op_s[pi, out_rows(d, r, b), :] = o
                lse_s[pi, out_rows(d, r, b), :] = jnp.broadcast_to(lse, (BLK, HEAD_DIM))

    chunk = 256
    for g in range(DEINT):
        for c in range(qseg // chunk):
            grouped = pl.ds(g * qseg + c * chunk, chunk)
            natural = pl.ds(g + DEINT * c * chunk, chunk, stride=DEINT)
            l0, l1, l2 = lse_s[0, natural, :], lse_s[1, grouped, :], lse_s[2, grouped, :]
            mx = jnp.maximum(jnp.maximum(l0, l1), l2)
            e0, e1, e2 = jnp.exp2(l0 - mx), jnp.exp2(l1 - mx), jnp.exp2(l2 - mx)
            num = e0 * op_s[0, natural, :] + e1 * op_s[1, grouped, :] + e2 * op_s[2, grouped, :]
            out_s[natural, :] = num / (e0 + e1 + e2)
    o_ref[...] = out_s[...].astype(o_ref.dtype)


def dilated_mixer(proj, bias):
    s = proj.shape[0]
    nsb = s // SUPER
    h8 = N_HEADS_A
    blk = (SUPER, HEAD_DIM)
    return pl.pallas_call(
        _dilated_kernel,
        grid=(h8, nsb),
        in_specs=[
            pl.BlockSpec(blk, lambda h, i: (i, h)),
            pl.BlockSpec(blk, lambda h, i: (jnp.maximum(i - 1, 0), h8 + h)),
            pl.BlockSpec(blk, lambda h, i: (i, h8 + h)),
            pl.BlockSpec(blk, lambda h, i: (jnp.maximum(i - 1, 0), 2 * h8 + h)),
            pl.BlockSpec(blk, lambda h, i: (i, 2 * h8 + h)),
            pl.BlockSpec((len(DILATED_PATTERNS), 1, BLK, 2 * BLK), lambda h, i: (0, h, 0, 0)),
        ],
        out_specs=pl.BlockSpec(blk, lambda h, i: (i, h)),
        out_shape=jax.ShapeDtypeStruct((s, WIDTH_A), BF16),
        scratch_shapes=[
            pltpu.VMEM((SUPER, HEAD_DIM), F32),
            pltpu.VMEM((2 * SUPER, HEAD_DIM), F32),
            pltpu.VMEM((2 * SUPER, HEAD_DIM), F32),
            pltpu.VMEM((len(DILATED_PATTERNS), SUPER, HEAD_DIM), F32),
            pltpu.VMEM((len(DILATED_PATTERNS), SUPER, HEAD_DIM), F32),
            pltpu.VMEM((SUPER, HEAD_DIM), F32),
        ],
        compiler_params=_params(("parallel", "arbitrary")),
        name="dilated_mixer",
    )(proj, proj, proj, proj, proj, bias)


def _mla_prep_kernel(cq_ref, ckv_ref, kr_ref, qag_ref, kvag_ref, wq_ref, wk_ref, wv_ref,
                     qg_ref, kg_ref, c_ref, s1_ref, s2_ref, q_out, k_out, v_out):
    c, s1, s2 = c_ref[...], s1_ref[...], s2_ref[...]

    def rope(x):
        return x * c + pltpu.roll(x, 96, 1) * s1 + pltpu.roll(x, 32, 1) * s2

    cqn = _rms(cq_ref[...], qag_ref[...]).astype(BF16)
    ckvn = _rms(ckv_ref[...], kvag_ref[...]).astype(BF16)
    qpre = jnp.dot(cqn, wq_ref[...], preferred_element_type=F32)
    knope = jnp.dot(ckvn, wk_ref[...], preferred_element_type=F32)
    v = jnp.dot(ckvn, wv_ref[...], preferred_element_type=F32)
    kr = kr_ref[...]
    kr_ss = jnp.sum(kr * kr, axis=-1, keepdims=True)
    qg, kg = qg_ref[...], kg_ref[...]
    kr_roped = rope(kr * kg[:, NOPE_DIM:])
    inv_n = 1.0 / QK_DIM_B
    ones = jnp.ones((VT_ROWS - V_DIM, cq_ref.shape[0]), BF16)
    for h in range(N_HEADS_B):
        qh = qpre[:, h * QK_PAD_B:(h + 1) * QK_PAD_B]
        rs = lax.rsqrt(jnp.sum(qh * qh, axis=-1, keepdims=True) * inv_n + EPS)
        q_out[h, 0:NOPE_DIM, :] = (qh[:, :NOPE_DIM] * rs * qg[:, :NOPE_DIM]).T.astype(BF16)
        q_out[h, NOPE_DIM:QK_PAD_B, :] = rope(qh[:, NOPE_DIM:] * rs * qg[:, NOPE_DIM:]).T.astype(BF16)
        kh = knope[:, h * NOPE_DIM:(h + 1) * NOPE_DIM]
        rs = lax.rsqrt((jnp.sum(kh * kh, axis=-1, keepdims=True) + kr_ss) * inv_n + EPS)
        k_out[h, :, 0:NOPE_DIM] = (kh * rs * kg[:, :NOPE_DIM]).astype(BF16)
        k_out[h, :, NOPE_DIM:QK_PAD_B] = (kr_roped * rs).astype(BF16)
        v_out[h, 0:V_DIM, :] = v[:, h * V_DIM:(h + 1) * V_DIM].T.astype(BF16)
        v_out[h, V_DIM:VT_ROWS, :] = ones


def mla_prep(proj, q_a_g, kv_a_g, wq, wk, wv, qg, kg, rope_c, rope_s1, rope_s2, *, tm):
    s = proj.shape[0]
    hb = N_HEADS_B
    full = lambda shape: pl.BlockSpec(shape, lambda i: (0,) * len(shape))
    cq_blk = (3 * WIDTH_A) // Q_LORA
    ckv_blk = (3 * WIDTH_A + Q_LORA) // KV_LORA
    kr_blk = (3 * WIDTH_A + Q_LORA + KV_LORA) // 128
    return pl.pallas_call(
        _mla_prep_kernel,
        grid=(s // tm,),
        in_specs=[
            pl.BlockSpec((tm, Q_LORA), lambda i: (i, cq_blk)),
            pl.BlockSpec((tm, KV_LORA), lambda i: (i, ckv_blk)),
            pl.BlockSpec((tm, 128), lambda i: (i, kr_blk)),
            full((1, Q_LORA)), full((1, KV_LORA)),
            full(wq.shape), full(wk.shape), full(wv.shape),
            full((1, QK_PAD_B)), full((1, QK_PAD_B)),
            pl.BlockSpec((tm, 128), lambda i: (i, 0)),
            pl.BlockSpec((tm, 128), lambda i: (i, 0)),
            pl.BlockSpec((tm, 128), lambda i: (i, 0)),
        ],
        out_specs=[
            pl.BlockSpec((hb, QK_PAD_B, tm), lambda i: (0, 0, i)),
            pl.BlockSpec((hb, tm, QK_PAD_B), lambda i: (0, i, 0)),
            pl.BlockSpec((hb, VT_ROWS, tm), lambda i: (0, 0, i)),
        ],
        out_shape=[
            jax.ShapeDtypeStruct((hb, QK_PAD_B, s), BF16),
            jax.ShapeDtypeStruct((hb, s, QK_PAD_B), BF16),
            jax.ShapeDtypeStruct((hb, VT_ROWS, s), BF16),
        ],
        compiler_params=_params(("parallel",)),
        name="mla_prep",
    )(proj, proj, proj, q_a_g, kv_a_g, wq, wk, wv, qg, kg, rope_c, rope_s1, rope_s2)


def _flash_kernel(qt_ref, k_ref, vt_ref, o_ref, s_a, s_b, m_s, acc_s, *, tq, tk):
    i = pl.program_id(1)
    m_s[...] = jnp.full(m_s.shape, NEG, F32)
    acc_s[...] = jnp.zeros(acc_s.shape, F32)
    per_tile = tq // tk
    assert per_tile % 2 == 0

    def compute(c, dst, q_lo=0):
        start = pl.multiple_of(c * tk, tk)
        dst[:, q_lo:] = jnp.dot(k_ref[0, pl.ds(start, tk), :], qt_ref[0, :, q_lo:],
                                preferred_element_type=F32)

    def process(c, src, q_lo=0, masked=False):
        s = src[:, q_lo:]
        if masked:
            key = c * tk + lax.broadcasted_iota(jnp.int32, s.shape, 0)
            qry = i * tq + q_lo + lax.broadcasted_iota(jnp.int32, s.shape, 1)
            s = jnp.where(key <= qry, s, NEG)
        m_prev = m_s[:, q_lo:]
        m_new = jnp.maximum(m_prev, jnp.max(s, axis=0, keepdims=True))
        alpha = jnp.exp2(m_prev - m_new)
        p = jnp.exp2(s - m_new).astype(BF16)
        start = pl.multiple_of(c * tk, tk)
        pv = jnp.dot(vt_ref[0, :, pl.ds(start, tk)], p, preferred_element_type=F32)
        acc_s[:, q_lo:] = alpha * acc_s[:, q_lo:] + pv
        m_s[:, q_lo:] = m_new

    bufs = (s_a, s_b)
    compute(0, s_a)
    n_full = per_tile * i

    def trip(t, carry):
        for u in range(per_tile):
            compute(t * per_tile + u + 1, bufs[(u + 1) % 2])
            process(t * per_tile + u, bufs[u % 2])
        return carry

    lax.fori_loop(0, i, trip, 0)
    for d in range(per_tile):
        if d + 1 < per_tile:
            compute(n_full + d + 1, bufs[(d + 1) % 2], (d + 1) * tk)
        process(n_full + d, bufs[d % 2], d * tk, masked=True)
    acc = acc_s[...]
    o_t = acc[0:V_DIM, :] / acc[V_DIM:V_DIM + 1, :]
    o_ref[...] = o_t.T.astype(o_ref.dtype)


def mla_flash(qt, k, vt, *, tq, tk):
    hb, s, _ = k.shape
    return pl.pallas_call(
        functools.partial(_flash_kernel, tq=tq, tk=tk),
        grid=(hb, s // tq),
        in_specs=[
            pl.BlockSpec((1, QK_PAD_B, tq), lambda h, i: (h, 0, i)),
            pl.BlockSpec((1, s, QK_PAD_B), lambda h, i: (h, 0, 0)),
            pl.BlockSpec((1, VT_ROWS, s), lambda h, i: (h, 0, 0)),
        ],
        out_specs=pl.BlockSpec((tq, V_DIM), lambda h, i: (i, h)),
        out_shape=jax.ShapeDtypeStruct((s, WIDTH_B), BF16),
        scratch_shapes=[
            pltpu.VMEM((tk, tq), F32),
            pltpu.VMEM((tk, tq), F32),
            pltpu.VMEM((1, tq), F32),
            pltpu.VMEM((VT_ROWS, tq), F32),
        ],
        compiler_params=_params(("parallel", "arbitrary")),
        name="mla_flash",
    )(qt, k, vt)


def _out_proj_kernel(x_ref, a_ref, b_ref, wa_ref, wb_ref, o_ref):
    acc = jnp.dot(a_ref[...], wa_ref[...], preferred_element_type=F32)
    acc = acc + jnp.dot(b_ref[...], wb_ref[...], preferred_element_type=F32)
    o_ref[...] = x_ref[...] + acc


def out_proj(x, a, b, wa, wb, *, tm, tn):
    s, n = x.shape
    ka, kb = a.shape[1], b.shape[1]
    return pl.pallas_call(
        _out_proj_kernel,
        grid=(s // tm, n // tn),
        in_specs=[
            pl.BlockSpec((tm, tn), lambda i, j: (i, j)),
            pl.BlockSpec((tm, ka), lambda i, j: (i, 0)),
            pl.BlockSpec((tm, kb), lambda i, j: (i, 0)),
            pl.BlockSpec((ka, tn), lambda i, j: (0, j)),
            pl.BlockSpec((kb, tn), lambda i, j: (0, j)),
        ],
        out_specs=pl.BlockSpec((tm, tn), lambda i, j: (i, j)),
        out_shape=jax.ShapeDtypeStruct((s, n), F32),
        compiler_params=_params(("parallel", "arbitrary")),
        name="out_proj",
    )(x, a, b, wa, wb)


def _swiglu_step(h, wg, wu, wd):
    gate = jnp.dot(h, wg, preferred_element_type=F32)
    up = jnp.dot(h, wu, preferred_element_type=F32)
    act = gate * jax.nn.sigmoid(gate) * up
    return jnp.dot(act.astype(BF16), wd, preferred_element_type=F32)


def _ffn_kernel(x_ref, g_ref, wg_ref, wu_ref, wd_ref, o_ref, hn_ref):
    @pl.when(pl.program_id(1) == 0)
    def _():
        hn_ref[...] = _rms(x_ref[...], g_ref[...]).astype(BF16)
        o_ref[...] = x_ref[...]

    o_ref[...] += _swiglu_step(hn_ref[...], wg_ref[...], wu_ref[...], wd_ref[...])


def ffn(x, g, wg, wu, wd, *, tm, tf):
    s, d = x.shape
    fdim = wg.shape[1]
    return pl.pallas_call(
        _ffn_kernel,
        grid=(s // tm, fdim // tf),
        in_specs=[
            pl.BlockSpec((tm, d), lambda i, f: (i, 0)),
            pl.BlockSpec((1, d), lambda i, f: (0, 0)),
            pl.BlockSpec((d, tf), lambda i, f: (0, f)),
            pl.BlockSpec((d, tf), lambda i, f: (0, f)),
            pl.BlockSpec((tf, d), lambda i, f: (f, 0)),
        ],
        out_specs=pl.BlockSpec((tm, d), lambda i, f: (i, 0)),
        out_shape=jax.ShapeDtypeStruct((s, d), F32),
        scratch_shapes=[pltpu.VMEM((tm, d), BF16)],
        compiler_params=_params(("parallel", "arbitrary")),
        name="ffn",
    )(x, g, wg, wu, wd)


def _pack_rows(y, dst):
    n = y.shape[0]
    for s in range(ROW_SUB):
        lo = y[:, s * 128:(s + 1) * 128].astype(BF16).astype(F32)
        hi = y[:, HALF + s * 128:HALF + (s + 1) * 128].astype(BF16).astype(F32)
        w = (lax.bitcast_convert_type(lo, U32) >> 16) | lax.bitcast_convert_type(hi, U32)
        dst[pl.ds(s, n, stride=ROW_SUB), :] = w


def _unpack_rows(src, n):
    lo, hi = [], []
    for s in range(ROW_SUB):
        w = src[pl.ds(s, n, stride=ROW_SUB), :]
        lo.append(lax.bitcast_convert_type(w << 16, F32))
        hi.append(lax.bitcast_convert_type(w & jnp.uint32(0xFFFF0000), F32))
    return lo + hi


META_I1, META_I2, META_R1, META_R2, META_G1, META_G2 = range(6)


def _router_kernel(x_ref, g_ref, wr_ref, hp_ref, meta_ref, cnt_ref, carry):
    @pl.when(pl.program_id(0) == 0)
    def _():
        carry[...] = jnp.zeros(carry.shape, F32)

    hn = _rms(x_ref[...], g_ref[...])
    _pack_rows(hn, hp_ref)
    h_hi = hn.astype(BF16)
    h_lo = (hn - h_hi.astype(F32)).astype(BF16)
    w_hi, w_lo = wr_ref[0], wr_ref[1]
    logits = (jnp.dot(h_hi, w_hi, preferred_element_type=F32)
              + jnp.dot(h_hi, w_lo, preferred_element_type=F32)
              + jnp.dot(h_lo, w_hi, preferred_element_type=F32))
    lane = lax.broadcasted_iota(jnp.int32, logits.shape, 1)
    logits = jnp.where(lane < N_EXPERTS, logits, NEG)
    v1 = jnp.max(logits, axis=-1, keepdims=True)
    i1 = jnp.min(jnp.where(logits == v1, lane, 128), axis=-1, keepdims=True)
    rest = jnp.where(lane == i1, NEG, logits)
    v2 = jnp.max(rest, axis=-1, keepdims=True)
    i2 = jnp.min(jnp.where(rest == v2, lane, 128), axis=-1, keepdims=True)
    e2 = jnp.exp(v2 - v1)
    g1 = 1.0 / (1.0 + e2)
    g2 = e2 / (1.0 + e2)

    tm = hn.shape[0]
    member = jnp.where(jnp.logical_or(lane == i1, lane == i2), 1.0, 0.0)
    row = lax.broadcasted_iota(jnp.int32, (tm, tm), 0)
    col = lax.broadcasted_iota(jnp.int32, (tm, tm), 1)
    earlier = jnp.where(col < row, 1.0, 0.0).astype(BF16)
    rank = carry[...] + jnp.dot(earlier, member.astype(BF16), preferred_element_type=F32)
    r1 = jnp.sum(jnp.where(lane == i1, rank, 0.0), axis=-1, keepdims=True)
    r2 = jnp.sum(jnp.where(lane == i2, rank, 0.0), axis=-1, keepdims=True)
    carry[...] += jnp.sum(member, axis=0, keepdims=True)
    cnt_ref[...] = carry[...]
    meta = jnp.zeros(logits.shape, F32)
    for pos, val in ((META_I1, i1.astype(F32)), (META_I2, i2.astype(F32)), (META_R1, r1),
                     (META_R2, r2), (META_G1, g1), (META_G2, g2)):
        meta = jnp.where(lane == pos, val, meta)
    meta_ref[...] = meta


def router(x, g, wr, *, tm):
    s, d = x.shape
    return pl.pallas_call(
        _router_kernel,
        grid=(s // tm,),
        in_specs=[
            pl.BlockSpec((tm, d), lambda i: (i, 0)),
            pl.BlockSpec((1, d), lambda i: (0, 0)),
            pl.BlockSpec((2, d, 128), lambda i: (0, 0, 0)),
        ],
        out_specs=[
            pl.BlockSpec((tm * ROW_SUB, 128), lambda i: (i, 0)),
            pl.BlockSpec((tm, 128), lambda i: (i, 0)),
            pl.BlockSpec((1, 128), lambda i: (0, 0)),
        ],
        out_shape=[
            jax.ShapeDtypeStruct((s * ROW_SUB, 128), U32),
            jax.ShapeDtypeStruct((s, 128), F32),
            jax.ShapeDtypeStruct((1, 128), F32),
        ],
        scratch_shapes=[pltpu.VMEM((1, 128), F32)],
        compiler_params=_params(("arbitrary",)),
        name="router",
    )(x, g, wr)


def _row_tile(ref, r):
    return ref.at[pl.ds(pl.multiple_of(r * ROW_SUB, ROW_SUB), ROW_SUB), :]


def _dispatch_kernel(dest_ref, hp_hbm, xs_in, xs_out, stage, in_sem, out_sem, *, tm):
    del xs_in
    i = pl.program_id(0)
    last = pl.num_programs(0) - 1
    block_rows = tm * ROW_SUB

    def fetch(step):
        src = hp_hbm.at[pl.ds(pl.multiple_of(step * block_rows, block_rows), block_rows), :]
        return pltpu.make_async_copy(src, stage.at[step % 3], in_sem.at[step % 3])

    def drain(step):
        for _ in range(TOP_K):
            pltpu.make_async_copy(stage.at[step % 3], xs_out.at[pl.ds(0, block_rows), :],
                                  out_sem.at[step % 2]).wait()

    @pl.when(i == 0)
    def _():
        fetch(0).start()

    @pl.when(i < last)
    def _():
        fetch(i + 1).start()

    fetch(i).wait()
    src = stage.at[i % 3]

    def issue(t, c):
        for k in range(TOP_K):
            pltpu.make_async_copy(_row_tile(src, t), _row_tile(xs_out, dest_ref[2 * (i * tm + t) + k]),
                                  out_sem.at[i % 2]).start()
        return c

    lax.fori_loop(0, tm, issue, 0)

    @pl.when(i > 0)
    def _():
        drain(i - 1)

    @pl.when(i == last)
    def _():
        drain(i)


def dispatch(dest, hp, n_rows, *, tm):
    s = hp.shape[0] // ROW_SUB
    xs0 = jnp.zeros((n_rows * ROW_SUB, 128), U32)
    return pl.pallas_call(
        functools.partial(_dispatch_kernel, tm=tm),
        grid_spec=pltpu.PrefetchScalarGridSpec(
            num_scalar_prefetch=1,
            grid=(s // tm,),
            in_specs=[pl.BlockSpec(memory_space=pl.ANY), pl.BlockSpec(memory_space=pl.ANY)],
            out_specs=pl.BlockSpec(memory_space=pl.ANY),
            scratch_shapes=[pltpu.VMEM((3, tm * ROW_SUB, 128), U32),
                            pltpu.SemaphoreType.DMA((3,)), pltpu.SemaphoreType.DMA((2,))],
        ),
        out_shape=jax.ShapeDtypeStruct(xs0.shape, U32),
        input_output_aliases={2: 0},
        compiler_params=_params(("arbitrary",)),
        name="moe_dispatch",
    )(dest, hp, xs0)


def _moe_ffn_kernel(te_ref, nu_ref, fill_ref, xs_ref, wg_ref, wu_ref, wd_ref, ys_ref, xb, acc, *, tm):
    del te_ref, nu_ref
    f = pl.program_id(1)
    fill = fill_ref[pl.program_id(0)]
    used = fill > 0
    last = f == pl.num_programs(1) - 1

    @pl.when(jnp.logical_and(used, f == 0))
    def _():
        for c, v in enumerate(_unpack_rows(xs_ref, tm)):
            xb[:, c * 128:(c + 1) * 128] = v.astype(BF16)
        acc[...] = jnp.zeros(acc.shape, F32)

    for groups in range(1, tm // MOE_ROW_GROUP + 1):
        @pl.when(fill == groups)
        def _(rows=groups * MOE_ROW_GROUP):
            acc[0:rows, :] += _swiglu_step(xb[0:rows, :], wg_ref[0], wu_ref[0], wd_ref[0])

    @pl.when(jnp.logical_and(used, last))
    def _():
        _pack_rows(acc[...], ys_ref)

    @pl.when(jnp.logical_and(jnp.logical_not(used), last))
    def _():
        ys_ref[...] = jnp.zeros(ys_ref.shape, U32)


def moe_ffn(tile_expert, n_used, tile_fill, xs, wg, wu, wd, *, tm, tf):
    nt = tile_expert.shape[0]
    _, d, fdim = wg.shape
    nf = fdim // tf

    def f_idx(j, f, nu):
        return jnp.where(j < nu[0], f, nf - 1)

    return pl.pallas_call(
        functools.partial(_moe_ffn_kernel, tm=tm),
        grid_spec=pltpu.PrefetchScalarGridSpec(
            num_scalar_prefetch=3,
            grid=(nt, nf),
            in_specs=[
                pl.BlockSpec((tm * ROW_SUB, 128), lambda j, f, te, nu, fl: (jnp.minimum(j, nu[0] - 1), 0)),
                pl.BlockSpec((1, d, tf), lambda j, f, te, nu, fl: (te[j], 0, f_idx(j, f, nu))),
                pl.BlockSpec((1, d, tf), lambda j, f, te, nu, fl: (te[j], 0, f_idx(j, f, nu))),
                pl.BlockSpec((1, tf, d), lambda j, f, te, nu, fl: (te[j], f_idx(j, f, nu), 0)),
            ],
            out_specs=pl.BlockSpec((tm * ROW_SUB, 128), lambda j, f, te, nu, fl: (j, 0)),
            scratch_shapes=[pltpu.VMEM((tm, d), BF16), pltpu.VMEM((tm, d), F32)],
        ),
        out_shape=jax.ShapeDtypeStruct(xs.shape, U32),
        compiler_params=_params(("parallel", "arbitrary")),
        name="moe_ffn",
    )(tile_expert, n_used, tile_fill, xs, wg, wu, wd)


def _combine_kernel(dest_ref, x_ref, meta_ref, ys_hbm, o_ref, buf, sem, *, tm):
    i = pl.program_id(0)

    def issue_step(step, slot):
        def issue(t, c):
            for k in range(TOP_K):
                pltpu.make_async_copy(_row_tile(ys_hbm, dest_ref[2 * (step * tm + t) + k]),
                                      _row_tile(buf.at[slot, k], t), sem.at[slot]).start()
            return c
        lax.fori_loop(0, tm, issue, 0)

    @pl.when(i == 0)
    def _():
        issue_step(0, 0)

    @pl.when(i + 1 < pl.num_programs(0))
    def _():
        issue_step(i + 1, (i + 1) % 2)

    slot = i % 2
    for k in range(TOP_K):
        pltpu.make_async_copy(ys_hbm.at[pl.ds(0, tm * ROW_SUB), :], buf.at[slot, k], sem.at[slot]).wait()
    meta = meta_ref[...]
    g1 = meta[:, META_G1:META_G1 + 1]
    g2 = meta[:, META_G2:META_G2 + 1]
    y1 = _unpack_rows(buf.at[slot, 0], tm)
    y2 = _unpack_rows(buf.at[slot, 1], tm)
    for c in range(len(y1)):
        sl = slice(c * 128, (c + 1) * 128)
        o_ref[:, sl] = x_ref[:, sl] + g1 * y1[c] + g2 * y2[c]


def combine(dest, x, meta, ys, *, tm):
    s, d = x.shape
    return pl.pallas_call(
        functools.partial(_combine_kernel, tm=tm),
        grid_spec=pltpu.PrefetchScalarGridSpec(
            num_scalar_prefetch=1,
            grid=(s // tm,),
            in_specs=[
                pl.BlockSpec((tm, d), lambda i, dest: (i, 0)),
                pl.BlockSpec((tm, 128), lambda i, dest: (i, 0)),
                pl.BlockSpec(memory_space=pl.ANY),
            ],
            out_specs=pl.BlockSpec((tm, d), lambda i, dest: (i, 0)),
            scratch_shapes=[pltpu.VMEM((2, TOP_K, tm * ROW_SUB, 128), U32), pltpu.SemaphoreType.DMA((2,))],
        ),
        out_shape=jax.ShapeDtypeStruct((s, d), F32),
        compiler_params=_params(("arbitrary",)),
        name="moe_combine",
    )(dest, x, meta, ys)


def moe(x, g, w_router, wg, wu, wd, *, tm_rows, tf):
    s = x.shape[0]
    wr = _pad_cols(w_router.astype(F32), 128)
    wr_hi = wr.astype(BF16)
    wr_lo = (wr - wr_hi.astype(F32)).astype(BF16)
    hp, meta, cnt = router(x, g, jnp.stack([wr_hi, wr_lo]), tm=T.route_rows)
    i1, i2 = meta[:, META_I1].astype(jnp.int32), meta[:, META_I2].astype(jnp.int32)
    r1, r2 = meta[:, META_R1].astype(jnp.int32), meta[:, META_R2].astype(jnp.int32)
    counts = cnt[0, :N_EXPERTS].astype(jnp.int32)
    padded = (counts + tm_rows - 1) // tm_rows * tm_rows
    ends = jnp.cumsum(padded)
    offs = ends - padded
    eids = jnp.arange(N_EXPERTS, dtype=jnp.int32)
    off1 = jnp.sum(jnp.where(i1[:, None] == eids, offs, 0), axis=1)
    off2 = jnp.sum(jnp.where(i2[:, None] == eids, offs, 0), axis=1)
    dest = jnp.stack([off1 + r1, off2 + r2], axis=1).reshape(-1)
    n_tiles = (TOP_K * s) // tm_rows + N_EXPERTS
    n_used = (ends[-1] // tm_rows).reshape(1)
    tile_start = jnp.minimum(jnp.arange(n_tiles, dtype=jnp.int32), n_used - 1) * tm_rows
    tile_expert = jnp.sum(tile_start[:, None] >= ends[None, :], axis=1).astype(jnp.int32)
    tile_ids = jnp.arange(n_tiles, dtype=jnp.int32)
    rows_end = jnp.sum(jnp.where(tile_expert[:, None] == eids, offs + counts, 0), axis=1)
    tile_rows = jnp.where(tile_ids < n_used, jnp.clip(rows_end - tile_ids * tm_rows, 0, tm_rows), 0)
    tile_fill = ((tile_rows + MOE_ROW_GROUP - 1) // MOE_ROW_GROUP).astype(jnp.int32)
    xs = dispatch(dest, hp, n_tiles * tm_rows, tm=T.move_rows)
    ys = moe_ffn(tile_expert, n_used, tile_fill, xs, wg, wu, wd, tm=tm_rows, tf=tf)
    return combine(dest, x, meta, ys, tm=T.move_rows)


def _pad_cols(w, n):
    return jnp.pad(w, ((0, 0), (0, n - w.shape[1])))


def _mla_weights(w_q_b, w_kv_b):
    wq = w_q_b.reshape(Q_LORA, N_HEADS_B, QK_DIM_B)
    wq = jnp.pad(wq, ((0, 0), (0, 0), (0, QK_PAD_B - QK_DIM_B))).reshape(Q_LORA, N_HEADS_B * QK_PAD_B)
    wkv = w_kv_b.reshape(KV_LORA, N_HEADS_B, NOPE_DIM + V_DIM)
    wk = wkv[:, :, :NOPE_DIM].reshape(KV_LORA, N_HEADS_B * NOPE_DIM)
    wv = wkv[:, :, NOPE_DIM:].reshape(KV_LORA, N_HEADS_B * V_DIM)
    return wq.astype(BF16), wk.astype(BF16), wv.astype(BF16)


def _rope_tables(positions):
    half = ROPE_DIM // 2
    inv = ROPE_THETA ** (-jnp.arange(half, dtype=F32) / half)
    ang = positions.astype(F32)[:, None] * inv
    cos, sin = jnp.cos(ang), jnp.sin(ang)
    z = jnp.zeros_like(cos)
    c = jnp.concatenate([cos, cos, z, z], axis=-1)
    s1 = jnp.concatenate([-sin, z, z, z], axis=-1)
    s2 = jnp.concatenate([z, sin, z, z], axis=-1)
    return c, s1, s2


def kernel(x, positions, rel_bias_table, norm_mix_g, w_in, q_a_norm_g, kv_a_norm_g, w_q_b, w_kv_b, q_norm_a_g, k_norm_a_g, q_norm_b_g, k_norm_b_g, w_out, norm_ffn_g, w_ff_gate, w_ff_up, w_ff_down, w_router, w_exp_gate, w_exp_up, w_exp_down):
    batch, seq, d = x.shape
    depth = w_in.shape[0]
    outs = []
    bias = _dilated_bias(rel_bias_table)
    for bi in range(batch):
        xs = x.reshape(seq, d) if batch == 1 else x[bi]
        rope_c, rope_s1, rope_s2 = _rope_tables(positions[bi])
        for l in range(depth):
            w_main = w_in[l][:, :3 * WIDTH_A].astype(BF16)
            w_tail = _pad_cols(w_in[l][:, 3 * WIDTH_A:], PROJ_COLS - 3 * WIDTH_A).astype(BF16)
            head_g = jnp.concatenate([jnp.tile(q_norm_a_g[l] * (HEAD_DIM ** -0.5 * LOG2E), N_HEADS_A),
                                      jnp.tile(k_norm_a_g[l], N_HEADS_A)])[None]
            proj = norm_matmul(xs, norm_mix_g[l][None], w_main, w_tail, head_g, tm=T.proj_rows,
                               out_dtype=F32)
            a = dilated_mixer(proj, bias)
            wq, wk, wv = _mla_weights(w_q_b[l], w_kv_b[l])
            qg = _pad_cols(q_norm_b_g[l][None] * (QK_DIM_B ** -0.5 * LOG2E), QK_PAD_B)
            kg = _pad_cols(k_norm_b_g[l][None], QK_PAD_B)
            qb, kb, vb = mla_prep(proj, q_a_norm_g[l][None], kv_a_norm_g[l][None], wq, wk, wv,
                                  qg, kg, rope_c, rope_s1, rope_s2, tm=T.prep_rows)
            b = mla_flash(qb, kb, vb, tq=T.flash_q, tk=T.flash_k)
            w_out_l = w_out[l].astype(BF16)
            xs = out_proj(xs, a, b, w_out_l[:WIDTH_A], w_out_l[WIDTH_A:], tm=T.out_rows, tn=T.out_cols)
            gf = norm_ffn_g[l][None]
            if l % 2 == 0:
                i = l // 2
                xs = ffn(xs, gf, w_ff_gate[i].astype(BF16), w_ff_up[i].astype(BF16),
                         w_ff_down[i].astype(BF16), tm=T.ffn_rows, tf=T.ffn_cols)
            else:
                i = l // 2
                xs = moe(xs, gf, w_router[i], w_exp_gate[i].astype(BF16),
                         w_exp_up[i].astype(BF16), w_exp_down[i].astype(BF16),
                         tm_rows=T.ffn_rows, tf=T.ffn_cols)
        outs.append(xs)
    return outs[0].reshape(1, seq, d) if batch == 1 else jnp.stack(outs, axis=0)
```

```python
import functools
import math
from typing import NamedTuple

import jax
import jax.numpy as jnp
from jax import lax
from jax.experimental import pallas as pl
from jax.experimental.pallas import tpu as pltpu

F32 = jnp.float32
BF16 = jnp.bfloat16

D_MODEL = 2048
HEAD_DIM = 128
N_HEADS_A = 8
DILATED_PATTERNS = ((128, 1), (512, 4), (2048, 16))
BLK = 128
N_BUCKETS = 32
MAX_DISTANCE = 2048
N_HEADS_B = 8
Q_LORA = 512
KV_LORA = 256
NOPE_DIM = 128
ROPE_DIM = 64
V_DIM = 128
QK_DIM_B = NOPE_DIM + ROPE_DIM
QK_PAD_B = 256
VT_ROWS = V_DIM + 16
LOG2E = math.log2(math.e)
ROPE_THETA = 10000.0
WIDTH_A = N_HEADS_A * HEAD_DIM
WIDTH_B = N_HEADS_B * V_DIM
IN_COLS = 3 * WIDTH_A + Q_LORA + KV_LORA + ROPE_DIM
PROJ_COLS = 4096
D_FF = 5632
N_EXPERTS = 8
TOP_K = 2
ROW_SUB = 8
MOE_ROW_GROUP = 256
HALF = D_MODEL // 2
U32 = jnp.uint32
EPS = 1e-6
NEG = -1e30

SUPER = 2048
DEINT = 4
VMEM_LIMIT = 56 * 1024 * 1024


class _Tiles(NamedTuple):
    proj_rows: int = 1024
    prep_rows: int = 512
    flash_q: int = 2048
    flash_k: int = 512
    out_rows: int = 1024
    out_cols: int = 1024
    ffn_rows: int = 512
    moe_rows: int = 1024
    ffn_cols: int = 512
    route_rows: int = 512
    move_rows: int = 256


T = _Tiles()


def _params(sem, vmem=VMEM_LIMIT):
    return pltpu.CompilerParams(dimension_semantics=sem, vmem_limit_bytes=vmem)


def _rms(x, g, n=None):
    ss = jnp.sum(x * x, axis=-1, keepdims=True)
    n = x.shape[-1] if n is None else n
    return x * lax.rsqrt(ss * (1.0 / n) + EPS) * g


def _norm_matmul_kernel(x_ref, g_ref, w_ref, wt_ref, hg_ref, o_ref, hn_ref, *, n_main, n_headnorm):
    j = pl.program_id(1)

    @pl.when(j == 0)
    def _():
        hn_ref[...] = _rms(x_ref[...], g_ref[...]).astype(BF16)

    @pl.when(j < n_headnorm)
    def _():
        acc = jnp.dot(hn_ref[...], w_ref[...], preferred_element_type=F32)
        for c in range(acc.shape[1] // HEAD_DIM):
            sl = slice(c * HEAD_DIM, (c + 1) * HEAD_DIM)
            o_ref[:, sl] = _rms(acc[:, sl], hg_ref[:, sl]).astype(o_ref.dtype)

    @pl.when(jnp.logical_and(j >= n_headnorm, j < n_main))
    def _():
        o_ref[...] = jnp.dot(hn_ref[...], w_ref[...], preferred_element_type=F32).astype(o_ref.dtype)

    @pl.when(j == n_main)
    def _():
        o_ref[...] = jnp.dot(hn_ref[...], wt_ref[...], preferred_element_type=F32).astype(o_ref.dtype)


def norm_matmul(x, g, w, w_tail, head_g, *, tm, out_dtype):
    s, k = x.shape
    tn = w_tail.shape[1]
    n_main = w.shape[1] // tn
    n_headnorm = head_g.shape[1] // tn
    return pl.pallas_call(
        functools.partial(_norm_matmul_kernel, n_main=n_main, n_headnorm=n_headnorm),
        grid=(s // tm, n_main + 1),
        in_specs=[
            pl.BlockSpec((tm, k), lambda i, j: (i, 0)),
            pl.BlockSpec((1, k), lambda i, j: (0, 0)),
            pl.BlockSpec((k, tn), lambda i, j: (0, jnp.minimum(j, n_main - 1))),
            pl.BlockSpec((k, tn), lambda i, j: (0, 0)),
            pl.BlockSpec((1, tn), lambda i, j: (0, jnp.minimum(j, n_headnorm - 1))),
        ],
        out_specs=pl.BlockSpec((tm, tn), lambda i, j: (i, j)),
        out_shape=jax.ShapeDtypeStruct((s, (n_main + 1) * tn), out_dtype),
        scratch_shapes=[pltpu.VMEM((tm, k), BF16)],
        compiler_params=_params(("parallel", "arbitrary")),
        name="norm_matmul",
    )(x, g, w, w_tail, head_g)


def _t5_bucket(n):
    max_exact = N_BUCKETS // 2
    nf = jnp.maximum(n, 1).astype(F32)
    large = max_exact + (jnp.log(nf / max_exact) / math.log(MAX_DISTANCE / max_exact)
                         * (N_BUCKETS - max_exact)).astype(jnp.int32)
    large = jnp.minimum(large, N_BUCKETS - 1)
    return jnp.where(n < max_exact, n, large)


def _dilated_bias(rel_bias):
    i = jnp.arange(BLK)[:, None]
    j = jnp.arange(2 * BLK)[None, :]
    delta = i + BLK - j
    out = []
    for window, dilation in DILATED_PATTERNS:
        band = (delta >= 0) & (delta <= window // dilation)
        bucket = _t5_bucket(jnp.maximum(delta, 0) * dilation)
        onehot = (bucket[None] == jnp.arange(N_BUCKETS)[:, None, None]).astype(F32)
        bias = jnp.einsum('nij,nh->hij', onehot, rel_bias.astype(F32), precision=lax.Precision.HIGHEST)
        out.append(jnp.where(band[None], bias * LOG2E, NEG))
    return jnp.stack(out, axis=0)


def _dilated_kernel(q_ref, kp_ref, kc_ref, vp_ref, vc_ref, bias_ref, o_ref,
                    q4_s, k4_s, v4_s, op_s, lse_s, out_s):
    sb = pl.program_id(1)
    qseg, kseg = SUPER // DEINT, 2 * SUPER // DEINT
    for r in range(DEINT):
        grp = pl.ds(r, qseg, stride=DEINT)
        q4_s[r * qseg:(r + 1) * qseg, :] = q_ref[grp, :]
        for src_p, src_c, dst in ((kp_ref, kc_ref, k4_s), (vp_ref, vc_ref, v4_s)):
            dst[r * kseg:r * kseg + qseg, :] = src_p[grp, :]
            dst[r * kseg + qseg:(r + 1) * kseg, :] = src_c[grp, :]

    def load_block(d, r, b):
        if d == 1:
            q = q_ref[b * BLK:(b + 1) * BLK, :]
            if b == 0:
                k = jnp.concatenate([kp_ref[SUPER - BLK:SUPER, :], kc_ref[0:BLK, :]], axis=0)
                v = jnp.concatenate([vp_ref[SUPER - BLK:SUPER, :], vc_ref[0:BLK, :]], axis=0)
            else:
                k = kc_ref[(b - 1) * BLK:(b + 1) * BLK, :]
                v = vc_ref[(b - 1) * BLK:(b + 1) * BLK, :]
        elif d == DEINT:
            q = q4_s[r * qseg + b * BLK:r * qseg + (b + 1) * BLK, :]
            k0 = r * kseg + qseg + (b - 1) * BLK
            k, v = k4_s[k0:k0 + 2 * BLK, :], v4_s[k0:k0 + 2 * BLK, :]
        else:
            assert d == DEINT * DEINT and b == 0
            g, c = r % DEINT, r // DEINT
            q = q4_s[pl.ds(g * qseg + c, BLK, stride=DEINT), :]
            k = k4_s[pl.ds(g * kseg + c, 2 * BLK, stride=DEINT), :]
            v = v4_s[pl.ds(g * kseg + c, 2 * BLK, stride=DEINT), :]
        return q.astype(BF16), k.astype(BF16), v.astype(BF16)

    col = lax.broadcasted_iota(jnp.int32, (BLK, 2 * BLK), 1)
    no_prev = jnp.logical_and(sb == 0, col < BLK)

    def out_rows(d, r, b):
        if d == 1:
            return pl.ds(b * BLK, BLK)
        if d == DEINT:
            return pl.ds(r * qseg + b * BLK, BLK)
        return pl.ds((r % DEINT) * qseg + r // DEINT, BLK, stride=DEINT)

    for pi, (_, d) in enumerate(DILATED_PATTERNS):
        bias = bias_ref[pi, 0]
        bias_first = jnp.where(no_prev, NEG, bias)
        for r in range(d):
            for b in range(SUPER // (d * BLK)):
                qb, k2, v2 = load_block(d, r, b)
                s = lax.dot_general(qb, k2, (((1,), (1,)), ((), ())), preferred_element_type=F32)
                s = s + (bias_first if b == 0 else bias)
                m = jnp.max(s, axis=-1, keepdims=True)
                p = jnp.exp2(s - m)
                den = jnp.sum(p, axis=-1, keepdims=True)
                o = jnp.dot(p.astype(BF16), v2, preferred_element_type=F32) / den
                lse = m + jnp.log2(den)
                op_s[pi, out_rows(d, r, b), :] = o
                lse_s[pi, out_rows(d, r, b), :] = jnp.broadcast_to(lse, (BLK, HEAD_DIM))

    chunk = 256
    for g in range(DEINT):
        for c in range(qseg // chunk):
            grouped = pl.ds(g * qseg + c * chunk, chunk)
            natural = pl.ds(g + DEINT * c * chunk, chunk, stride=DEINT)
            l0, l1, l2 = lse_s[0, natural, :], lse_s[1, grouped, :], lse_s[2, grouped, :]
            mx = jnp.maximum(jnp.maximum(l0, l1), l2)
            e0, e1, e2 = jnp.exp2(l0 - mx), jnp.exp2(l1 - mx), jnp.exp2(l2 - mx)
            num = e0 * op_s[0, natural, :] + e1 * op_s[1, grouped, :] + e2 * op_s[2, grouped, :]
            out_s[natural, :] = num / (e0 + e1 + e2)
    o_ref[...] = out_s[...].astype(o_ref.dtype)


def dilated_mixer(proj, bias):
    s = proj.shape[0]
    nsb = s // SUPER
    h8 = N_HEADS_A
    blk = (SUPER, HEAD_DIM)
    return pl.pallas_call(
        _dilated_kernel,
        grid=(h8, nsb),
        in_specs=[
            pl.BlockSpec(blk, lambda h, i: (i, h)),
            pl.BlockSpec(blk, lambda h, i: (jnp.maximum(i - 1, 0), h8 + h)),
            pl.BlockSpec(blk, lambda h, i: (i, h8 + h)),
            pl.BlockSpec(blk, lambda h, i: (jnp.maximum(i - 1, 0), 2 * h8 + h)),
            pl.BlockSpec(blk, lambda h, i: (i, 2 * h8 + h)),
            pl.BlockSpec((len(DILATED_PATTERNS), 1, BLK, 2 * BLK), lambda h, i: (0, h, 0, 0)),
        ],
        out_specs=pl.BlockSpec(blk, lambda h, i: (i, h)),
        out_shape=jax.ShapeDtypeStruct((s, WIDTH_A), BF16),
        scratch_shapes=[
            pltpu.VMEM((SUPER, HEAD_DIM), F32),
            pltpu.VMEM((2 * SUPER, HEAD_DIM), F32),
            pltpu.VMEM((2 * SUPER, HEAD_DIM), F32),
            pltpu.VMEM((len(DILATED_PATTERNS), SUPER, HEAD_DIM), F32),
            pltpu.VMEM((len(DILATED_PATTERNS), SUPER, HEAD_DIM), F32),
            pltpu.VMEM((SUPER, HEAD_DIM), F32),
        ],
        compiler_params=_params(("parallel", "arbitrary")),
        name="dilated_mixer",
    )(proj, proj, proj, proj, proj, bias)


def _mla_prep_kernel(cq_ref, ckv_ref, kr_ref, qag_ref, kvag_ref, wq_ref, wk_ref, wv_ref,
                     qg_ref, kg_ref, c_ref, s1_ref, s2_ref, q_out, k_out, v_out):
    c, s1, s2 = c_ref[...], s1_ref[...], s2_ref[...]

    def rope(x):
        return x * c + pltpu.roll(x, 96, 1) * s1 + pltpu.roll(x, 32, 1) * s2

    cqn = _rms(cq_ref[...], qag_ref[...]).astype(BF16)
    ckvn = _rms(ckv_ref[...], kvag_ref[...]).astype(BF16)
    qpre = jnp.dot(cqn, wq_ref[...], preferred_element_type=F32)
    knope = jnp.dot(ckvn, wk_ref[...], preferred_element_type=F32)
    v = jnp.dot(ckvn, wv_ref[...], preferred_element_type=F32)
    kr = kr_ref[...]
    kr_ss = jnp.sum(kr * kr, axis=-1, keepdims=True)
    qg, kg = qg_ref[...], kg_ref[...]
    kr_roped = rope(kr * kg[:, NOPE_DIM:])
    inv_n = 1.0 / QK_DIM_B
    ones = jnp.ones((VT_ROWS - V_DIM, cq_ref.shape[0]), BF16)
    for h in range(N_HEADS_B):
        qh = qpre[:, h * QK_PAD_B:(h + 1) * QK_PAD_B]
        rs = lax.rsqrt(jnp.sum(qh * qh, axis=-1, keepdims=True) * inv_n + EPS)
        q_out[h, 0:NOPE_DIM, :] = (qh[:, :NOPE_DIM] * rs * qg[:, :NOPE_DIM]).T.astype(BF16)
        q_out[h, NOPE_DIM:QK_PAD_B, :] = rope(qh[:, NOPE_DIM:] * rs * qg[:, NOPE_DIM:]).T.astype(BF16)
        kh = knope[:, h * NOPE_DIM:(h + 1) * NOPE_DIM]
        rs = lax.rsqrt((jnp.sum(kh * kh, axis=-1, keepdims=True) + kr_ss) * inv_n + EPS)
        k_out[h, :, 0:NOPE_DIM] = (kh * rs * kg[:, :NOPE_DIM]).astype(BF16)
        k_out[h, :, NOPE_DIM:QK_PAD_B] = (kr_roped * rs).astype(BF16)
        v_out[h, 0:V_DIM, :] = v[:, h * V_DIM:(h + 1) * V_DIM].T.astype(BF16)
        v_out[h, V_DIM:VT_ROWS, :] = ones


def mla_prep(proj, q_a_g, kv_a_g, wq, wk, wv, qg, kg, rope_c, rope_s1, rope_s2, *, tm):
    s = proj.shape[0]
    hb = N_HEADS_B
    full = lambda shape: pl.BlockSpec(shape, lambda i: (0,) * len(shape))
    cq_blk = (3 * WIDTH_A) // Q_LORA
    ckv_blk = (3 * WIDTH_A + Q_LORA) // KV_LORA
    kr_blk = (3 * WIDTH_A + Q_LORA + KV_LORA) // 128
    return pl.pallas_call(
        _mla_prep_kernel,
        grid=(s // tm,),
        in_specs=[
            pl.BlockSpec((tm, Q_LORA), lambda i: (i, cq_blk)),
            pl.BlockSpec((tm, KV_LORA), lambda i: (i, ckv_blk)),
            pl.BlockSpec((tm, 128), lambda i: (i, kr_blk)),
            full((1, Q_LORA)), full((1, KV_LORA)),
            full(wq.shape), full(wk.shape), full(wv.shape),
            full((1, QK_PAD_B)), full((1, QK_PAD_B)),
            pl.BlockSpec((tm, 128), lambda i: (i, 0)),
            pl.BlockSpec((tm, 128), lambda i: (i, 0)),
            pl.BlockSpec((tm, 128), lambda i: (i, 0)),
        ],
        out_specs=[
            pl.BlockSpec((hb, QK_PAD_B, tm), lambda i: (0, 0, i)),
            pl.BlockSpec((hb, tm, QK_PAD_B), lambda i: (0, i, 0)),
            pl.BlockSpec((hb, VT_ROWS, tm), lambda i: (0, 0, i)),
        ],
        out_shape=[
            jax.ShapeDtypeStruct((hb, QK_PAD_B, s), BF16),
            jax.ShapeDtypeStruct((hb, s, QK_PAD_B), BF16),
            jax.ShapeDtypeStruct((hb, VT_ROWS, s), BF16),
        ],
        compiler_params=_params(("parallel",)),
        name="mla_prep",
    )(proj, proj, proj, q_a_g, kv_a_g, wq, wk, wv, qg, kg, rope_c, rope_s1, rope_s2)


def _flash_kernel(qt_ref, k_ref, vt_ref, o_ref, s_a, s_b, m_s, acc_s, *, tq, tk):
    i = pl.program_id(1)
    m_s[...] = jnp.full(m_s.shape, NEG, F32)
    acc_s[...] = jnp.zeros(acc_s.shape, F32)
    per_tile = tq // tk
    assert per_tile % 2 == 0

    def compute(c, dst, q_lo=0):
        start = pl.multiple_of(c * tk, tk)
        dst[:, q_lo:] = jnp.dot(k_ref[0, pl.ds(start, tk), :], qt_ref[0, :, q_lo:],
                                preferred_element_type=F32)

    def process(c, src, q_lo=0, masked=False):
        s = src[:, q_lo:]
        if masked:
            key = c * tk + lax.broadcasted_iota(jnp.int32, s.shape, 0)
            qry = i * tq + q_lo + lax.broadcasted_iota(jnp.int32, s.shape, 1)
            s = jnp.where(key <= qry, s, NEG)
        m_prev = m_s[:, q_lo:]
        m_new = jnp.maximum(m_prev, jnp.max(s, axis=0, keepdims=True))
        alpha = jnp.exp2(m_prev - m_new)
        p = jnp.exp2(s - m_new).astype(BF16)
        start = pl.multiple_of(c * tk, tk)
        pv = jnp.dot(vt_ref[0, :, pl.ds(start, tk)], p, preferred_element_type=F32)
        acc_s[:, q_lo:] = alpha * acc_s[:, q_lo:] + pv
        m_s[:, q_lo:] = m_new

    bufs = (s_a, s_b)
    compute(0, s_a)
    n_full = per_tile * i

    def trip(t, carry):
        for u in range(per_tile):
            compute(t * per_tile + u + 1, bufs[(u + 1) % 2])
            process(t * per_tile + u, bufs[u % 2])
        return carry

    lax.fori_loop(0, i, trip, 0)
    for d in range(per_tile):
        if d + 1 < per_tile:
            compute(n_full + d + 1, bufs[(d + 1) % 2], (d + 1) * tk)
        process(n_full + d, bufs[d % 2], d * tk, masked=True)
    acc = acc_s[...]
    o_t = acc[0:V_DIM, :] / acc[V_DIM:V_DIM + 1, :]
    o_ref[...] = o_t.T.astype(o_ref.dtype)


def mla_flash(qt, k, vt, *, tq, tk):
    hb, s, _ = k.shape
    return pl.pallas_call(
        functools.partial(_flash_kernel, tq=tq, tk=tk),
        grid=(hb, s // tq),
        in_specs=[
            pl.BlockSpec((1, QK_PAD_B, tq), lambda h, i: (h, 0, i)),
            pl.BlockSpec((1, s, QK_PAD_B), lambda h, i: (h, 0, 0)),
            pl.BlockSpec((1, VT_ROWS, s), lambda h, i: (h, 0, 0)),
        ],
        out_specs=pl.BlockSpec((tq, V_DIM), lambda h, i: (i, h)),
        out_shape=jax.ShapeDtypeStruct((s, WIDTH_B), BF16),
        scratch_shapes=[
            pltpu.VMEM((tk, tq), F32),
            pltpu.VMEM((tk, tq), F32),
            pltpu.VMEM((1, tq), F32),
            pltpu.VMEM((VT_ROWS, tq), F32),
        ],
        compiler_params=_params(("parallel", "arbitrary")),
        name="mla_flash",
    )(qt, k, vt)


def _out_proj_kernel(x_ref, a_ref, b_ref, wa_ref, wb_ref, o_ref):
    acc = jnp.dot(a_ref[...], wa_ref[...], preferred_element_type=F32)
    acc = acc + jnp.dot(b_ref[...], wb_ref[...], preferred_element_type=F32)
    o_ref[...] = x_ref[...] + acc


def out_proj(x, a, b, wa, wb, *, tm, tn):
    s, n = x.shape
    ka, kb = a.shape[1], b.shape[1]
    return pl.pallas_call(
        _out_proj_kernel,
        grid=(s // tm, n // tn),
        in_specs=[
            pl.BlockSpec((tm, tn), lambda i, j: (i, j)),
            pl.BlockSpec((tm, ka), lambda i, j: (i, 0)),
            pl.BlockSpec((tm, kb), lambda i, j: (i, 0)),
            pl.BlockSpec((ka, tn), lambda i, j: (0, j)),
            pl.BlockSpec((kb, tn), lambda i, j: (0, j)),
        ],
        out_specs=pl.BlockSpec((tm, tn), lambda i, j: (i, j)),
        out_shape=jax.ShapeDtypeStruct((s, n), F32),
        compiler_params=_params(("parallel", "arbitrary")),
        name="out_proj",
    )(x, a, b, wa, wb)


def _swiglu_step(h, wg, wu, wd):
    gate = jnp.dot(h, wg, preferred_element_type=F32)
    up = jnp.dot(h, wu, preferred_element_type=F32)
    act = gate * jax.nn.sigmoid(gate) * up
    return jnp.dot(act.astype(BF16), wd, preferred_element_type=F32)


def _ffn_kernel(x_ref, g_ref, wg_ref, wu_ref, wd_ref, o_ref, hn_ref):
    @pl.when(pl.program_id(1) == 0)
    def _():
        hn_ref[...] = _rms(x_ref[...], g_ref[...]).astype(BF16)
        o_ref[...] = x_ref[...]

    o_ref[...] += _swiglu_step(hn_ref[...], wg_ref[...], wu_ref[...], wd_ref[...])


def ffn(x, g, wg, wu, wd, *, tm, tf):
    s, d = x.shape
    fdim = wg.shape[1]
    return pl.pallas_call(
        _ffn_kernel,
        grid=(s // tm, fdim // tf),
        in_specs=[
            pl.BlockSpec((tm, d), lambda i, f: (i, 0)),
            pl.BlockSpec((1, d), lambda i, f: (0, 0)),
            pl.BlockSpec((d, tf), lambda i, f: (0, f)),
            pl.BlockSpec((d, tf), lambda i, f: (0, f)),
            pl.BlockSpec((tf, d), lambda i, f: (f, 0)),
        ],
        out_specs=pl.BlockSpec((tm, d), lambda i, f: (i, 0)),
        out_shape=jax.ShapeDtypeStruct((s, d), F32),
        scratch_shapes=[pltpu.VMEM((tm, d), BF16)],
        compiler_params=_params(("parallel", "arbitrary")),
        name="ffn",
    )(x, g, wg, wu, wd)


def _pack_rows(y, dst):
    n = y.shape[0]
    for s in range(ROW_SUB):
        lo = y[:, s * 128:(s + 1) * 128].astype(BF16).astype(F32)
        hi = y[:, HALF + s * 128:HALF + (s + 1) * 128].astype(BF16).astype(F32)
        w = (lax.bitcast_convert_type(lo, U32) >> 16) | lax.bitcast_convert_type(hi, U32)
        dst[pl.ds(s, n, stride=ROW_SUB), :] = w


def _unpack_rows(src, n):
    lo, hi = [], []
    for s in range(ROW_SUB):
        w = src[pl.ds(s, n, stride=ROW_SUB), :]
        lo.append(lax.bitcast_convert_type(w << 16, F32))
        hi.append(lax.bitcast_convert_type(w & jnp.uint32(0xFFFF0000), F32))
    return lo + hi


META_I1, META_I2, META_R1, META_R2, META_G1, META_G2 = range(6)


def _router_kernel(x_ref, g_ref, wr_ref, hp_ref, meta_ref, cnt_ref, carry):
    @pl.when(pl.program_id(0) == 0)
    def _():
        carry[...] = jnp.zeros(carry.shape, F32)

    hn = _rms(x_ref[...], g_ref[...])
    _pack_rows(hn, hp_ref)
    h_hi = hn.astype(BF16)
    h_lo = (hn - h_hi.astype(F32)).astype(BF16)
    w_hi, w_lo = wr_ref[0], wr_ref[1]
    logits = (jnp.dot(h_hi, w_hi, preferred_element_type=F32)
              + jnp.dot(h_hi, w_lo, preferred_element_type=F32)
              + jnp.dot(h_lo, w_hi, preferred_element_type=F32))
    lane = lax.broadcasted_iota(jnp.int32, logits.shape, 1)
    logits = jnp.where(lane < N_EXPERTS, logits, NEG)
    v1 = jnp.max(logits, axis=-1, keepdims=True)
    i1 = jnp.min(jnp.where(logits == v1, lane, 128), axis=-1, keepdims=True)
    rest = jnp.where(lane == i1, NEG, logits)
    v2 = jnp.max(rest, axis=-1, keepdims=True)
    i2 = jnp.min(jnp.where(rest == v2, lane, 128), axis=-1, keepdims=True)
    e2 = jnp.exp(v2 - v1)
    g1 = 1.0 / (1.0 + e2)
    g2 = e2 / (1.0 + e2)

    tm = hn.shape[0]
    member = jnp.where(jnp.logical_or(lane == i1, lane == i2), 1.0, 0.0)
    row = lax.broadcasted_iota(jnp.int32, (tm, tm), 0)
    col = lax.broadcasted_iota(jnp.int32, (tm, tm), 1)
    earlier = jnp.where(col < row, 1.0, 0.0).astype(BF16)
    rank = carry[...] + jnp.dot(earlier, member.astype(BF16), preferred_element_type=F32)
    r1 = jnp.sum(jnp.where(lane == i1, rank, 0.0), axis=-1, keepdims=True)
    r2 = jnp.sum(jnp.where(lane == i2, rank, 0.0), axis=-1, keepdims=True)
    carry[...] += jnp.sum(member, axis=0, keepdims=True)
    cnt_ref[...] = carry[...]
    meta = jnp.zeros(logits.shape, F32)
    for pos, val in ((META_I1, i1.astype(F32)), (META_I2, i2.astype(F32)), (META_R1, r1),
                     (META_R2, r2), (META_G1, g1), (META_G2, g2)):
        meta = jnp.where(lane == pos, val, meta)
    meta_ref[...] = meta


def router(x, g, wr, *, tm):
    s, d = x.shape
    return pl.pallas_call(
        _router_kernel,
        grid=(s // tm,),
        in_specs=[
            pl.BlockSpec((tm, d), lambda i: (i, 0)),
            pl.BlockSpec((1, d), lambda i: (0, 0)),
            pl.BlockSpec((2, d, 128), lambda i: (0, 0, 0)),
        ],
        out_specs=[
            pl.BlockSpec((tm * ROW_SUB, 128), lambda i: (i, 0)),
            pl.BlockSpec((tm, 128), lambda i: (i, 0)),
            pl.BlockSpec((1, 128), lambda i: (0, 0)),
        ],
        out_shape=[
            jax.ShapeDtypeStruct((s * ROW_SUB, 128), U32),
            jax.ShapeDtypeStruct((s, 128), F32),
            jax.ShapeDtypeStruct((1, 128), F32),
        ],
        scratch_shapes=[pltpu.VMEM((1, 128), F32)],
        compiler_params=_params(("arbitrary",)),
        name="router",
    )(x, g, wr)


def _row_tile(ref, r):
    return ref.at[pl.ds(pl.multiple_of(r * ROW_SUB, ROW_SUB), ROW_SUB), :]


def _dispatch_kernel(dest_ref, hp_hbm, xs_in, xs_out, stage, in_sem, out_sem, *, tm):
    del xs_in
    i = pl.program_id(0)
    last = pl.num_programs(0) - 1
    block_rows = tm * ROW_SUB

    def fetch(step):
        src = hp_hbm.at[pl.ds(pl.multiple_of(step * block_rows, block_rows), block_rows), :]
        return pltpu.make_async_copy(src, stage.at[step % 3], in_sem.at[step % 3])

    def drain(step):
        for _ in range(TOP_K):
            pltpu.make_async_copy(stage.at[step % 3], xs_out.at[pl.ds(0, block_rows), :],
                                  out_sem.at[step % 2]).wait()

    @pl.when(i == 0)
    def _():
        fetch(0).start()

    @pl.when(i < last)
    def _():
        fetch(i + 1).start()

    fetch(i).wait()
    src = stage.at[i % 3]

    def issue(t, c):
        for k in range(TOP_K):
            pltpu.make_async_copy(_row_tile(src, t), _row_tile(xs_out, dest_ref[2 * (i * tm + t) + k]),
                                  out_sem.at[i % 2]).start()
        return c

    lax.fori_loop(0, tm, issue, 0)

    @pl.when(i > 0)
    def _():
        drain(i - 1)

    @pl.when(i == last)
    def _():
        drain(i)


def dispatch(dest, hp, n_rows, *, tm):
    s = hp.shape[0] // ROW_SUB
    xs0 = jnp.zeros((n_rows * ROW_SUB, 128), U32)
    return pl.pallas_call(
        functools.partial(_dispatch_kernel, tm=tm),
        grid_spec=pltpu.PrefetchScalarGridSpec(
            num_scalar_prefetch=1,
            grid=(s // tm,),
            in_specs=[pl.BlockSpec(memory_space=pl.ANY), pl.BlockSpec(memory_space=pl.ANY)],
            out_specs=pl.BlockSpec(memory_space=pl.ANY),
            scratch_shapes=[pltpu.VMEM((3, tm * ROW_SUB, 128), U32),
                            pltpu.SemaphoreType.DMA((3,)), pltpu.SemaphoreType.DMA((2,))],
        ),
        out_shape=jax.ShapeDtypeStruct(xs0.shape, U32),
        input_output_aliases={2: 0},
        compiler_params=_params(("arbitrary",)),
        name="moe_dispatch",
    )(dest, hp, xs0)


def _moe_ffn_kernel(te_ref, nu_ref, fill_ref, xs_ref, wg_ref, wu_ref, wd_ref, ys_ref, xb, acc, *, tm):
    del te_ref, nu_ref
    f = pl.program_id(1)
    fill = fill_ref[pl.program_id(0)]
    used = fill > 0
    last = f == pl.num_programs(1) - 1

    @pl.when(jnp.logical_and(used, f == 0))
    def _():
        for c, v in enumerate(_unpack_rows(xs_ref, tm)):
            xb[:, c * 128:(c + 1) * 128] = v.astype(BF16)
        acc[...] = jnp.zeros(acc.shape, F32)

    for groups in range(1, tm // MOE_ROW_GROUP + 1):
        @pl.when(fill == groups)
        def _(rows=groups * MOE_ROW_GROUP):
            acc[0:rows, :] += _swiglu_step(xb[0:rows, :], wg_ref[0], wu_ref[0], wd_ref[0])

    @pl.when(jnp.logical_and(used, last))
    def _():
        _pack_rows(acc[...], ys_ref)

    @pl.when(jnp.logical_and(jnp.logical_not(used), last))
    def _():
        ys_ref[...] = jnp.zeros(ys_ref.shape, U32)


def moe_ffn(tile_expert, n_used, tile_fill, xs, wg, wu, wd, *, tm, tf):
    nt = tile_expert.shape[0]
    _, d, fdim = wg.shape
    nf = fdim // tf

    def f_idx(j, f, nu):
        return jnp.where(j < nu[0], f, nf - 1)

    return pl.pallas_call(
        functools.partial(_moe_ffn_kernel, tm=tm),
        grid_spec=pltpu.PrefetchScalarGridSpec(
            num_scalar_prefetch=3,
            grid=(nt, nf),
            in_specs=[
                pl.BlockSpec((tm * ROW_SUB, 128), lambda j, f, te, nu, fl: (jnp.minimum(j, nu[0] - 1), 0)),
                pl.BlockSpec((1, d, tf), lambda j, f, te, nu, fl: (te[j], 0, f_idx(j, f, nu))),
                pl.BlockSpec((1, d, tf), lambda j, f, te, nu, fl: (te[j], 0, f_idx(j, f, nu))),
                pl.BlockSpec((1, tf, d), lambda j, f, te, nu, fl: (te[j], f_idx(j, f, nu), 0)),
            ],
            out_specs=pl.BlockSpec((tm * ROW_SUB, 128), lambda j, f, te, nu, fl: (j, 0)),
            scratch_shapes=[pltpu.VMEM((tm, d), BF16), pltpu.VMEM((tm, d), F32)],
        ),
        out_shape=jax.ShapeDtypeStruct(xs.shape, U32),
        compiler_params=_params(("parallel", "arbitrary")),
        name="moe_ffn",
    )(tile_expert, n_used, tile_fill, xs, wg, wu, wd)


def _combine_kernel(dest_ref, x_ref, meta_ref, ys_hbm, o_ref, buf, sem, *, tm):
    i = pl.program_id(0)

    def issue_step(step, slot):
        def issue(t, c):
            for k in range(TOP_K):
                pltpu.make_async_copy(_row_tile(ys_hbm, dest_ref[2 * (step * tm + t) + k]),
                                      _row_tile(buf.at[slot, k], t), sem.at[slot]).start()
            return c
        lax.fori_loop(0, tm, issue, 0)

    @pl.when(i == 0)
    def _():
        issue_step(0, 0)

    @pl.when(i + 1 < pl.num_programs(0))
    def _():
        issue_step(i + 1, (i + 1) % 2)

    slot = i % 2
    for k in range(TOP_K):
        pltpu.make_async_copy(ys_hbm.at[pl.ds(0, tm * ROW_SUB), :], buf.at[slot, k], sem.at[slot]).wait()
    meta = meta_ref[...]
    g1 = meta[:, META_G1:META_G1 + 1]
    g2 = meta[:, META_G2:META_G2 + 1]
    y1 = _unpack_rows(buf.at[slot, 0], tm)
    y2 = _unpack_rows(buf.at[slot, 1], tm)
    for c in range(len(y1)):
        sl = slice(c * 128, (c + 1) * 128)
        o_ref[:, sl] = x_ref[:, sl] + g1 * y1[c] + g2 * y2[c]


def combine(dest, x, meta, ys, *, tm):
    s, d = x.shape
    return pl.pallas_call(
        functools.partial(_combine_kernel, tm=tm),
        grid_spec=pltpu.PrefetchScalarGridSpec(
            num_scalar_prefetch=1,
            grid=(s // tm,),
            in_specs=[
                pl.BlockSpec((tm, d), lambda i, dest: (i, 0)),
                pl.BlockSpec((tm, 128), lambda i, dest: (i, 0)),
                pl.BlockSpec(memory_space=pl.ANY),
            ],
            out_specs=pl.BlockSpec((tm, d), lambda i, dest: (i, 0)),
            scratch_shapes=[pltpu.VMEM((2, TOP_K, tm * ROW_SUB, 128), U32), pltpu.SemaphoreType.DMA((2,))],
        ),
        out_shape=jax.ShapeDtypeStruct((s, d), F32),
        compiler_params=_params(("arbitrary",)),
        name="moe_combine",
    )(dest, x, meta, ys)


def moe(x, g, w_router, wg, wu, wd, *, tm_rows, tf):
    s = x.shape[0]
    wr = _pad_cols(w_router.astype(F32), 128)
    wr_hi = wr.astype(BF16)
    wr_lo = (wr - wr_hi.astype(F32)).astype(BF16)
    hp, meta, cnt = router(x, g, jnp.stack([wr_hi, wr_lo]), tm=T.route_rows)
    i1, i2 = meta[:, META_I1].astype(jnp.int32), meta[:, META_I2].astype(jnp.int32)
    r1, r2 = meta[:, META_R1].astype(jnp.int32), meta[:, META_R2].astype(jnp.int32)
    counts = cnt[0, :N_EXPERTS].astype(jnp.int32)
    padded = (counts + tm_rows - 1) // tm_rows * tm_rows
    ends = jnp.cumsum(padded)
    offs = ends - padded
    eids = jnp.arange(N_EXPERTS, dtype=jnp.int32)
    off1 = jnp.sum(jnp.where(i1[:, None] == eids, offs, 0), axis=1)
    off2 = jnp.sum(jnp.where(i2[:, None] == eids, offs, 0), axis=1)
    dest = jnp.stack([off1 + r1, off2 + r2], axis=1).reshape(-1)
    n_tiles = (TOP_K * s) // tm_rows + N_EXPERTS
    n_used = (ends[-1] // tm_rows).reshape(1)
    tile_start = jnp.minimum(jnp.arange(n_tiles, dtype=jnp.int32), n_used - 1) * tm_rows
    tile_expert = jnp.sum(tile_start[:, None] >= ends[None, :], axis=1).astype(jnp.int32)
    tile_ids = jnp.arange(n_tiles, dtype=jnp.int32)
    rows_end = jnp.sum(jnp.where(tile_expert[:, None] == eids, offs + counts, 0), axis=1)
    tile_rows = jnp.where(tile_ids < n_used, jnp.clip(rows_end - tile_ids * tm_rows, 0, tm_rows), 0)
    tile_fill = ((tile_rows + MOE_ROW_GROUP - 1) // MOE_ROW_GROUP).astype(jnp.int32)
    xs = dispatch(dest, hp, n_tiles * tm_rows, tm=T.move_rows)
    ys = moe_ffn(tile_expert, n_used, tile_fill, xs, wg, wu, wd, tm=tm_rows, tf=tf)
    return combine(dest, x, meta, ys, tm=T.move_rows)


def _pad_cols(w, n):
    return jnp.pad(w, ((0, 0), (0, n - w.shape[1])))


def _mla_weights(w_q_b, w_kv_b):
    wq = w_q_b.reshape(Q_LORA, N_HEADS_B, QK_DIM_B)
    wq = jnp.pad(wq, ((0, 0), (0, 0), (0, QK_PAD_B - QK_DIM_B))).reshape(Q_LORA, N_HEADS_B * QK_PAD_B)
    wkv = w_kv_b.reshape(KV_LORA, N_HEADS_B, NOPE_DIM + V_DIM)
    wk = wkv[:, :, :NOPE_DIM].reshape(KV_LORA, N_HEADS_B * NOPE_DIM)
    wv = wkv[:, :, NOPE_DIM:].reshape(KV_LORA, N_HEADS_B * V_DIM)
    return wq.astype(BF16), wk.astype(BF16), wv.astype(BF16)


def _rope_tables(positions):
    half = ROPE_DIM // 2
    inv = ROPE_THETA ** (-jnp.arange(half, dtype=F32) / half)
    ang = positions.astype(F32)[:, None] * inv
    cos, sin = jnp.cos(ang), jnp.sin(ang)
    z = jnp.zeros_like(cos)
    c = jnp.concatenate([cos, cos, z, z], axis=-1)
    s1 = jnp.concatenate([-sin, z, z, z], axis=-1)
    s2 = jnp.concatenate([z, sin, z, z], axis=-1)
    return c, s1, s2


def kernel(x, positions, rel_bias_table, norm_mix_g, w_in, q_a_norm_g, kv_a_norm_g, w_q_b, w_kv_b, q_norm_a_g, k_norm_a_g, q_norm_b_g, k_norm_b_g, w_out, norm_ffn_g, w_ff_gate, w_ff_up, w_ff_down, w_router, w_exp_gate, w_exp_up, w_exp_down):
    batch, seq, d = x.shape
    depth = w_in.shape[0]
    outs = []
    bias = _dilated_bias(rel_bias_table)
    for bi in range(batch):
        xs = x.reshape(seq, d) if batch == 1 else x[bi]
        rope_c, rope_s1, rope_s2 = _rope_tables(positions[bi])
        for l in range(depth):
            w_main = w_in[l][:, :3 * WIDTH_A].astype(BF16)
            w_tail = _pad_cols(w_in[l][:, 3 * WIDTH_A:], PROJ_COLS - 3 * WIDTH_A).astype(BF16)
            head_g = jnp.concatenate([jnp.tile(q_norm_a_g[l] * (HEAD_DIM ** -0.5 * LOG2E), N_HEADS_A),
                                      jnp.tile(k_norm_a_g[l], N_HEADS_A)])[None]
            proj = norm_matmul(xs, norm_mix_g[l][None], w_main, w_tail, head_g, tm=T.proj_rows,
                               out_dtype=F32)
            a = dilated_mixer(proj, bias)
            wq, wk, wv = _mla_weights(w_q_b[l], w_kv_b[l])
            qg = _pad_cols(q_norm_b_g[l][None] * (QK_DIM_B ** -0.5 * LOG2E), QK_PAD_B)
            kg = _pad_cols(k_norm_b_g[l][None], QK_PAD_B)
            qb, kb, vb = mla_prep(proj, q_a_norm_g[l][None], kv_a_norm_g[l][None], wq, wk, wv,
                                  qg, kg, rope_c, rope_s1, rope_s2, tm=T.prep_rows)
            b = mla_flash(qb, kb, vb, tq=T.flash_q, tk=T.flash_k)
            w_out_l = w_out[l].astype(BF16)
            xs = out_proj(xs, a, b, w_out_l[:WIDTH_A], w_out_l[WIDTH_A:], tm=T.out_rows, tn=T.out_cols)
            gf = norm_ffn_g[l][None]
            if l % 2 == 0:
                i = l // 2
                xs = ffn(xs, gf, w_ff_gate[i].astype(BF16), w_ff_up[i].astype(BF16),
                         w_ff_down[i].astype(BF16), tm=T.ffn_rows, tf=T.ffn_cols)
            else:
                i = l // 2
                xs = moe(xs, gf, w_router[i], w_exp_gate[i].astype(BF16),
                         w_exp_up[i].astype(BF16), w_exp_down[i].astype(BF16),
                         tm_rows=T.moe_rows, tf=T.ffn_cols)
        outs.append(xs)
    return outs[0].reshape(1, seq, d) if batch == 1 else jnp.stack(outs, axis=0)
```
